```python
import math
import jax, jax.numpy as jnp
from jax import lax
import numpy as np

D_MODEL = 1024
BATCH = 1
SEQ = 16384
DEPTH = 4
DEC_BATCH = 32
DEC_SEQ = 32
PAST_LEN = 4096

CHUNK = 64
N_META = 16
N_MIXERS = 4
EPS = 1e-6
D_FF = ((8 * D_MODEL + 3 * 256 - 1) // (3 * 256)) * 256

MB_D_INNER = 2 * D_MODEL
MB_HEAD_DIM = 64
MB_HEADS = MB_D_INNER // MB_HEAD_DIM
MB_D_STATE = 128
MB_GROUPS = 4
MB_CONV = 4
MB_CONV_DIM = MB_D_INNER + 2 * MB_GROUPS * MB_D_STATE
MB_IN_DIM = MB_D_INNER + MB_CONV_DIM + MB_HEADS

GDN_DK = 128
GDN_DV = 128
GDN_HK = D_MODEL // GDN_DK
GDN_HV = 2 * GDN_HK
GDN_CONV = 4
GDN_KEY_DIM = GDN_HK * GDN_DK
GDN_VAL_DIM = GDN_HV * GDN_DV
GDN_CONV_DIM = 2 * GDN_KEY_DIM + GDN_VAL_DIM
GDN_IN_DIM = GDN_CONV_DIM + GDN_VAL_DIM + 2 * GDN_HV

RW_HEAD_DIM = 64
RW_HEADS = D_MODEL // RW_HEAD_DIM
RW_DECAY_LORA = 64
RW_ICL_LORA = 64
RW_GATE_LORA = 128
RW_GN_EPS = 64e-5
RW_DECAY_SCALE = 0.6065306597126334

SB_HEADS = 16
SB_HEAD_DIM = D_MODEL // SB_HEADS
SB_Q_BLOCK = 128

kernel_name = 'hybrid_stream_ssd_gdn_rwkv7_stickbreak'


def rmsnorm(x, g):
    xf = x.astype(jnp.float32)
    y = xf * lax.rsqrt(jnp.mean(xf * xf, axis=-1, keepdims=True) + EPS)
    return (y * g.astype(jnp.float32)).astype(x.dtype)


def l2norm(x):
    return x * lax.rsqrt(jnp.sum(x * x, axis=-1, keepdims=True) + 1e-6)


def swiglu(x, w_gu, w_down):
    g, u = jnp.split(x @ w_gu, 2, axis=-1)
    return (jax.nn.silu(g) * u) @ w_down


def causal_conv(u, buf, w, b=None):
    width = w.shape[0]
    t_len = u.shape[1]
    full = jnp.concatenate([buf.astype(u.dtype), u], axis=1)
    out = full[:, 0:t_len] * w[0]
    for i in range(1, width):
        out = out + full[:, i:i + t_len] * w[i]
    if b is not None:
        out = out + b
    return out, full[:, -(width - 1):]


def pad_time(a, n):
    return jnp.pad(a, [(0, 0), (0, n)] + [(0, 0)] * (a.ndim - 2))


def to_chunks(a, n_chunks):
    a = pad_time(a, n_chunks * CHUNK - a.shape[1])
    return a.reshape((a.shape[0], n_chunks, CHUNK) + a.shape[2:]).swapaxes(0, 1)


def from_chunks(a, length):
    a = a.swapaxes(0, 1)
    return a.reshape((a.shape[0], a.shape[1] * CHUNK) + a.shape[3:])[:, :length]


def ssd_chunked(x, dt, a, bm, cm, h0):
    t_len = x.shape[1]
    n_chunks = -(-t_len // CHUNK)
    idx = jnp.arange(CHUNK)
    incl = (idx[:, None] >= idx[None, :])[None, :, :, None, None]

    def step(h, inp):
        xc, dtc, bc, cc = inp
        acum = jnp.cumsum(dtc * a, axis=1)
        seg = acum[:, :, None] - acum[:, None, :]
        decay = jnp.where(incl, jnp.exp(jnp.where(incl, seg, 0.0)), 0.0)
        cb = jnp.einsum('btgn,bsgn->btsg', cc, bc)
        y = jnp.einsum('btsg,btsgh,bsgh,bsghp->btghp', cb, decay, dtc, xc)
        y = y + jnp.einsum('btgn,bghpn->btghp', cc, h) * jnp.exp(acum)[..., None]
        last = acum[:, -1]
        wts = jnp.exp(last[:, None] - acum) * dtc
        h = h * jnp.exp(last)[..., None, None] + jnp.einsum('bsgh,bsgn,bsghp->bghpn', wts, bc, xc)
        return h, y

    xs = tuple(to_chunks(t, n_chunks) for t in (x, dt, bm, cm))
    h, ys = lax.scan(step, h0, xs)
    return from_chunks(ys, t_len), h


def mamba2_mixer(h, conv_buf, ssm_state, w_in, conv_w, conv_b, dt_bias, a_log, d_skip, norm_w, w_out):
    bsz, t_len, _ = h.shape
    hg = MB_HEADS // MB_GROUPS
    z, xbc, dt = jnp.split(h @ w_in, [MB_D_INNER, MB_D_INNER + MB_CONV_DIM], axis=-1)
    xbc, conv_new = causal_conv(xbc, conv_buf, conv_w, conv_b)
    xbc = jax.nn.silu(xbc).astype(jnp.float32)
    xs, bm, cm = jnp.split(xbc, [MB_D_INNER, MB_D_INNER + MB_GROUPS * MB_D_STATE], axis=-1)
    xs = xs.reshape(bsz, t_len, MB_GROUPS, hg, MB_HEAD_DIM)
    bm = bm.reshape(bsz, t_len, MB_GROUPS, MB_D_STATE)
    cm = cm.reshape(bsz, t_len, MB_GROUPS, MB_D_STATE)
    dt = jax.nn.softplus(dt.astype(jnp.float32) + dt_bias.astype(jnp.float32))
    dt = dt.reshape(bsz, t_len, MB_GROUPS, hg)
    a = -jnp.exp(a_log.astype(jnp.float32)).reshape(MB_GROUPS, hg)
    h0 = ssm_state.astype(jnp.float32).reshape(bsz, MB_GROUPS, hg, MB_HEAD_DIM, MB_D_STATE)
    y, h_last = ssd_chunked(xs, dt, a, bm, cm, h0)
    y = y + xs * d_skip.astype(jnp.float32).reshape(MB_GROUPS, hg, 1)
    y = y.reshape(bsz, t_len, MB_GROUPS, -1) * jax.nn.silu(z.astype(jnp.float32)).reshape(bsz, t_len, MB_GROUPS, -1)
    y = y * lax.rsqrt(jnp.mean(y * y, axis=-1, keepdims=True) + EPS)
    y = (y.reshape(bsz, t_len, MB_D_INNER) * norm_w.astype(jnp.float32)).astype(h.dtype)
    new_state = h_last.reshape(bsz, MB_HEADS, MB_HEAD_DIM, MB_D_STATE).astype(ssm_state.dtype)
    return y @ w_out, conv_new, new_state


def gated_delta_chunked(q, k, v, g, beta, s0):
    t_len = q.shape[1]
    n_chunks = -(-t_len // CHUNK)
    idx = jnp.arange(CHUNK)
    incl = idx[:, None] >= idx[None, :]
    strict = idx[:, None] > idx[None, :]

    def step(s, inp):
        qc, kc, vc, gc, bc = inp
        gcum = jnp.cumsum(gc, axis=1).swapaxes(1, 2)
        seg = gcum[..., :, None] - gcum[..., None, :]
        dec = jnp.where(incl, jnp.exp(jnp.where(incl, seg, 0.0)), 0.0)
        bt = bc.swapaxes(1, 2)[..., None]
        kk = jnp.einsum('bthd,bshd->bhts', kc, kc)
        a_mat = jnp.where(strict, kk * dec, 0.0) * bt
        eg = jnp.exp(gcum)[..., None]
        rhs = bt * (vc.swapaxes(1, 2) - eg * jnp.einsum('bthk,bhkv->bhtv', kc, s))
        delta = lax.linalg.triangular_solve(a_mat, rhs, left_side=True, lower=True, unit_diagonal=True)
        qk = jnp.einsum('bthd,bshd->bhts', qc, kc) * dec
        o = eg * jnp.einsum('bthk,bhkv->bhtv', qc, s) + qk @ delta
        w_last = jnp.exp(gcum[..., -1:] - gcum)
        s = s * jnp.exp(gcum[..., -1])[..., None, None] + jnp.einsum('bhs,bshk,bhsv->bhkv', w_last, kc, delta)
        return s, o.swapaxes(1, 2)

    xs = tuple(to_chunks(t, n_chunks) for t in (q, k, v, g, beta))
    s, o = lax.scan(step, s0, xs)
    return from_chunks(o, t_len), s


def gdn_mixer(h, conv_buf, s0, w_in, conv_w, dt_bias, a_log, norm_w, w_out):
    bsz, t_len, _ = h.shape
    qkv, z, b, a = jnp.split(h @ w_in, [GDN_CONV_DIM, GDN_CONV_DIM + GDN_VAL_DIM,
                                       GDN_CONV_DIM + GDN_VAL_DIM + GDN_HV], axis=-1)
    qkv, conv_new = causal_conv(qkv, conv_buf, conv_w)
    qkv = jax.nn.silu(qkv).astype(jnp.float32)
    q, k, v = jnp.split(qkv, [GDN_KEY_DIM, 2 * GDN_KEY_DIM], axis=-1)
    rep = GDN_HV // GDN_HK
    q = jnp.repeat(l2norm(q.reshape(bsz, t_len, GDN_HK, GDN_DK)), rep, axis=2) * (GDN_DK ** -0.5)
    k = jnp.repeat(l2norm(k.reshape(bsz, t_len, GDN_HK, GDN_DK)), rep, axis=2)
    v = v.reshape(bsz, t_len, GDN_HV, GDN_DV)
    beta = jax.nn.sigmoid(b.astype(jnp.float32))
    g = -jnp.exp(a_log.astype(jnp.float32)) * jax.nn.softplus(a.astype(jnp.float32) + dt_bias.astype(jnp.float32))
    o, s_last = gated_delta_chunked(q, k, v, g, beta, s0.astype(jnp.float32))
    o = o * lax.rsqrt(jnp.mean(o * o, axis=-1, keepdims=True) + EPS) * norm_w.astype(jnp.float32)
    o = o * jax.nn.silu(z.astype(jnp.float32)).reshape(bsz, t_len, GDN_HV, GDN_DV)
    out = o.reshape(bsz, t_len, GDN_VAL_DIM).astype(h.dtype) @ w_out
    return out, conv_new, s_last.astype(s0.dtype)


def rwkv7_mixer(h, shift_buf, s0, mu, w_rkv, w0, w1, w2, a0, a1, a2, g1, g2,
                k_k, k_a, r_k, ln_w, ln_b, w_out):
    bsz, t_len, _ = h.shape
    prev = jnp.concatenate([shift_buf.astype(h.dtype), h[:, :-1]], axis=1)
    xm = h[None] + (prev - h)[None] * mu[:, None, None, :]
    rkv = jnp.einsum('cbtd,cde->cbte', xm[:3], w_rkv)
    r, k, v = rkv[0], rkv[1], rkv[2]
    decay = jnp.exp(-RW_DECAY_SCALE * jax.nn.sigmoid((w0 + jnp.tanh(xm[3] @ w1) @ w2).astype(jnp.float32)))
    a = jax.nn.sigmoid((a0 + (xm[4] @ a1) @ a2).astype(jnp.float32))
    g = jax.nn.sigmoid(xm[5] @ g1) @ g2

    def heads(t):
        return t.reshape(bsz, t_len, RW_HEADS, RW_HEAD_DIM).astype(jnp.float32)

    k_a_h = k_a.astype(jnp.float32).reshape(RW_HEADS, RW_HEAD_DIM)
    kk = l2norm(heads(k * k_k))
    k = heads(k) * (1.0 + (heads(a) - 1.0) * k_a_h)
    r, v, decay, a = heads(r), heads(v), heads(decay), heads(a)

    def step(s, inp):
        r_t, w_t, k_t, v_t, kk_t, a_t = inp
        sa = jnp.einsum('bhvk,bhk->bhv', s, -kk_t)
        s = (s * w_t[:, :, None, :] + sa[..., None] * (kk_t * a_t)[:, :, None, :]
             + v_t[..., None] * k_t[:, :, None, :])
        return s, jnp.einsum('bhvk,bhk->bhv', s, r_t)

    xs = tuple(t.swapaxes(0, 1) for t in (r, decay, k, v, kk, a))
    s_last, y = lax.scan(step, s0.astype(jnp.float32), xs)
    y = y.swapaxes(0, 1)
    mean = jnp.mean(y, axis=-1, keepdims=True)
    var = jnp.mean(jnp.square(y - mean), axis=-1, keepdims=True)
    y = ((y - mean) * lax.rsqrt(var + RW_GN_EPS)).reshape(bsz, t_len, D_MODEL) * ln_w + ln_b
    bonus = jnp.sum(r * k * r_k.astype(jnp.float32), axis=-1, keepdims=True) * v
    y = y + bonus.reshape(bsz, t_len, D_MODEL)
    out = (y * g).astype(h.dtype) @ w_out
    return out, h[:, -1:], s_last.astype(s0.dtype)


def stick_breaking_attend(q, k, v, q_offset):
    bsz, tq, n_h, d = q.shape
    tk = k.shape[1]
    qb = min(SB_Q_BLOCK, tq)
    nb = -(-tq // qb)
    qp = pad_time(q, nb * qb - tq).reshape(bsz, nb, qb, n_h, d).swapaxes(0, 1)
    kf = k.astype(jnp.float32)
    vf = v.astype(jnp.float32)
    s_idx = jnp.arange(tk)
    scale = d ** -0.5

    def block(inp):
        qblk, i = inp
        z = jnp.einsum('bqhd,bkhd->bhqk', qblk.astype(jnp.float32), kf) * scale
        t_abs = q_offset + i * qb + jnp.arange(qb)
        mask = s_idx[None, :] < t_abs[:, None]
        lnb = jnp.where(mask, jax.nn.log_sigmoid(-z), 0.0)
        between = lax.cumsum(lnb, axis=3, reverse=True) - lnb
        att = jnp.where(mask, jnp.exp(jax.nn.log_sigmoid(z) + between), 0.0)
        return jnp.einsum('bhqk,bkhd->bqhd', att, vf)

    o = lax.map(block, (qp, jnp.arange(nb)))
    return o.swapaxes(0, 1).reshape(bsz, nb * qb, n_h, d)[:, :tq]


def sb_mixer(h, k_past, v_past, w_qkv, w_out):
    bsz, t_len, _ = h.shape
    q, k, v = jnp.split(h @ w_qkv, 3, axis=-1)
    shp = (bsz, t_len, SB_HEADS, SB_HEAD_DIM)
    q, k, v = q.reshape(shp), k.reshape(shp), v.reshape(shp)
    if k_past is None:
        keys, vals, offset = k, v, 0
    else:
        keys = jnp.concatenate([k_past.astype(k.dtype), k], axis=1)
        vals = jnp.concatenate([v_past.astype(v.dtype), v], axis=1)
        offset = k_past.shape[1]
    o = stick_breaking_attend(q, keys, vals, offset)
    return o.reshape(bsz, t_len, D_MODEL).astype(h.dtype) @ w_out, k, v


def run_trunk(x, st, p):
    new = {}
    for i in range(DEPTH):
        kind = i % N_MIXERS
        hn = rmsnorm(x, p['norm_mix'][i])
        if kind == 0:
            mix, new['ssm_conv'], new['ssm'] = mamba2_mixer(
                hn, st['ssm_conv'], st['ssm'], p['mb_w_in'], p['mb_conv_w'], p['mb_conv_b'],
                p['mb_dt_bias'], p['mb_a_log'], p['mb_d'], p['mb_norm'], p['mb_w_out'])
        elif kind == 1:
            mix, new['gdn_conv'], new['gdn'] = gdn_mixer(
                hn, st['gdn_conv'], st['gdn'], p['gdn_w_in'], p['gdn_conv_w'], p['gdn_dt_bias'],
                p['gdn_a_log'], p['gdn_norm'], p['gdn_w_out'])
        elif kind == 2:
            mix, new['rwkv_shift'], new['rwkv'] = rwkv7_mixer(
                hn, st['rwkv_shift'], st['rwkv'], p['rw_mu'], p['rw_w_rkv'], p['rw_w0'], p['rw_w1'],
                p['rw_w2'], p['rw_a0'], p['rw_a1'], p['rw_a2'], p['rw_g1'], p['rw_g2'], p['rw_k_k'],
                p['rw_k_a'], p['rw_r_k'], p['rw_ln_w'], p['rw_ln_b'], p['rw_w_out'])
        else:
            mix, new['sb_k'], new['sb_v'] = sb_mixer(hn, st['sb_k'], st['sb_v'], p['sb_w_qkv'], p['sb_w_out'])
        x = x + mix
        x = x + swiglu(rmsnorm(x, p['norm_ffn'][i]), p['ffn_w_gu'][i], p['ffn_w_down'][i])
    return rmsnorm(x, p['norm_final']), new


def setup_inputs(seed: int = 0) -> dict:
    key = jax.random.key(seed)
    ks = iter(jax.random.split(key, 80))
    d = D_MODEL

    def nrm(shape, scale=1.0):
        return jax.random.normal(next(ks), shape, jnp.float32) * scale

    def unif(shape, lo, hi):
        return jax.random.uniform(next(ks), shape, jnp.float32, lo, hi)

    def gain(shape):
        return 1.0 + nrm(shape, 0.02)

    def dt_bias(n):
        dt = jnp.exp(unif((n,), math.log(1e-3), math.log(1e-1)))
        return dt + jnp.log(-jnp.expm1(-dt))

    return {
        'x_prompt': nrm((BATCH, SEQ, d)),
        'x_sample': nrm((DEC_BATCH, DEC_SEQ, d)),
        'state_ssm': nrm((DEC_BATCH, MB_HEADS, MB_HEAD_DIM, MB_D_STATE), 0.3),
        'state_ssm_conv': nrm((DEC_BATCH, MB_CONV - 1, MB_CONV_DIM)),
        'state_gdn': nrm((DEC_BATCH, GDN_HV, GDN_DK, GDN_DV), 0.1),
        'state_gdn_conv': nrm((DEC_BATCH, GDN_CONV - 1, GDN_CONV_DIM)),
        'state_rwkv': nrm((DEC_BATCH, RW_HEADS, RW_HEAD_DIM, RW_HEAD_DIM), 0.3),
        'state_rwkv_shift': nrm((DEC_BATCH, 1, d)),
        'cache_sb_k': nrm((DEC_BATCH, PAST_LEN, SB_HEADS, SB_HEAD_DIM)),
        'cache_sb_v': nrm((DEC_BATCH, PAST_LEN, SB_HEADS, SB_HEAD_DIM)),
        'meta_tokens': nrm((N_META, d)),
        'norm_mix': gain((DEPTH, d)),
        'norm_ffn': gain((DEPTH, d)),
        'norm_final': gain((d,)),
        'ffn_w_gu': nrm((DEPTH, d, 2 * D_FF), d ** -0.5),
        'ffn_w_down': nrm((DEPTH, D_FF, d), D_FF ** -0.5),
        'mb_w_in': nrm((d, MB_IN_DIM), d ** -0.5),
        'mb_conv_w': nrm((MB_CONV, MB_CONV_DIM), MB_CONV ** -0.5),
        'mb_conv_b': nrm((MB_CONV_DIM,), 0.02),
        'mb_dt_bias': dt_bias(MB_HEADS),
        'mb_a_log': jnp.log(unif((MB_HEADS,), 1.0, 16.0)),
        'mb_d': gain((MB_HEADS,)),
        'mb_norm': gain((MB_D_INNER,)),
        'mb_w_out': nrm((MB_D_INNER, d), MB_D_INNER ** -0.5),
        'gdn_w_in': nrm((d, GDN_IN_DIM), d ** -0.5),
        'gdn_conv_w': nrm((GDN_CONV, GDN_CONV_DIM), GDN_CONV ** -0.5),
        'gdn_dt_bias': dt_bias(GDN_HV),
        'gdn_a_log': jnp.log(unif((GDN_HV,), 1.0, 16.0)),
        'gdn_norm': gain((GDN_DV,)),
        'gdn_w_out': nrm((GDN_VAL_DIM, d), GDN_VAL_DIM ** -0.5),
        'rw_mu': unif((6, d), 0.0, 1.0),
        'rw_w_rkv': nrm((3, d, d), d ** -0.5),
        'rw_w0': unif((d,), -6.0, 2.0),
        'rw_w1': nrm((d, RW_DECAY_LORA), d ** -0.5),
        'rw_w2': nrm((RW_DECAY_LORA, d), 0.5 * RW_DECAY_LORA ** -0.5),
        'rw_a0': nrm((d,), 0.1),
        'rw_a1': nrm((d, RW_ICL_LORA), d ** -0.5),
        'rw_a2': nrm((RW_ICL_LORA, d), 0.5 * RW_ICL_LORA ** -0.5),
        'rw_g1': nrm((d, RW_GATE_LORA), d ** -0.5),
        'rw_g2': nrm((RW_GATE_LORA, d), RW_GATE_LORA ** -0.5),
        'rw_k_k': 0.85 + nrm((d,), 0.02),
        'rw_k_a': gain((d,)),
        'rw_r_k': nrm((RW_HEADS, RW_HEAD_DIM), 0.1),
        'rw_ln_w': gain((d,)),
        'rw_ln_b': nrm((d,), 0.02),
        'rw_w_out': nrm((d, d), d ** -0.5),
        'sb_w_qkv': nrm((d, 3 * d), d ** -0.5),
        'sb_w_out': nrm((d, d), d ** -0.5),
    }


def reference(x_prompt, x_sample, state_ssm, state_ssm_conv, state_gdn, state_gdn_conv,
              state_rwkv, state_rwkv_shift, cache_sb_k, cache_sb_v, meta_tokens,
              norm_mix, norm_ffn, norm_final, ffn_w_gu, ffn_w_down,
              mb_w_in, mb_conv_w, mb_conv_b, mb_dt_bias, mb_a_log, mb_d, mb_norm, mb_w_out,
              gdn_w_in, gdn_conv_w, gdn_dt_bias, gdn_a_log, gdn_norm, gdn_w_out,
              rw_mu, rw_w_rkv, rw_w0, rw_w1, rw_w2, rw_a0, rw_a1, rw_a2, rw_g1, rw_g2,
              rw_k_k, rw_k_a, rw_r_k, rw_ln_w, rw_ln_b, rw_w_out, sb_w_qkv, sb_w_out):
    params = dict(
        norm_mix=norm_mix, norm_ffn=norm_ffn, norm_final=norm_final,
        ffn_w_gu=ffn_w_gu, ffn_w_down=ffn_w_down,
        mb_w_in=mb_w_in, mb_conv_w=mb_conv_w, mb_conv_b=mb_conv_b, mb_dt_bias=mb_dt_bias,
        mb_a_log=mb_a_log, mb_d=mb_d, mb_norm=mb_norm, mb_w_out=mb_w_out,
        gdn_w_in=gdn_w_in, gdn_conv_w=gdn_conv_w, gdn_dt_bias=gdn_dt_bias, gdn_a_log=gdn_a_log,
        gdn_norm=gdn_norm, gdn_w_out=gdn_w_out,
        rw_mu=rw_mu, rw_w_rkv=rw_w_rkv, rw_w0=rw_w0, rw_w1=rw_w1, rw_w2=rw_w2, rw_a0=rw_a0,
        rw_a1=rw_a1, rw_a2=rw_a2, rw_g1=rw_g1, rw_g2=rw_g2, rw_k_k=rw_k_k, rw_k_a=rw_k_a,
        rw_r_k=rw_r_k, rw_ln_w=rw_ln_w, rw_ln_b=rw_ln_b, rw_w_out=rw_w_out,
        sb_w_qkv=sb_w_qkv, sb_w_out=sb_w_out)

    bsz = x_prompt.shape[0]
    act = x_prompt.dtype
    meta = jnp.broadcast_to(meta_tokens.astype(act)[None], (bsz, N_META, D_MODEL))
    x0 = jnp.concatenate([meta, x_prompt], axis=1)
    fresh = dict(
        ssm=jnp.zeros((bsz, MB_HEADS, MB_HEAD_DIM, MB_D_STATE), act),
        ssm_conv=jnp.zeros((bsz, MB_CONV - 1, MB_CONV_DIM), act),
        gdn=jnp.zeros((bsz, GDN_HV, GDN_DK, GDN_DV), act),
        gdn_conv=jnp.zeros((bsz, GDN_CONV - 1, GDN_CONV_DIM), act),
        rwkv=jnp.zeros((bsz, RW_HEADS, RW_HEAD_DIM, RW_HEAD_DIM), act),
        rwkv_shift=jnp.zeros((bsz, 1, D_MODEL), act),
        sb_k=None, sb_v=None)
    y_full, sp = run_trunk(x0, fresh, params)
    y_prompt = y_full[:, N_META:]

    past = dict(
        ssm=state_ssm, ssm_conv=state_ssm_conv, gdn=state_gdn, gdn_conv=state_gdn_conv,
        rwkv=state_rwkv, rwkv_shift=state_rwkv_shift, sb_k=cache_sb_k, sb_v=cache_sb_v)
    y_sample, ss = run_trunk(x_sample, past, params)

    return (y_prompt, y_sample,
            sp['ssm'], sp['ssm_conv'], sp['gdn'], sp['gdn_conv'], sp['rwkv'], sp['rwkv_shift'],
            sp['sb_k'], sp['sb_v'],
            ss['ssm'], ss['ssm_conv'], ss['gdn'], ss['gdn_conv'], ss['rwkv'], ss['rwkv_shift'],
            ss['sb_k'], ss['sb_v'])
```

```python
import functools
import math

import jax
import jax.numpy as jnp
from jax import lax
from jax.experimental import pallas as pl
from jax.experimental.pallas import tpu as pltpu

F32 = jnp.float32
BF16 = jnp.bfloat16
EPS = 1e-6
VMEM_LIMIT = 48 * 1024 * 1024
HI = lax.Precision.HIGHEST

PROMPT_ROW_ALIGN = 256
MB_HEAD_DIM = 64
MB_D_STATE = 128
MB_GROUPS = 4
GDN_DK = 128
GDN_DV = 128
RW_HEAD_DIM = 64
RW_GN_EPS = 64e-5
RW_DECAY_SCALE = 0.6065306597126334
SB_HEAD_DIM = 64


def _cparams(*sem):
    return pltpu.CompilerParams(dimension_semantics=sem, vmem_limit_bytes=VMEM_LIMIT)


def _dot(a, b):
    return jnp.dot(a.astype(BF16), b.astype(BF16), preferred_element_type=F32)


def _dot_nt(a, b):
    return lax.dot_general(a.astype(BF16), b.astype(BF16), (((1,), (1,)), ((), ())),
                           preferred_element_type=F32)


def _dot_tn(a, b):
    return lax.dot_general(a.astype(BF16), b.astype(BF16), (((0,), (0,)), ((), ())),
                           preferred_element_type=F32)


def _dot_hi(a, b):
    return jnp.dot(a, b, preferred_element_type=F32, precision=HI)


def _silu(x):
    return x * jax.nn.sigmoid(x)


def _softplus(x):
    return jnp.maximum(x, 0.0) + jnp.log(1.0 + jnp.exp(-jnp.abs(x)))


def _rmsnorm(x, g):
    return x * lax.rsqrt(jnp.mean(x * x, axis=-1, keepdims=True) + EPS) * g


def _row_tile(n, want):
    return n if n <= want else want


def _norm_matmul_kernel(x_ref, g_ref, w_ref, o_ref, h_ref):
    @pl.when(pl.program_id(1) == 0)
    def _():
        h_ref[...] = _rmsnorm(x_ref[...], g_ref[...]).astype(BF16)

    o_ref[...] = jnp.dot(h_ref[...], w_ref[...], preferred_element_type=F32)


def norm_matmul(x, g, w, *, tm=512, tn=512):
    n, d = x.shape
    m = w.shape[1]
    tm = _row_tile(n, tm)
    tn = _row_tile(m, tn)
    return pl.pallas_call(
        _norm_matmul_kernel,
        out_shape=jax.ShapeDtypeStruct((n, m), F32),
        grid=(pl.cdiv(n, tm), pl.cdiv(m, tn)),
        in_specs=[pl.BlockSpec((tm, d), lambda i, j: (i, 0)),
                  pl.BlockSpec((1, d), lambda i, j: (0, 0)),
                  pl.BlockSpec((d, tn), lambda i, j: (0, j))],
        out_specs=pl.BlockSpec((tm, tn), lambda i, j: (i, j)),
        scratch_shapes=[pltpu.VMEM((tm, d), BF16)],
        compiler_params=_cparams("parallel", "arbitrary"),
        name="norm_matmul",
    )(x, g.reshape(1, d), w)


def _matmul_res_kernel(a_ref, w_ref, r_ref, o_ref):
    o_ref[...] = r_ref[...] + jnp.dot(a_ref[...].astype(BF16), w_ref[...], preferred_element_type=F32)


def matmul_res(a, w, res, *, tm=512):
    n, k = a.shape
    d = w.shape[1]
    tm = _row_tile(n, tm)
    return pl.pallas_call(
        _matmul_res_kernel,
        out_shape=jax.ShapeDtypeStruct((n, d), F32),
        grid=(pl.cdiv(n, tm),),
        in_specs=[pl.BlockSpec((tm, k), lambda i: (i, 0)),
                  pl.BlockSpec((k, d), lambda i: (0, 0)),
                  pl.BlockSpec((tm, d), lambda i: (i, 0))],
        out_specs=pl.BlockSpec((tm, d), lambda i: (i, 0)),
        compiler_params=_cparams("parallel"),
        name="matmul_res",
    )(a, w, res)


def _ffn_kernel(x_ref, g_ref, wg_ref, wu_ref, wd_ref, o_ref, h_ref, acc_ref):
    f = pl.program_id(1)

    @pl.when(f == 0)
    def _():
        h_ref[...] = _rmsnorm(x_ref[...], g_ref[...]).astype(BF16)
        acc_ref[...] = jnp.zeros_like(acc_ref)

    h = h_ref[...]
    gate = jnp.dot(h, wg_ref[...], preferred_element_type=F32)
    up = jnp.dot(h, wu_ref[...], preferred_element_type=F32)
    act = (_silu(gate) * up).astype(BF16)
    acc_ref[...] += jnp.dot(act, wd_ref[...], preferred_element_type=F32)

    @pl.when(f == pl.num_programs(1) - 1)
    def _():
        o_ref[...] = x_ref[...] + acc_ref[...]


def ffn(x, g, w_gu, w_down, *, tm=512, tf=1408):
    n, d = x.shape
    f = w_down.shape[0]
    tm = _row_tile(n, tm)
    if f % tf:
        tf = f
    nf = f // tf
    return pl.pallas_call(
        _ffn_kernel,
        out_shape=jax.ShapeDtypeStruct((n, d), F32),
        grid=(pl.cdiv(n, tm), nf),
        in_specs=[pl.BlockSpec((tm, d), lambda i, j: (i, 0)),
                  pl.BlockSpec((1, d), lambda i, j: (0, 0)),
                  pl.BlockSpec((d, tf), lambda i, j: (0, j)),
                  pl.BlockSpec((d, tf), lambda i, j: (0, j + nf)),
                  pl.BlockSpec((tf, d), lambda i, j: (j, 0))],
        out_specs=pl.BlockSpec((tm, d), lambda i, j: (i, 0)),
        scratch_shapes=[pltpu.VMEM((tm, d), BF16), pltpu.VMEM((tm, d), F32)],
        compiler_params=_cparams("parallel", "arbitrary"),
        name="ffn",
    )(x, g.reshape(1, d), w_gu, w_gu, w_down)


def _rmsnorm_kernel(x_ref, g_ref, o_ref):
    o_ref[...] = _rmsnorm(x_ref[...], g_ref[...])


def rmsnorm_rows(x, g, *, tm=1024):
    n, d = x.shape
    tm = _row_tile(n, tm)
    return pl.pallas_call(
        _rmsnorm_kernel,
        out_shape=jax.ShapeDtypeStruct((n, d), F32),
        grid=(pl.cdiv(n, tm),),
        in_specs=[pl.BlockSpec((tm, d), lambda i: (i, 0)),
                  pl.BlockSpec((1, d), lambda i: (0, 0))],
        out_specs=pl.BlockSpec((tm, d), lambda i: (i, 0)),
        compiler_params=_cparams("parallel"),
        name="rmsnorm",
    )(x, g.reshape(1, d))


def _causal_conv_silu(raw, tail_ref, buf_ref, w_ref, bias, q):
    del tail_ref
    buf_ref[pl.ds(8, q), :] = raw
    out = buf_ref[pl.ds(5, q), :] * w_ref[0:1, :]
    out = out + buf_ref[pl.ds(6, q), :] * w_ref[1:2, :]
    out = out + buf_ref[pl.ds(7, q), :] * w_ref[2:3, :]
    out = out + raw * w_ref[3:4, :]
    if bias is not None:
        out = out + bias
    buf_ref[pl.ds(5, 3), :] = buf_ref[pl.ds(q + 5, 3), :]
    return _silu(out)


def _tri(q, strict=False):
    r = lax.broadcasted_iota(jnp.int32, (q, q), 0)
    c = lax.broadcasted_iota(jnp.int32, (q, q), 1)
    return (r > c) if strict else (r >= c)


def _mamba_kernel(z_ref, xs_ref, bc_ref, dt_ref, cx_ref, cbc_ref, h0_ref,
                  cwx_ref, cwbc_ref, cbx_ref, cbbc_ref, dtb_ref, a_ref, dskip_ref, nw_ref,
                  y_ref, h_ref, bufx_ref, bufbc_ref, *, q, t_len, n_heads):
    c = pl.program_id(1)
    n_state = MB_D_STATE
    hpg = n_heads // MB_GROUPS
    gw = hpg * MB_HEAD_DIM

    @pl.when(c == 0)
    def _():
        bufx_ref[pl.ds(5, 3), :] = cx_ref[0]
        bufbc_ref[pl.ds(5, 3), :] = cbc_ref[0]
        h_ref[0] = h0_ref[0]

    rows = c * q + lax.broadcasted_iota(jnp.int32, (q, 1), 0)
    valid = rows < t_len
    xs = _causal_conv_silu(jnp.where(valid, xs_ref[0], 0.0), None, bufx_ref, cwx_ref, cbx_ref[...], q)
    bc = _causal_conv_silu(jnp.where(valid, bc_ref[0], 0.0), None, bufbc_ref, cwbc_ref, cbbc_ref[...], q)

    lane = lax.broadcasted_iota(jnp.int32, (q, 128), 1)
    dt = _softplus(jnp.where(valid & (lane < n_heads), dt_ref[0], 0.0) + dtb_ref[...])
    dt = jnp.where(valid & (lane < n_heads), dt, 0.0)
    da = dt * a_ref[...]
    incl = _tri(q)
    acum = _dot_hi(incl.astype(F32), da)
    acum_t = acum.T
    dt_t = dt.T
    last = acum[q - 1:q, :]
    wts = jnp.exp(last - acum) * dt
    e_acum = jnp.exp(acum)
    e_last = jnp.exp(last)

    lo_half = lax.broadcasted_iota(jnp.int32, (q, 128), 1) < MB_HEAD_DIM
    lo_rows = lax.broadcasted_iota(jnp.int32, (128, 128), 0) < MB_HEAD_DIM

    z = z_ref[0]
    for g in range(MB_GROUPS):
        bm = bc[:, g * n_state:(g + 1) * n_state]
        cm = bc[:, (MB_GROUPS + g) * n_state:(MB_GROUPS + g + 1) * n_state]
        cb = _dot_nt(cm, bm)
        hg = h_ref[0, g * gw:(g + 1) * gw, :]
        ch = _dot_nt(cm, hg)
        ys = []
        for p in range(hpg // 2):
            h_e = g * hpg + 2 * p
            x_pair = xs[:, h_e * MB_HEAD_DIM:(h_e + 2) * MB_HEAD_DIM]
            outs = []
            for hh in (h_e, h_e + 1):
                seg = acum[:, hh:hh + 1] - acum_t[hh:hh + 1, :]
                decay = jnp.where(incl, jnp.exp(jnp.where(incl, seg, 0.0)), 0.0)
                m = cb * decay * dt_t[hh:hh + 1, :]
                outs.append(_dot(m, x_pair))
            y_pair = jnp.where(lo_half, outs[0], outs[1])
            e_pair = jnp.where(lo_half, e_acum[:, h_e:h_e + 1], e_acum[:, h_e + 1:h_e + 2])
            y_pair = y_pair + e_pair * ch[:, 2 * p * MB_HEAD_DIM:(2 * p + 2) * MB_HEAD_DIM]
            ys.append(y_pair)
            w_pair = jnp.where(lo_half, wts[:, h_e:h_e + 1], wts[:, h_e + 1:h_e + 2])
            upd = _dot_tn(x_pair * w_pair, bm)
            scale = jnp.where(lo_rows, e_last[:, h_e:h_e + 1], e_last[:, h_e + 1:h_e + 2])
            r0 = h_e * MB_HEAD_DIM
            h_ref[0, r0:r0 + 128, :] = h_ref[0, r0:r0 + 128, :] * scale + upd
        yg = jnp.concatenate(ys, axis=1)
        xg = xs[:, g * gw:(g + 1) * gw]
        yg = yg + xg * dskip_ref[:, g * gw:(g + 1) * gw]
        yg = yg * _silu(z[:, g * gw:(g + 1) * gw])
        yg = yg * lax.rsqrt(jnp.mean(yg * yg, axis=-1, keepdims=True) + EPS)
        y_ref[0, :, g * gw:(g + 1) * gw] = (yg * nw_ref[:, g * gw:(g + 1) * gw]).astype(y_ref.dtype)


def mamba_core(proj, t_valid, conv_buf, h0, conv_w, conv_b, dt_bias, a_log, d_skip, norm_w, *, chunk=128):
    bsz, t_len, _ = proj.shape
    n_heads = dt_bias.shape[0]
    d_inner = n_heads * MB_HEAD_DIM
    gn = MB_GROUPS * MB_D_STATE
    assert d_inner % 1024 == 0 and 2 * gn == 1024 and n_heads <= 128
    q = min(chunk, t_len)
    assert q % 8 == 0 and t_len % q == 0
    n_chunks = t_len // q
    dtb = jnp.zeros((1, 128), F32).at[0, :n_heads].set(dt_bias)
    a_row = jnp.zeros((1, 128), F32).at[0, :n_heads].set(-jnp.exp(a_log))
    dskip = jnp.repeat(d_skip, MB_HEAD_DIM).reshape(1, d_inner)
    kern = functools.partial(_mamba_kernel, q=q, t_len=t_valid, n_heads=n_heads)
    xblk = d_inner // 1024
    y, h = pl.pallas_call(
        kern,
        out_shape=(jax.ShapeDtypeStruct((bsz, t_len, d_inner), BF16),
                   jax.ShapeDtypeStruct((bsz, n_heads * MB_HEAD_DIM, MB_D_STATE), F32)),
        grid=(bsz, n_chunks),
        in_specs=[
            pl.BlockSpec((1, q, d_inner), lambda b, c: (b, c, 0)),
            pl.BlockSpec((1, q, d_inner), lambda b, c: (b, c, 1)),
            pl.BlockSpec((1, q, 2 * gn), lambda b, c: (b, c, 2 * xblk)),
            pl.BlockSpec((1, q, 128), lambda b, c: (b, c, (2 * d_inner + 2 * gn) // 128)),
            pl.BlockSpec((1, 3, d_inner), lambda b, c: (b, 0, 0)),
            pl.BlockSpec((1, 3, 2 * gn), lambda b, c: (b, 0, xblk)),
            pl.BlockSpec((1, n_heads * MB_HEAD_DIM, MB_D_STATE), lambda b, c: (b, 0, 0)),
            pl.BlockSpec((4, d_inner), lambda b, c: (0, 0)),
            pl.BlockSpec((4, 2 * gn), lambda b, c: (0, xblk)),
            pl.BlockSpec((1, d_inner), lambda b, c: (0, 0)),
            pl.BlockSpec((1, 2 * gn), lambda b, c: (0, xblk)),
            pl.BlockSpec((1, 128), lambda b, c: (0, 0)),
            pl.BlockSpec((1, 128), lambda b, c: (0, 0)),
            pl.BlockSpec((1, d_inner), lambda b, c: (0, 0)),
            pl.BlockSpec((1, d_inner), lambda b, c: (0, 0)),
        ],
        out_specs=(pl.BlockSpec((1, q, d_inner), lambda b, c: (b, c, 0)),
                   pl.BlockSpec((1, n_heads * MB_HEAD_DIM, MB_D_STATE), lambda b, c: (b, 0, 0))),
        scratch_shapes=[pltpu.VMEM((q + 8, d_inner), F32), pltpu.VMEM((q + 8, 2 * gn), F32)],
        compiler_params=_cparams("parallel", "arbitrary"),
        name="mamba_core",
    )(proj, proj, proj, proj, conv_buf, conv_buf, h0.reshape(bsz, n_heads * MB_HEAD_DIM, MB_D_STATE),
      conv_w, conv_w, conv_b.reshape(1, -1), conv_b.reshape(1, -1), dtb, a_row, dskip,
      norm_w.reshape(1, d_inner))
    return y, h.reshape(bsz, n_heads, MB_HEAD_DIM, MB_D_STATE)


def mamba2_layer(x, conv_buf, ssm_state, norm_g, w_in, conv_w, conv_b, dt_bias, a_log, d_skip,
                 norm_w, w_out):
    return mamba2_layer_padded(x, x.shape[1], conv_buf, ssm_state, norm_g, w_in, conv_w, conv_b, dt_bias,
                               a_log, d_skip, norm_w, w_out)


def mamba2_layer_padded(x, t_valid, conv_buf, ssm_state, norm_g, w_in, conv_w, conv_b, dt_bias, a_log,
                        d_skip, norm_w, w_out):
    bsz, t_len, d = x.shape
    n_heads = dt_bias.shape[0]
    d_inner = n_heads * MB_HEAD_DIM
    conv_dim = conv_w.shape[1]
    x2 = x.reshape(bsz * t_len, d)
    proj = norm_matmul(x2, norm_g, w_in).reshape(bsz, t_len, -1)
    y, h_new = mamba_core(proj, t_valid, conv_buf, ssm_state, conv_w, conv_b, dt_bias, a_log, d_skip, norm_w)
    conv_new = proj[:, t_valid - 3:t_valid, d_inner:d_inner + conv_dim]
    out = matmul_res(y.reshape(bsz * t_len, d_inner), w_out, x2)
    return out.reshape(bsz, t_len, d), conv_new, h_new


def _split(a):
    hi = a.astype(BF16)
    lo = (a - hi.astype(F32)).astype(BF16)
    return hi, lo


def _mm(a, b):
    return jnp.dot(a, b, preferred_element_type=F32)


def _dot3(a, b):
    ah, al = _split(a)
    bh, bl = _split(b)
    return _mm(ah, bh) + (_mm(ah, bl) + _mm(al, bh))


def _unit_lower_inverse(a):
    n = a.shape[0]
    eye = (lax.broadcasted_iota(jnp.int32, (n, n), 0) == lax.broadcasted_iota(jnp.int32, (n, n), 1))
    x = jnp.where(eye, 1.0, 0.0) - a
    p = a
    k = 1
    while 2 * k < n:
        p = _dot3(p, p)
        x = x + _dot3(x, p)
        k *= 2
    return x


def _gdn_kernel(q_ref, k_ref, v_ref, z_ref, ba_ref, cq_ref, ck_ref, cv_ref, s0_ref,
                cwq_ref, cwk_ref, cwv_ref, dtb_ref, al_ref, nw_ref,
                o_ref, s_ref, bufq_ref, bufk_ref, bufv_ref, *, q, t_len, n_hv, n_hk):
    c = pl.program_id(1)
    rep = n_hv // n_hk

    @pl.when(c == 0)
    def _():
        bufq_ref[pl.ds(5, 3), :] = cq_ref[0]
        bufk_ref[pl.ds(5, 3), :] = ck_ref[0]
        bufv_ref[pl.ds(5, 3), :] = cv_ref[0]
        s_ref[0] = s0_ref[0]

    rows = c * q + lax.broadcasted_iota(jnp.int32, (q, 1), 0)
    valid = rows < t_len
    qc = _causal_conv_silu(jnp.where(valid, q_ref[0], 0.0), None, bufq_ref, cwq_ref, None, q)
    kc = _causal_conv_silu(jnp.where(valid, k_ref[0], 0.0), None, bufk_ref, cwk_ref, None, q)
    vc = _causal_conv_silu(jnp.where(valid, v_ref[0], 0.0), None, bufv_ref, cwv_ref, None, q)

    lane = lax.broadcasted_iota(jnp.int32, (q, 128), 1)
    ba = jnp.where(valid & (lane < 2 * n_hv), ba_ref[0], 0.0)
    beta = jnp.where(valid & (lane < n_hv), jax.nn.sigmoid(ba), 0.0)
    g = jnp.where(valid & (lane >= n_hv) & (lane < 2 * n_hv),
                  -jnp.exp(al_ref[...]) * _softplus(ba + dtb_ref[...]), 0.0)
    incl = _tri(q)
    strict = _tri(q, strict=True)
    gcum = _dot_hi(incl.astype(F32), g)
    gcum_t = gcum.T
    eg = jnp.exp(gcum)
    glast = gcum[q - 1:q, :]
    w_last = jnp.exp(glast - gcum)
    e_last = jnp.exp(glast)

    z = z_ref[0]
    for kh in range(n_hk):
        qh = qc[:, kh * GDN_DK:(kh + 1) * GDN_DK]
        kh_ = kc[:, kh * GDN_DK:(kh + 1) * GDN_DK]
        qh = qh * (lax.rsqrt(jnp.sum(qh * qh, axis=-1, keepdims=True) + 1e-6) * (GDN_DK ** -0.5))
        kh_ = kh_ * lax.rsqrt(jnp.sum(kh_ * kh_, axis=-1, keepdims=True) + 1e-6)
        kk = _dot_nt(kh_, kh_)
        qk0 = _dot_nt(qh, kh_)
        for r in range(rep):
            h = kh * rep + r
            hl = n_hv + h
            seg = gcum[:, hl:hl + 1] - gcum_t[hl:hl + 1, :]
            dec = jnp.where(incl, jnp.exp(jnp.where(incl, seg, 0.0)), 0.0)
            bt = beta[:, h:h + 1]
            a_mat = jnp.where(strict, kk * dec, 0.0) * bt
            s = s_ref[0, h * GDN_DK:(h + 1) * GDN_DK, :]
            egc = eg[:, hl:hl + 1]
            vh = vc[:, h * GDN_DV:(h + 1) * GDN_DV]
            rhs = bt * (vh - egc * _dot(kh_, s))
            delta = _dot3(_unit_lower_inverse(a_mat), rhs)
            o = egc * _dot(qh, s) + _dot(qk0 * dec, delta)
            s_ref[0, h * GDN_DK:(h + 1) * GDN_DK, :] = (
                s * e_last[:, hl:hl + 1] + _dot_tn(kh_ * w_last[:, hl:hl + 1], delta))
            o = o * lax.rsqrt(jnp.mean(o * o, axis=-1, keepdims=True) + EPS) * nw_ref[...]
            o = o * _silu(z[:, h * GDN_DV:(h + 1) * GDN_DV])
            o_ref[0, :, h * GDN_DV:(h + 1) * GDN_DV] = o.astype(o_ref.dtype)


def gdn_core(proj, t_valid, conv_buf, s0, conv_w, dt_bias, a_log, norm_w, *, chunk=64):
    bsz, t_len, _ = proj.shape
    n_hv = dt_bias.shape[0]
    val_dim = n_hv * GDN_DV
    key_dim = (conv_w.shape[1] - val_dim) // 2
    n_hk = key_dim // GDN_DK
    assert val_dim == 2 * key_dim and key_dim % 128 == 0 and 2 * n_hv <= 128
    q = min(chunk, t_len)
    assert q % 8 == 0 and t_len % q == 0
    n_chunks = t_len // q
    pad = jnp.zeros((n_hv,), F32)
    dtb = jnp.zeros((1, 128), F32).at[0, n_hv:2 * n_hv].set(dt_bias)
    alog = jnp.zeros((1, 128), F32).at[0, n_hv:2 * n_hv].set(a_log)
    del pad
    kern = functools.partial(_gdn_kernel, q=q, t_len=t_valid, n_hv=n_hv, n_hk=n_hk)
    ba_blk = (2 * key_dim + 2 * val_dim) // 128
    s0 = s0.reshape(bsz, n_hv * GDN_DK, GDN_DV)
    o, s = pl.pallas_call(
        kern,
        out_shape=(jax.ShapeDtypeStruct((bsz, t_len, val_dim), BF16),
                   jax.ShapeDtypeStruct(s0.shape, F32)),
        grid=(bsz, n_chunks),
        in_specs=[
            pl.BlockSpec((1, q, key_dim), lambda b, c: (b, c, 0)),
            pl.BlockSpec((1, q, key_dim), lambda b, c: (b, c, 1)),
            pl.BlockSpec((1, q, val_dim), lambda b, c: (b, c, 1)),
            pl.BlockSpec((1, q, val_dim), lambda b, c: (b, c, 2)),
            pl.BlockSpec((1, q, 128), lambda b, c: (b, c, ba_blk)),
            pl.BlockSpec((1, 3, key_dim), lambda b, c: (b, 0, 0)),
            pl.BlockSpec((1, 3, key_dim), lambda b, c: (b, 0, 1)),
            pl.BlockSpec((1, 3, val_dim), lambda b, c: (b, 0, 1)),
            pl.BlockSpec((1,) + s0.shape[1:], lambda b, c: (b, 0, 0)),
            pl.BlockSpec((4, key_dim), lambda b, c: (0, 0)),
            pl.BlockSpec((4, key_dim), lambda b, c: (0, 1)),
            pl.BlockSpec((4, val_dim), lambda b, c: (0, 1)),
            pl.BlockSpec((1, 128), lambda b, c: (0, 0)),
            pl.BlockSpec((1, 128), lambda b, c: (0, 0)),
            pl.BlockSpec((1, GDN_DV), lambda b, c: (0, 0)),
        ],
        out_specs=(pl.BlockSpec((1, q, val_dim), lambda b, c: (b, c, 0)),
                   pl.BlockSpec((1,) + s0.shape[1:], lambda b, c: (b, 0, 0))),
        scratch_shapes=[pltpu.VMEM((q + 8, key_dim), F32), pltpu.VMEM((q + 8, key_dim), F32),
                        pltpu.VMEM((q + 8, val_dim), F32)],
        compiler_params=_cparams("parallel", "arbitrary"),
        name="gdn_core",
    )(proj, proj, proj, proj, proj, conv_buf, conv_buf, conv_buf, s0,
      conv_w, conv_w, conv_w, dtb, alog, norm_w.reshape(1, GDN_DV))
    return o, s.reshape(bsz, n_hv, GDN_DK, GDN_DV)


def gdn_layer(x, conv_buf, s0, norm_g, w_in, conv_w, dt_bias, a_log, norm_w, w_out):
    return gdn_layer_padded(x, x.shape[1], conv_buf, s0, norm_g, w_in, conv_w, dt_bias, a_log, norm_w, w_out)


def gdn_layer_padded(x, t_valid, conv_buf, s0, norm_g, w_in, conv_w, dt_bias, a_log, norm_w, w_out):
    bsz, t_len, d = x.shape
    conv_dim = conv_w.shape[1]
    x2 = x.reshape(bsz * t_len, d)
    proj = norm_matmul(x2, norm_g, w_in).reshape(bsz, t_len, -1)
    o, s_new = gdn_core(proj, t_valid, conv_buf, s0, conv_w, dt_bias, a_log, norm_w)
    conv_new = proj[:, t_valid - 3:t_valid, :conv_dim]
    out = matmul_res(o.reshape(bsz * t_len, -1), w_out, x2)
    return out.reshape(bsz, t_len, d), conv_new, s_new


def _head_ones(width, head):
    r = lax.broadcasted_iota(jnp.int32, (width, width), 0) // head
    c = lax.broadcasted_iota(jnp.int32, (width, width), 1) // head
    return jnp.where(r == c, 1.0, 0.0).astype(BF16)


def _head_sum(x, ones):
    hi, lo = _split(x)
    return _mm(hi, ones) + _mm(lo, ones)


def _rw_proj_kernel(hn_ref, pv_ref, mu_ref, wr_ref, wk_ref, wv_ref, w1_ref, w2_ref, a1_ref, a2_ref,
                    g1_ref, g2_ref, w0_ref, a0_ref, kk_ref, ka_ref,
                    r_ref, lw_ref, k_ref, v_ref, kn_ref, b_ref, g_ref):
    hn = hn_ref[...]
    dlt = pv_ref[...] - hn

    def mix(c):
        return (hn + dlt * mu_ref[c:c + 1, :]).astype(BF16)

    r_ref[...] = _mm(mix(0), wr_ref[...])
    k = _mm(mix(1), wk_ref[...])
    v_ref[...] = _mm(mix(2), wv_ref[...])
    dec = w0_ref[...] + _mm(jnp.tanh(_mm(mix(3), w1_ref[...])).astype(BF16), w2_ref[...])
    lw_ref[...] = -RW_DECAY_SCALE * jax.nn.sigmoid(dec)
    a = jax.nn.sigmoid(a0_ref[...] + _mm(_mm(mix(4), a1_ref[...]).astype(BF16), a2_ref[...]))
    g_ref[...] = _mm(jax.nn.sigmoid(_mm(mix(5), g1_ref[...])).astype(BF16), g2_ref[...])
    ones = _head_ones(128, RW_HEAD_DIM)
    kn = k * kk_ref[...]
    d = kn.shape[1]
    for j in range(d // 128):
        sl = slice(j * 128, (j + 1) * 128)
        knj = kn[:, sl]
        knj = knj * lax.rsqrt(_head_sum(knj * knj, ones) + 1e-6)
        kn_ref[:, sl] = knj
        b_ref[:, sl] = knj * a[:, sl]
    k_ref[...] = k * (1.0 + (a - 1.0) * ka_ref[...])


def rw_proj(hn, prev, p, *, tm=256):
    n, d = hn.shape
    tm = _row_tile(n, tm)
    row = lambda i: (i, 0)
    fix = lambda i: (0, 0)
    big = pl.BlockSpec((tm, d), row)
    vec = pl.BlockSpec((1, d), fix)

    def full(a):
        return pl.BlockSpec(a.shape, fix)

    ws = [p['rw_w_rkv'][0].astype(BF16), p['rw_w_rkv'][1].astype(BF16), p['rw_w_rkv'][2].astype(BF16),
          p['rw_w1'].astype(BF16), p['rw_w2'].astype(BF16), p['rw_a1'].astype(BF16), p['rw_a2'].astype(BF16),
          p['rw_g1'].astype(BF16), p['rw_g2'].astype(BF16)]
    vecs = [p['rw_w0'].reshape(1, d), p['rw_a0'].reshape(1, d), p['rw_k_k'].reshape(1, d),
            p['rw_k_a'].reshape(1, d)]
    return pl.pallas_call(
        _rw_proj_kernel,
        out_shape=tuple(jax.ShapeDtypeStruct((n, d), F32) for _ in range(7)),
        grid=(pl.cdiv(n, tm),),
        in_specs=[big, big, full(p['rw_mu'])] + [full(w) for w in ws] + [vec] * 4,
        out_specs=tuple(big for _ in range(7)),
        compiler_params=_cparams("parallel"),
        name="rw_proj",
    )(hn, prev, p['rw_mu'], *ws, *vecs)


def _rw_scan_kernel(r_ref, lw_ref, k_ref, v_ref, kn_ref, b_ref, g_ref, s0_ref, rk_ref, lnw_ref, lnb_ref,
                    o_ref, s_ref, *, q, t_len):
    c = pl.program_id(1)

    @pl.when(c == 0)
    def _():
        s_ref[0] = s0_ref[0]

    rows = c * q + lax.broadcasted_iota(jnp.int32, (q, 1), 0)
    valid = rows < t_len
    incl = _tri(q)
    strict = _tri(q, strict=True)
    lw_all = jnp.where(valid, lw_ref[0], 0.0)
    cum_all = _dot_hi(incl.astype(F32), lw_all)
    ones = _head_ones(128, RW_HEAD_DIM)
    lo = lax.broadcasted_iota(jnp.int32, (q, 128), 1) < RW_HEAD_DIM
    blockdiag = ((lax.broadcasted_iota(jnp.int32, (128, 128), 0) < RW_HEAD_DIM)
                 == (lax.broadcasted_iota(jnp.int32, (128, 128), 1) < RW_HEAD_DIM))
    d = lw_all.shape[1]
    for p in range(d // 128):
        sl = slice(p * 128, (p + 1) * 128)
        lw = lw_all[:, sl]
        cum = cum_all[:, sl]
        cum_end = cum[q - 1:q, :]
        r = jnp.where(valid, r_ref[0, :, sl], 0.0)
        k = jnp.where(valid, k_ref[0, :, sl], 0.0)
        v = jnp.where(valid, v_ref[0, :, sl], 0.0)
        kn = jnp.where(valid, kn_ref[0, :, sl], 0.0)
        b = jnp.where(valid, b_ref[0, :, sl], 0.0)
        kq = kn * jnp.exp(cum - lw)
        rq = r * jnp.exp(cum)
        einv = jnp.exp(-cum)
        kd = k * einv
        bd = b * einv
        eend = jnp.exp(cum_end - cum)
        s2 = s_ref[0, sl, :]
        lhs = jnp.concatenate([jnp.where(lo, kq, 0.0), jnp.where(lo, 0.0, kq),
                               jnp.where(lo, rq, 0.0), jnp.where(lo, 0.0, rq)], axis=0)
        pk = _dot_nt(lhs, kd)
        pb = _dot_nt(lhs, bd)
        u0 = _dot_nt(kq, s2)
        y0 = _dot_nt(rq, s2)
        ak = [jnp.where(strict, pk[e * q:(e + 1) * q], 0.0) for e in range(2)]
        ab = [jnp.where(strict, pb[e * q:(e + 1) * q], 0.0) for e in range(2)]
        rk = [jnp.where(incl, pk[(2 + e) * q:(3 + e) * q], 0.0) for e in range(2)]
        rb = [jnp.where(incl, pb[(2 + e) * q:(3 + e) * q], 0.0) for e in range(2)]
        rhs = u0 + jnp.where(lo, _dot(ak[0], v), _dot(ak[1], v))
        u = jnp.where(lo, _dot3(_unit_lower_inverse(ab[0]), rhs), _dot3(_unit_lower_inverse(ab[1]), rhs))
        y = y0 + jnp.where(lo, _dot(rk[0], v) - _dot(rb[0], u), _dot(rk[1], v) - _dot(rb[1], u))
        s_new = s2 * jnp.exp(cum_end) + _dot_tn(v, k * eend) - _dot_tn(u, b * eend)
        s_ref[0, sl, :] = jnp.where(blockdiag, s_new, 0.0)
        mean = _head_sum(y, ones) * (1.0 / RW_HEAD_DIM)
        yc = y - mean
        var = _head_sum(yc * yc, ones) * (1.0 / RW_HEAD_DIM)
        yn = yc * lax.rsqrt(var + RW_GN_EPS) * lnw_ref[:, sl] + lnb_ref[:, sl]
        bonus = _head_sum(r * k * rk_ref[:, sl], ones) * v
        o_ref[0, :, sl] = ((yn + bonus) * g_ref[0, :, sl]).astype(o_ref.dtype)


def rw_scan(r, lw, k, v, kn, b, g, t_valid, s0, r_k, ln_w, ln_b, *, chunk=64):
    bsz, t_len, d = r.shape
    q = min(chunk, t_len)
    assert q % 8 == 0 and t_len % q == 0 and d % 128 == 0
    n_heads = d // RW_HEAD_DIM
    s5 = s0.reshape(bsz, n_heads // 2, 2, RW_HEAD_DIM, RW_HEAD_DIM)
    s2 = jnp.einsum('bpevk,ef->bpevfk', s5, jnp.eye(2, dtype=F32)).reshape(bsz, d, 128)
    blk = pl.BlockSpec((1, q, d), lambda bb, c: (bb, c, 0))
    vec = pl.BlockSpec((1, d), lambda bb, c: (0, 0))
    st = pl.BlockSpec((1, d, 128), lambda bb, c: (bb, 0, 0))
    o, s_new = pl.pallas_call(
        functools.partial(_rw_scan_kernel, q=q, t_len=t_valid),
        out_shape=(jax.ShapeDtypeStruct((bsz, t_len, d), BF16), jax.ShapeDtypeStruct((bsz, d, 128), F32)),
        grid=(bsz, t_len // q),
        in_specs=[blk] * 7 + [st, vec, vec, vec],
        out_specs=(blk, st),
        compiler_params=_cparams("parallel", "arbitrary"),
        name="rw_scan",
    )(r, lw, k, v, kn, b, g, s2, r_k.reshape(1, d), ln_w.reshape(1, d), ln_b.reshape(1, d))
    s6 = s_new.reshape(bsz, n_heads // 2, 2, RW_HEAD_DIM, 2, RW_HEAD_DIM)
    s_out = jnp.stack([s6[:, :, 0, :, 0, :], s6[:, :, 1, :, 1, :]], axis=2)
    return o, s_out.reshape(bsz, n_heads, RW_HEAD_DIM, RW_HEAD_DIM)


def rwkv_layer_padded(x, t_valid, shift_buf, s0, norm_g, p):
    bsz, t_len, d = x.shape
    x2 = x.reshape(bsz * t_len, d)
    hn = rmsnorm_rows(x2, norm_g).reshape(bsz, t_len, d)
    prev = jnp.concatenate([shift_buf, hn[:, :-1]], axis=1)
    outs = rw_proj(hn.reshape(-1, d), prev.reshape(-1, d), p)
    r, lw, k, v, kn, b, g = (a.reshape(bsz, t_len, d) for a in outs)
    yg, s_new = rw_scan(r, lw, k, v, kn, b, g, t_valid, s0, p['rw_r_k'], p['rw_ln_w'], p['rw_ln_b'])
    out = matmul_res(yg.reshape(bsz * t_len, d), p['rw_w_out'].astype(BF16), x2)
    return out.reshape(bsz, t_len, d), hn[:, t_valid - 1:t_valid], s_new


def _dot3_nt(a, b):
    ah, al = _split(a)
    bh, bl = _split(b)
    dn = (((1,), (1,)), ((), ()))
    f = lambda u, w: lax.dot_general(u, w, dn, preferred_element_type=F32)
    return f(ah, bh) + (f(ah, bl) + f(al, bh))


def _suffix_ones(n):
    r = lax.broadcasted_iota(jnp.int32, (n, n), 0)
    c = lax.broadcasted_iota(jnp.int32, (n, n), 1)
    return jnp.where(r >= c, 1.0, 0.0).astype(BF16)


def _sb_block(q_e, kblk, vblk, carry, acc, lo, suffix, mask):
    outs = []
    new_carry = []
    for e in range(2):
        z = _dot3_nt(q_e[e], kblk)
        lnb = -_softplus(z)
        if mask is not None:
            lnb = jnp.where(mask, lnb, 0.0)
        hi, lw = _split(lnb)
        rsum = _mm(hi, suffix) + _mm(lw, suffix) + carry[e]
        att = jnp.exp(z + rsum)
        if mask is not None:
            att = jnp.where(mask, att, 0.0)
        outs.append(_dot(att, vblk))
        new_carry.append(rsum[:, 0:1])
    return new_carry, acc + jnp.where(lo, outs[0], outs[1])


def _sb_kernel(*refs, bq, bp, n_past, scale):
    if n_past:
        q_ref, kn_ref, vn_ref, kp_ref, vp_ref, o_ref = refs
    else:
        q_ref, kn_ref, vn_ref, o_ref = refs
    i = pl.program_id(2)
    lo = lax.broadcasted_iota(jnp.int32, (bq, 128), 1) < SB_HEAD_DIM
    q = q_ref[0] * scale
    q_e = [jnp.where(lo, q, 0.0), jnp.where(lo, 0.0, q)]
    suffix = _suffix_ones(bq)
    mask = _tri(bq, strict=True)
    zero_c = jnp.zeros((bq, 1), F32)
    start = pl.multiple_of(i * bq, bq)
    carry, acc = _sb_block(q_e, kn_ref[0, pl.ds(start, bq), :], vn_ref[0, pl.ds(start, bq), :],
                           [zero_c, zero_c], jnp.zeros((bq, 128), F32), lo, suffix, mask)

    def new_body(jj, st):
        c0, c1, a = st
        off = pl.multiple_of((i - 1 - jj) * bq, bq)
        (c0, c1), a = _sb_block(q_e, kn_ref[0, pl.ds(off, bq), :], vn_ref[0, pl.ds(off, bq), :],
                                [c0, c1], a, lo, suffix, None)
        return c0, c1, a

    c0, c1, acc = lax.fori_loop(0, i, new_body, (carry[0], carry[1], acc))

    if n_past:
        suffix_p = suffix if bp == bq else _suffix_ones(bp)

        def past_body(jj, st):
            c0, c1, a = st
            off = pl.multiple_of((n_past - 1 - jj) * bp, bp)
            (c0, c1), a = _sb_block(q_e, kp_ref[0, pl.ds(off, bp), :], vp_ref[0, pl.ds(off, bp), :],
                                    [c0, c1], a, lo, suffix_p, None)
            return c0, c1, a

        c0, c1, acc = lax.fori_loop(0, n_past, past_body, (c0, c1, acc))
    o_ref[0] = acc.astype(o_ref.dtype)


def sb_attention(qkv, k_past, v_past, *, block=256):
    bsz, t_len, d3 = qkv.shape
    d = d3 // 3
    assert d % 128 == 0
    n_pairs = d // 128
    bq = min(block, t_len)
    assert t_len % bq == 0 and bq % 8 == 0
    args = [qkv, qkv, qkv]
    in_specs = [pl.BlockSpec((1, bq, 128), lambda b, p, i: (b, i, p)),
                pl.BlockSpec((1, t_len, 128), lambda b, p, i: (b, 0, n_pairs + p)),
                pl.BlockSpec((1, t_len, 128), lambda b, p, i: (b, 0, 2 * n_pairs + p))]
    n_past = bp = 0
    if k_past is not None:
        past_len = k_past.shape[1]
        bp = min(block, past_len)
        assert past_len % bp == 0
        n_past = past_len // bp
        args += [k_past, v_past]
        in_specs += [pl.BlockSpec((1, past_len, 128), lambda b, p, i: (b, 0, p))] * 2
    return pl.pallas_call(
        functools.partial(_sb_kernel, bq=bq, bp=bp, n_past=n_past, scale=SB_HEAD_DIM ** -0.5),
        out_shape=jax.ShapeDtypeStruct((bsz, t_len, d), BF16),
        grid=(bsz, n_pairs, t_len // bq),
        in_specs=in_specs,
        out_specs=pl.BlockSpec((1, bq, 128), lambda b, p, i: (b, i, p)),
        compiler_params=_cparams("parallel", "parallel", "arbitrary"),
        name="sb_attention",
    )(*args)


def sb_layer_padded(x, t_valid, k_past, v_past, norm_g, w_qkv, w_out):
    bsz, t_len, d = x.shape
    x2 = x.reshape(bsz * t_len, d)
    qkv = norm_matmul(x2, norm_g, w_qkv).reshape(bsz, t_len, 3 * d)
    if k_past is not None:
        k_past = k_past.reshape(bsz, -1, d)
        v_past = v_past.reshape(bsz, -1, d)
    o = sb_attention(qkv, k_past, v_past)
    out = matmul_res(o.reshape(bsz * t_len, d), w_out, x2)
    n_heads = d // SB_HEAD_DIM
    k_new = qkv[:, :t_valid, d:2 * d].reshape(bsz, t_valid, n_heads, SB_HEAD_DIM)
    v_new = qkv[:, :t_valid, 2 * d:].reshape(bsz, t_valid, n_heads, SB_HEAD_DIM)
    return out.reshape(bsz, t_len, d), k_new, v_new


def _run_trunk(x, t_valid, st, p):
    bsz, t_len, d = x.shape
    new = {}
    bf = lambda a: a.astype(BF16)

    def ffn_layer(x, i):
        return ffn(x.reshape(bsz * t_len, d), p['norm_ffn'][i], bf(p['ffn_w_gu'][i]),
                   bf(p['ffn_w_down'][i])).reshape(bsz, t_len, d)

    x, new['ssm_conv'], new['ssm'] = mamba2_layer_padded(
        x, t_valid, st['ssm_conv'], st['ssm'], p['norm_mix'][0], bf(p['mb_w_in']), p['mb_conv_w'],
        p['mb_conv_b'], p['mb_dt_bias'], p['mb_a_log'], p['mb_d'], p['mb_norm'], bf(p['mb_w_out']))
    x = ffn_layer(x, 0)
    x, new['gdn_conv'], new['gdn'] = gdn_layer_padded(
        x, t_valid, st['gdn_conv'], st['gdn'], p['norm_mix'][1], bf(p['gdn_w_in']), p['gdn_conv_w'],
        p['gdn_dt_bias'], p['gdn_a_log'], p['gdn_norm'], bf(p['gdn_w_out']))
    x = ffn_layer(x, 1)
    x, new['rwkv_shift'], new['rwkv'] = rwkv_layer_padded(
        x, t_valid, st['rwkv_shift'], st['rwkv'], p['norm_mix'][2], p)
    x = ffn_layer(x, 2)
    x, new['sb_k'], new['sb_v'] = sb_layer_padded(
        x, t_valid, st['sb_k'], st['sb_v'], p['norm_mix'][3], bf(p['sb_w_qkv']), bf(p['sb_w_out']))
    x = ffn_layer(x, 3)
    y = rmsnorm_rows(x.reshape(bsz * t_len, d), p['norm_final']).reshape(bsz, t_len, d)
    return y, new


def kernel(x_prompt, x_sample, state_ssm, state_ssm_conv, state_gdn, state_gdn_conv, state_rwkv, state_rwkv_shift, cache_sb_k, cache_sb_v, meta_tokens, norm_mix, norm_ffn, norm_final, ffn_w_gu, ffn_w_down, mb_w_in, mb_conv_w, mb_conv_b, mb_dt_bias, mb_a_log, mb_d, mb_norm, mb_w_out, gdn_w_in, gdn_conv_w, gdn_dt_bias, gdn_a_log, gdn_norm, gdn_w_out, rw_mu, rw_w_rkv, rw_w0, rw_w1, rw_w2, rw_a0, rw_a1, rw_a2, rw_g1, rw_g2, rw_k_k, rw_k_a, rw_r_k, rw_ln_w, rw_ln_b, rw_w_out, sb_w_qkv, sb_w_out):
    p = dict(
        norm_mix=norm_mix, norm_ffn=norm_ffn, norm_final=norm_final, ffn_w_gu=ffn_w_gu, ffn_w_down=ffn_w_down,
        mb_w_in=mb_w_in, mb_conv_w=mb_conv_w, mb_conv_b=mb_conv_b, mb_dt_bias=mb_dt_bias, mb_a_log=mb_a_log,
        mb_d=mb_d, mb_norm=mb_norm, mb_w_out=mb_w_out,
        gdn_w_in=gdn_w_in, gdn_conv_w=gdn_conv_w, gdn_dt_bias=gdn_dt_bias, gdn_a_log=gdn_a_log,
        gdn_norm=gdn_norm, gdn_w_out=gdn_w_out,
        rw_mu=rw_mu, rw_w_rkv=rw_w_rkv, rw_w0=rw_w0, rw_w1=rw_w1, rw_w2=rw_w2, rw_a0=rw_a0, rw_a1=rw_a1,
        rw_a2=rw_a2, rw_g1=rw_g1, rw_g2=rw_g2, rw_k_k=rw_k_k, rw_k_a=rw_k_a, rw_r_k=rw_r_k, rw_ln_w=rw_ln_w,
        rw_ln_b=rw_ln_b, rw_w_out=rw_w_out, sb_w_qkv=sb_w_qkv, sb_w_out=sb_w_out)
    bsz, seq, d = x_prompt.shape
    n_meta = meta_tokens.shape[0]
    t_valid = n_meta + seq
    t_pad = -(-t_valid // PROMPT_ROW_ALIGN) * PROMPT_ROW_ALIGN
    meta = jnp.broadcast_to(meta_tokens[None], (bsz, n_meta, d))
    x0 = jnp.concatenate([meta, x_prompt, jnp.zeros((bsz, t_pad - t_valid, d), x_prompt.dtype)], axis=1)
    fresh = dict(
        ssm=jnp.zeros((bsz,) + state_ssm.shape[1:], F32), ssm_conv=jnp.zeros((bsz,) + state_ssm_conv.shape[1:], F32),
        gdn=jnp.zeros((bsz,) + state_gdn.shape[1:], F32), gdn_conv=jnp.zeros((bsz,) + state_gdn_conv.shape[1:], F32),
        rwkv=jnp.zeros((bsz,) + state_rwkv.shape[1:], F32),
        rwkv_shift=jnp.zeros((bsz,) + state_rwkv_shift.shape[1:], F32), sb_k=None, sb_v=None)
    y_full, sp = _run_trunk(x0, t_valid, fresh, p)
    y_prompt = y_full[:, n_meta:t_valid]
    past = dict(ssm=state_ssm, ssm_conv=state_ssm_conv, gdn=state_gdn, gdn_conv=state_gdn_conv,
                rwkv=state_rwkv, rwkv_shift=state_rwkv_shift, sb_k=cache_sb_k, sb_v=cache_sb_v)
    y_sample, ss = _run_trunk(x_sample, x_sample.shape[1], past, p)
    return (y_prompt, y_sample,
            sp['ssm'], sp['ssm_conv'], sp['gdn'], sp['gdn_conv'], sp['rwkv'], sp['rwkv_shift'],
            sp['sb_k'], sp['sb_v'],
            ss['ssm'], ss['ssm_conv'], ss['gdn'], ss['gdn_conv'], ss['rwkv'], ss['rwkv_shift'],
            ss['sb_k'], ss['sb_v'])
```

```python
import functools
import math

import jax
import jax.numpy as jnp
from jax import lax
from jax.experimental import pallas as pl
from jax.experimental.pallas import tpu as pltpu

F32 = jnp.float32
BF16 = jnp.bfloat16
EPS = 1e-6
VMEM_LIMIT = 48 * 1024 * 1024
HI = lax.Precision.HIGHEST

PROMPT_ROW_ALIGN = 256
MB_HEAD_DIM = 64
MB_D_STATE = 128
MB_GROUPS = 4
GDN_DK = 128
GDN_DV = 128
RW_HEAD_DIM = 64
RW_GN_EPS = 64e-5
RW_DECAY_SCALE = 0.6065306597126334
SB_HEAD_DIM = 64
SB_LOG_CUT = -100.0


def _cparams(*sem):
    return pltpu.CompilerParams(dimension_semantics=sem, vmem_limit_bytes=VMEM_LIMIT)


def _dot(a, b):
    return jnp.dot(a.astype(BF16), b.astype(BF16), preferred_element_type=F32)


def _dot_nt(a, b):
    return lax.dot_general(a.astype(BF16), b.astype(BF16), (((1,), (1,)), ((), ())),
                           preferred_element_type=F32)


def _dot_tn(a, b):
    return lax.dot_general(a.astype(BF16), b.astype(BF16), (((0,), (0,)), ((), ())),
                           preferred_element_type=F32)


def _dot_hi(a, b):
    return jnp.dot(a, b, preferred_element_type=F32, precision=HI)


def _silu(x):
    return x * jax.nn.sigmoid(x)


def _softplus(x):
    return jnp.maximum(x, 0.0) + jnp.log(1.0 + jnp.exp(-jnp.abs(x)))


def _rmsnorm(x, g):
    return x * lax.rsqrt(jnp.mean(x * x, axis=-1, keepdims=True) + EPS) * g


def _row_tile(n, want):
    return n if n <= want else want


def _norm_matmul_kernel(x_ref, g_ref, w_ref, o_ref, h_ref):
    @pl.when(pl.program_id(1) == 0)
    def _():
        h_ref[...] = _rmsnorm(x_ref[...], g_ref[...]).astype(BF16)

    o_ref[...] = jnp.dot(h_ref[...], w_ref[...], preferred_element_type=F32)


def norm_matmul(x, g, w, *, tm=512, tn=512):
    n, d = x.shape
    m = w.shape[1]
    tm = _row_tile(n, tm)
    tn = _row_tile(m, tn)
    return pl.pallas_call(
        _norm_matmul_kernel,
        out_shape=jax.ShapeDtypeStruct((n, m), F32),
        grid=(pl.cdiv(n, tm), pl.cdiv(m, tn)),
        in_specs=[pl.BlockSpec((tm, d), lambda i, j: (i, 0)),
                  pl.BlockSpec((1, d), lambda i, j: (0, 0)),
                  pl.BlockSpec((d, tn), lambda i, j: (0, j))],
        out_specs=pl.BlockSpec((tm, tn), lambda i, j: (i, j)),
        scratch_shapes=[pltpu.VMEM((tm, d), BF16)],
        compiler_params=_cparams("parallel", "arbitrary"),
        name="norm_matmul",
    )(x, g.reshape(1, d), w)


def _matmul_res_kernel(a_ref, w_ref, r_ref, o_ref):
    o_ref[...] = r_ref[...] + jnp.dot(a_ref[...].astype(BF16), w_ref[...], preferred_element_type=F32)


def matmul_res(a, w, res, *, tm=512):
    n, k = a.shape
    d = w.shape[1]
    tm = _row_tile(n, tm)
    return pl.pallas_call(
        _matmul_res_kernel,
        out_shape=jax.ShapeDtypeStruct((n, d), F32),
        grid=(pl.cdiv(n, tm),),
        in_specs=[pl.BlockSpec((tm, k), lambda i: (i, 0)),
                  pl.BlockSpec((k, d), lambda i: (0, 0)),
                  pl.BlockSpec((tm, d), lambda i: (i, 0))],
        out_specs=pl.BlockSpec((tm, d), lambda i: (i, 0)),
        compiler_params=_cparams("parallel"),
        name="matmul_res",
    )(a, w, res)


def _ffn_kernel(x_ref, g_ref, wg_ref, wu_ref, wd_ref, o_ref, h_ref, acc_ref):
    f = pl.program_id(1)

    @pl.when(f == 0)
    def _():
        h_ref[...] = _rmsnorm(x_ref[...], g_ref[...]).astype(BF16)
        acc_ref[...] = jnp.zeros_like(acc_ref)

    h = h_ref[...]
    gate = jnp.dot(h, wg_ref[...], preferred_element_type=F32)
    up = jnp.dot(h, wu_ref[...], preferred_element_type=F32)
    act = (_silu(gate) * up).astype(BF16)
    acc_ref[...] += jnp.dot(act, wd_ref[...], preferred_element_type=F32)

    @pl.when(f == pl.num_programs(1) - 1)
    def _():
        o_ref[...] = x_ref[...] + acc_ref[...]


def ffn(x, g, w_gu, w_down, *, tm=512, tf=1408):
    n, d = x.shape
    f = w_down.shape[0]
    tm = _row_tile(n, tm)
    if f % tf:
        tf = f
    nf = f // tf
    return pl.pallas_call(
        _ffn_kernel,
        out_shape=jax.ShapeDtypeStruct((n, d), F32),
        grid=(pl.cdiv(n, tm), nf),
        in_specs=[pl.BlockSpec((tm, d), lambda i, j: (i, 0)),
                  pl.BlockSpec((1, d), lambda i, j: (0, 0)),
                  pl.BlockSpec((d, tf), lambda i, j: (0, j)),
                  pl.BlockSpec((d, tf), lambda i, j: (0, j + nf)),
                  pl.BlockSpec((tf, d), lambda i, j: (j, 0))],
        out_specs=pl.BlockSpec((tm, d), lambda i, j: (i, 0)),
        scratch_shapes=[pltpu.VMEM((tm, d), BF16), pltpu.VMEM((tm, d), F32)],
        compiler_params=_cparams("parallel", "arbitrary"),
        name="ffn",
    )(x, g.reshape(1, d), w_gu, w_gu, w_down)


def _rmsnorm_kernel(x_ref, g_ref, o_ref):
    o_ref[...] = _rmsnorm(x_ref[...], g_ref[...])


def rmsnorm_rows(x, g, *, tm=1024):
    n, d = x.shape
    tm = _row_tile(n, tm)
    return pl.pallas_call(
        _rmsnorm_kernel,
        out_shape=jax.ShapeDtypeStruct((n, d), F32),
        grid=(pl.cdiv(n, tm),),
        in_specs=[pl.BlockSpec((tm, d), lambda i: (i, 0)),
                  pl.BlockSpec((1, d), lambda i: (0, 0))],
        out_specs=pl.BlockSpec((tm, d), lambda i: (i, 0)),
        compiler_params=_cparams("parallel"),
        name="rmsnorm",
    )(x, g.reshape(1, d))


def _causal_conv_silu(raw, tail_ref, buf_ref, w_ref, bias, q):
    del tail_ref
    buf_ref[pl.ds(8, q), :] = raw
    out = buf_ref[pl.ds(5, q), :] * w_ref[0:1, :]
    out = out + buf_ref[pl.ds(6, q), :] * w_ref[1:2, :]
    out = out + buf_ref[pl.ds(7, q), :] * w_ref[2:3, :]
    out = out + raw * w_ref[3:4, :]
    if bias is not None:
        out = out + bias
    buf_ref[pl.ds(5, 3), :] = buf_ref[pl.ds(q + 5, 3), :]
    return _silu(out)


def _tri(q, strict=False):
    r = lax.broadcasted_iota(jnp.int32, (q, q), 0)
    c = lax.broadcasted_iota(jnp.int32, (q, q), 1)
    return (r > c) if strict else (r >= c)


def _mamba_kernel(z_ref, xs_ref, bc_ref, dt_ref, cx_ref, cbc_ref, h0_ref,
                  cwx_ref, cwbc_ref, cbx_ref, cbbc_ref, dtb_ref, a_ref, dskip_ref, nw_ref,
                  y_ref, h_ref, bufx_ref, bufbc_ref, *, q, t_len, n_heads):
    c = pl.program_id(1)
    n_state = MB_D_STATE
    hpg = n_heads // MB_GROUPS
    gw = hpg * MB_HEAD_DIM

    @pl.when(c == 0)
    def _():
        bufx_ref[pl.ds(5, 3), :] = cx_ref[0]
        bufbc_ref[pl.ds(5, 3), :] = cbc_ref[0]
        h_ref[0] = h0_ref[0]

    rows = c * q + lax.broadcasted_iota(jnp.int32, (q, 1), 0)
    valid = rows < t_len
    xs = _causal_conv_silu(jnp.where(valid, xs_ref[0], 0.0), None, bufx_ref, cwx_ref, cbx_ref[...], q)
    bc = _causal_conv_silu(jnp.where(valid, bc_ref[0], 0.0), None, bufbc_ref, cwbc_ref, cbbc_ref[...], q)

    lane = lax.broadcasted_iota(jnp.int32, (q, 128), 1)
    dt = _softplus(jnp.where(valid & (lane < n_heads), dt_ref[0], 0.0) + dtb_ref[...])
    dt = jnp.where(valid & (lane < n_heads), dt, 0.0)
    da = dt * a_ref[...]
    incl = _tri(q)
    acum = _dot_hi(incl.astype(F32), da)
    acum_t = acum.T
    dt_t = dt.T
    last = acum[q - 1:q, :]
    wts = jnp.exp(last - acum) * dt
    e_acum = jnp.exp(acum)
    e_last = jnp.exp(last)

    lo_half = lax.broadcasted_iota(jnp.int32, (q, 128), 1) < MB_HEAD_DIM
    lo_rows = lax.broadcasted_iota(jnp.int32, (128, 128), 0) < MB_HEAD_DIM

    z = z_ref[0]
    for g in range(MB_GROUPS):
        bm = bc[:, g * n_state:(g + 1) * n_state]
        cm = bc[:, (MB_GROUPS + g) * n_state:(MB_GROUPS + g + 1) * n_state]
        cb = _dot_nt(cm, bm)
        hg = h_ref[0, g * gw:(g + 1) * gw, :]
        ch = _dot_nt(cm, hg)
        ys = []
        for p in range(hpg // 2):
            h_e = g * hpg + 2 * p
            x_pair = xs[:, h_e * MB_HEAD_DIM:(h_e + 2) * MB_HEAD_DIM]
            outs = []
            for hh in (h_e, h_e + 1):
                seg = acum[:, hh:hh + 1] - acum_t[hh:hh + 1, :]
                decay = jnp.where(incl, jnp.exp(jnp.where(incl, seg, 0.0)), 0.0)
                m = cb * decay * dt_t[hh:hh + 1, :]
                outs.append(_dot(m, x_pair))
            y_pair = jnp.where(lo_half, outs[0], outs[1])
            e_pair = jnp.where(lo_half, e_acum[:, h_e:h_e + 1], e_acum[:, h_e + 1:h_e + 2])
            y_pair = y_pair + e_pair * ch[:, 2 * p * MB_HEAD_DIM:(2 * p + 2) * MB_HEAD_DIM]
            ys.append(y_pair)
            w_pair = jnp.where(lo_half, wts[:, h_e:h_e + 1], wts[:, h_e + 1:h_e + 2])
            upd = _dot_tn(x_pair * w_pair, bm)
            scale = jnp.where(lo_rows, e_last[:, h_e:h_e + 1], e_last[:, h_e + 1:h_e + 2])
            r0 = h_e * MB_HEAD_DIM
            h_ref[0, r0:r0 + 128, :] = h_ref[0, r0:r0 + 128, :] * scale + upd
        yg = jnp.concatenate(ys, axis=1)
        xg = xs[:, g * gw:(g + 1) * gw]
        yg = yg + xg * dskip_ref[:, g * gw:(g + 1) * gw]
        yg = yg * _silu(z[:, g * gw:(g + 1) * gw])
        yg = yg * lax.rsqrt(jnp.mean(yg * yg, axis=-1, keepdims=True) + EPS)
        y_ref[0, :, g * gw:(g + 1) * gw] = (yg * nw_ref[:, g * gw:(g + 1) * gw]).astype(y_ref.dtype)


def mamba_core(proj, t_valid, conv_buf, h0, conv_w, conv_b, dt_bias, a_log, d_skip, norm_w, *, chunk=128):
    bsz, t_len, _ = proj.shape
    n_heads = dt_bias.shape[0]
    d_inner = n_heads * MB_HEAD_DIM
    gn = MB_GROUPS * MB_D_STATE
    assert d_inner % 1024 == 0 and 2 * gn == 1024 and n_heads <= 128
    q = min(chunk, t_len)
    assert q % 8 == 0 and t_len % q == 0
    n_chunks = t_len // q
    dtb = jnp.zeros((1, 128), F32).at[0, :n_heads].set(dt_bias)
    a_row = jnp.zeros((1, 128), F32).at[0, :n_heads].set(-jnp.exp(a_log))
    dskip = jnp.repeat(d_skip, MB_HEAD_DIM).reshape(1, d_inner)
    kern = functools.partial(_mamba_kernel, q=q, t_len=t_valid, n_heads=n_heads)
    xblk = d_inner // 1024
    y, h = pl.pallas_call(
        kern,
        out_shape=(jax.ShapeDtypeStruct((bsz, t_len, d_inner), BF16),
                   jax.ShapeDtypeStruct((bsz, n_heads * MB_HEAD_DIM, MB_D_STATE), F32)),
        grid=(bsz, n_chunks),
        in_specs=[
            pl.BlockSpec((1, q, d_inner), lambda b, c: (b, c, 0)),
            pl.BlockSpec((1, q, d_inner), lambda b, c: (b, c, 1)),
            pl.BlockSpec((1, q, 2 * gn), lambda b, c: (b, c, 2 * xblk)),
            pl.BlockSpec((1, q, 128), lambda b, c: (b, c, (2 * d_inner + 2 * gn) // 128)),
            pl.BlockSpec((1, 3, d_inner), lambda b, c: (b, 0, 0)),
            pl.BlockSpec((1, 3, 2 * gn), lambda b, c: (b, 0, xblk)),
            pl.BlockSpec((1, n_heads * MB_HEAD_DIM, MB_D_STATE), lambda b, c: (b, 0, 0)),
            pl.BlockSpec((4, d_inner), lambda b, c: (0, 0)),
            pl.BlockSpec((4, 2 * gn), lambda b, c: (0, xblk)),
            pl.BlockSpec((1, d_inner), lambda b, c: (0, 0)),
            pl.BlockSpec((1, 2 * gn), lambda b, c: (0, xblk)),
            pl.BlockSpec((1, 128), lambda b, c: (0, 0)),
            pl.BlockSpec((1, 128), lambda b, c: (0, 0)),
            pl.BlockSpec((1, d_inner), lambda b, c: (0, 0)),
            pl.BlockSpec((1, d_inner), lambda b, c: (0, 0)),
        ],
        out_specs=(pl.BlockSpec((1, q, d_inner), lambda b, c: (b, c, 0)),
                   pl.BlockSpec((1, n_heads * MB_HEAD_DIM, MB_D_STATE), lambda b, c: (b, 0, 0))),
        scratch_shapes=[pltpu.VMEM((q + 8, d_inner), F32), pltpu.VMEM((q + 8, 2 * gn), F32)],
        compiler_params=_cparams("parallel", "arbitrary"),
        name="mamba_core",
    )(proj, proj, proj, proj, conv_buf, conv_buf, h0.reshape(bsz, n_heads * MB_HEAD_DIM, MB_D_STATE),
      conv_w, conv_w, conv_b.reshape(1, -1), conv_b.reshape(1, -1), dtb, a_row, dskip,
      norm_w.reshape(1, d_inner))
    return y, h.reshape(bsz, n_heads, MB_HEAD_DIM, MB_D_STATE)


def mamba2_layer(x, conv_buf, ssm_state, norm_g, w_in, conv_w, conv_b, dt_bias, a_log, d_skip,
                 norm_w, w_out):
    return mamba2_layer_padded(x, x.shape[1], conv_buf, ssm_state, norm_g, w_in, conv_w, conv_b, dt_bias,
                               a_log, d_skip, norm_w, w_out)


def mamba2_layer_padded(x, t_valid, conv_buf, ssm_state, norm_g, w_in, conv_w, conv_b, dt_bias, a_log,
                        d_skip, norm_w, w_out):
    bsz, t_len, d = x.shape
    n_heads = dt_bias.shape[0]
    d_inner = n_heads * MB_HEAD_DIM
    conv_dim = conv_w.shape[1]
    x2 = x.reshape(bsz * t_len, d)
    proj = norm_matmul(x2, norm_g, w_in).reshape(bsz, t_len, -1)
    y, h_new = mamba_core(proj, t_valid, conv_buf, ssm_state, conv_w, conv_b, dt_bias, a_log, d_skip, norm_w)
    conv_new = proj[:, t_valid - 3:t_valid, d_inner:d_inner + conv_dim]
    out = matmul_res(y.reshape(bsz * t_len, d_inner), w_out, x2)
    return out.reshape(bsz, t_len, d), conv_new, h_new


def _split(a):
    hi = a.astype(BF16)
    lo = (a - hi.astype(F32)).astype(BF16)
    return hi, lo


def _mm(a, b):
    return jnp.dot(a, b, preferred_element_type=F32)


def _dot3(a, b):
    ah, al = _split(a)
    bh, bl = _split(b)
    return _mm(ah, bh) + (_mm(ah, bl) + _mm(al, bh))


def _unit_lower_inverse(mats):
    n = mats[0].shape[0]
    eye = (lax.broadcasted_iota(jnp.int32, (n, n), 0) == lax.broadcasted_iota(jnp.int32, (n, n), 1))
    eye = jnp.where(eye, 1.0, 0.0)
    xs = [eye - a for a in mats]
    ps = list(mats)
    k = 1
    while 2 * k < n:
        ps = [_dot3(p, p) for p in ps]
        xs = [x + _dot3(x, p) for x, p in zip(xs, ps)]
        k *= 2
    return xs


def _gdn_kernel(q_ref, k_ref, v_ref, z_ref, ba_ref, cq_ref, ck_ref, cv_ref, s0_ref,
                cwq_ref, cwk_ref, cwv_ref, dtb_ref, al_ref, nw_ref,
                o_ref, s_ref, bufq_ref, bufk_ref, bufv_ref, *, q, t_len, n_hv, n_hk):
    c = pl.program_id(1)
    rep = n_hv // n_hk

    @pl.when(c == 0)
    def _():
        bufq_ref[pl.ds(5, 3), :] = cq_ref[0]
        bufk_ref[pl.ds(5, 3), :] = ck_ref[0]
        bufv_ref[pl.ds(5, 3), :] = cv_ref[0]
        s_ref[0] = s0_ref[0]

    rows = c * q + lax.broadcasted_iota(jnp.int32, (q, 1), 0)
    valid = rows < t_len
    qc = _causal_conv_silu(jnp.where(valid, q_ref[0], 0.0), None, bufq_ref, cwq_ref, None, q)
    kc = _causal_conv_silu(jnp.where(valid, k_ref[0], 0.0), None, bufk_ref, cwk_ref, None, q)
    vc = _causal_conv_silu(jnp.where(valid, v_ref[0], 0.0), None, bufv_ref, cwv_ref, None, q)

    lane = lax.broadcasted_iota(jnp.int32, (q, 128), 1)
    ba = jnp.where(valid & (lane < 2 * n_hv), ba_ref[0], 0.0)
    beta = jnp.where(valid & (lane < n_hv), jax.nn.sigmoid(ba), 0.0)
    g = jnp.where(valid & (lane >= n_hv) & (lane < 2 * n_hv),
                  -jnp.exp(al_ref[...]) * _softplus(ba + dtb_ref[...]), 0.0)
    incl = _tri(q)
    strict = _tri(q, strict=True)
    gcum = _dot_hi(incl.astype(F32), g)
    gcum_t = gcum.T
    eg = jnp.exp(gcum)
    glast = gcum[q - 1:q, :]
    w_last = jnp.exp(glast - gcum)
    e_last = jnp.exp(glast)

    z = z_ref[0]
    qn, kn, kk, qk0 = [], [], [], []
    for kh in range(n_hk):
        qh = qc[:, kh * GDN_DK:(kh + 1) * GDN_DK]
        kh_ = kc[:, kh * GDN_DK:(kh + 1) * GDN_DK]
        qh = qh * (lax.rsqrt(jnp.sum(qh * qh, axis=-1, keepdims=True) + 1e-6) * (GDN_DK ** -0.5))
        kh_ = kh_ * lax.rsqrt(jnp.sum(kh_ * kh_, axis=-1, keepdims=True) + 1e-6)
        qn.append(qh)
        kn.append(kh_)
        kk.append(_dot_nt(kh_, kh_))
        qk0.append(_dot_nt(qh, kh_))
    heads = range(n_hv)
    s_old = [s_ref[0, h * GDN_DK:(h + 1) * GDN_DK, :] for h in heads]
    ks = [_dot(kn[h // rep], s_old[h]) for h in heads]
    qs = [_dot(qn[h // rep], s_old[h]) for h in heads]
    dec, a_mat, rhs = [], [], []
    for h in heads:
        hl = n_hv + h
        seg = gcum[:, hl:hl + 1] - gcum_t[hl:hl + 1, :]
        dec.append(jnp.where(incl, jnp.exp(jnp.where(incl, seg, 0.0)), 0.0))
        bt = beta[:, h:h + 1]
        a_mat.append(jnp.where(strict, kk[h // rep] * dec[h], 0.0) * bt)
        rhs.append(bt * (vc[:, h * GDN_DV:(h + 1) * GDN_DV] - eg[:, hl:hl + 1] * ks[h]))
    t_inv = _unit_lower_inverse(a_mat)
    delta = [_dot3(t_inv[h], rhs[h]) for h in heads]
    o_mm = [_dot(qk0[h // rep] * dec[h], delta[h]) for h in heads]
    upd = [_dot_tn(kn[h // rep] * w_last[:, n_hv + h:n_hv + h + 1], delta[h]) for h in heads]
    for h in heads:
        hl = n_hv + h
        s_ref[0, h * GDN_DK:(h + 1) * GDN_DK, :] = s_old[h] * e_last[:, hl:hl + 1] + upd[h]
        o = eg[:, hl:hl + 1] * qs[h] + o_mm[h]
        o = o * lax.rsqrt(jnp.mean(o * o, axis=-1, keepdims=True) + EPS) * nw_ref[...]
        o = o * _silu(z[:, h * GDN_DV:(h + 1) * GDN_DV])
        o_ref[0, :, h * GDN_DV:(h + 1) * GDN_DV] = o.astype(o_ref.dtype)


def gdn_core(proj, t_valid, conv_buf, s0, conv_w, dt_bias, a_log, norm_w, *, chunk=64):
    bsz, t_len, _ = proj.shape
    n_hv = dt_bias.shape[0]
    val_dim = n_hv * GDN_DV
    key_dim = (conv_w.shape[1] - val_dim) // 2
    n_hk = key_dim // GDN_DK
    assert val_dim == 2 * key_dim and key_dim % 128 == 0 and 2 * n_hv <= 128
    q = min(chunk, t_len)
    assert q % 8 == 0 and t_len % q == 0
    n_chunks = t_len // q
    pad = jnp.zeros((n_hv,), F32)
    dtb = jnp.zeros((1, 128), F32).at[0, n_hv:2 * n_hv].set(dt_bias)
    alog = jnp.zeros((1, 128), F32).at[0, n_hv:2 * n_hv].set(a_log)
    del pad
    kern = functools.partial(_gdn_kernel, q=q, t_len=t_valid, n_hv=n_hv, n_hk=n_hk)
    ba_blk = (2 * key_dim + 2 * val_dim) // 128
    s0 = s0.reshape(bsz, n_hv * GDN_DK, GDN_DV)
    o, s = pl.pallas_call(
        kern,
        out_shape=(jax.ShapeDtypeStruct((bsz, t_len, val_dim), BF16),
                   jax.ShapeDtypeStruct(s0.shape, F32)),
        grid=(bsz, n_chunks),
        in_specs=[
            pl.BlockSpec((1, q, key_dim), lambda b, c: (b, c, 0)),
            pl.BlockSpec((1, q, key_dim), lambda b, c: (b, c, 1)),
            pl.BlockSpec((1, q, val_dim), lambda b, c: (b, c, 1)),
            pl.BlockSpec((1, q, val_dim), lambda b, c: (b, c, 2)),
            pl.BlockSpec((1, q, 128), lambda b, c: (b, c, ba_blk)),
            pl.BlockSpec((1, 3, key_dim), lambda b, c: (b, 0, 0)),
            pl.BlockSpec((1, 3, key_dim), lambda b, c: (b, 0, 1)),
            pl.BlockSpec((1, 3, val_dim), lambda b, c: (b, 0, 1)),
            pl.BlockSpec((1,) + s0.shape[1:], lambda b, c: (b, 0, 0)),
            pl.BlockSpec((4, key_dim), lambda b, c: (0, 0)),
            pl.BlockSpec((4, key_dim), lambda b, c: (0, 1)),
            pl.BlockSpec((4, val_dim), lambda b, c: (0, 1)),
            pl.BlockSpec((1, 128), lambda b, c: (0, 0)),
            pl.BlockSpec((1, 128), lambda b, c: (0, 0)),
            pl.BlockSpec((1, GDN_DV), lambda b, c: (0, 0)),
        ],
        out_specs=(pl.BlockSpec((1, q, val_dim), lambda b, c: (b, c, 0)),
                   pl.BlockSpec((1,) + s0.shape[1:], lambda b, c: (b, 0, 0))),
        scratch_shapes=[pltpu.VMEM((q + 8, key_dim), F32), pltpu.VMEM((q + 8, key_dim), F32),
                        pltpu.VMEM((q + 8, val_dim), F32)],
        compiler_params=_cparams("parallel", "arbitrary"),
        name="gdn_core",
    )(proj, proj, proj, proj, proj, conv_buf, conv_buf, conv_buf, s0,
      conv_w, conv_w, conv_w, dtb, alog, norm_w.reshape(1, GDN_DV))
    return o, s.reshape(bsz, n_hv, GDN_DK, GDN_DV)


def gdn_layer(x, conv_buf, s0, norm_g, w_in, conv_w, dt_bias, a_log, norm_w, w_out):
    return gdn_layer_padded(x, x.shape[1], conv_buf, s0, norm_g, w_in, conv_w, dt_bias, a_log, norm_w, w_out)


def gdn_layer_padded(x, t_valid, conv_buf, s0, norm_g, w_in, conv_w, dt_bias, a_log, norm_w, w_out):
    bsz, t_len, d = x.shape
    conv_dim = conv_w.shape[1]
    x2 = x.reshape(bsz * t_len, d)
    proj = norm_matmul(x2, norm_g, w_in).reshape(bsz, t_len, -1)
    o, s_new = gdn_core(proj, t_valid, conv_buf, s0, conv_w, dt_bias, a_log, norm_w)
    conv_new = proj[:, t_valid - 3:t_valid, :conv_dim]
    out = matmul_res(o.reshape(bsz * t_len, -1), w_out, x2)
    return out.reshape(bsz, t_len, d), conv_new, s_new


def _head_ones(width, head):
    r = lax.broadcasted_iota(jnp.int32, (width, width), 0) // head
    c = lax.broadcasted_iota(jnp.int32, (width, width), 1) // head
    return jnp.where(r == c, 1.0, 0.0).astype(BF16)


def _head_sum(x, ones):
    hi, lo = _split(x)
    return _mm(hi, ones) + _mm(lo, ones)


def _rw_proj_kernel(hn_ref, pv_ref, mu_ref, wr_ref, wk_ref, wv_ref, w1_ref, w2_ref, a1_ref, a2_ref,
                    g1_ref, g2_ref, w0_ref, a0_ref, kk_ref, ka_ref,
                    r_ref, lw_ref, k_ref, v_ref, kn_ref, b_ref, g_ref):
    hn = hn_ref[...]
    dlt = pv_ref[...] - hn

    def mix(c):
        return (hn + dlt * mu_ref[c:c + 1, :]).astype(BF16)

    r_ref[...] = _mm(mix(0), wr_ref[...])
    k = _mm(mix(1), wk_ref[...])
    v_ref[...] = _mm(mix(2), wv_ref[...])
    dec = w0_ref[...] + _mm(jnp.tanh(_mm(mix(3), w1_ref[...])).astype(BF16), w2_ref[...])
    lw_ref[...] = -RW_DECAY_SCALE * jax.nn.sigmoid(dec)
    a = jax.nn.sigmoid(a0_ref[...] + _mm(_mm(mix(4), a1_ref[...]).astype(BF16), a2_ref[...]))
    g_ref[...] = _mm(jax.nn.sigmoid(_mm(mix(5), g1_ref[...])).astype(BF16), g2_ref[...])
    ones = _head_ones(128, RW_HEAD_DIM)
    kn = k * kk_ref[...]
    d = kn.shape[1]
    for j in range(d // 128):
        sl = slice(j * 128, (j + 1) * 128)
        knj = kn[:, sl]
        knj = knj * lax.rsqrt(_head_sum(knj * knj, ones) + 1e-6)
        kn_ref[:, sl] = knj
        b_ref[:, sl] = knj * a[:, sl]
    k_ref[...] = k * (1.0 + (a - 1.0) * ka_ref[...])


def rw_proj(hn, prev, p, *, tm=256):
    n, d = hn.shape
    tm = _row_tile(n, tm)
    row = lambda i: (i, 0)
    fix = lambda i: (0, 0)
    big = pl.BlockSpec((tm, d), row)
    vec = pl.BlockSpec((1, d), fix)

    def full(a):
        return pl.BlockSpec(a.shape, fix)

    ws = [p['rw_w_rkv'][0].astype(BF16), p['rw_w_rkv'][1].astype(BF16), p['rw_w_rkv'][2].astype(BF16),
          p['rw_w1'].astype(BF16), p['rw_w2'].astype(BF16), p['rw_a1'].astype(BF16), p['rw_a2'].astype(BF16),
          p['rw_g1'].astype(BF16), p['rw_g2'].astype(BF16)]
    vecs = [p['rw_w0'].reshape(1, d), p['rw_a0'].reshape(1, d), p['rw_k_k'].reshape(1, d),
            p['rw_k_a'].reshape(1, d)]
    return pl.pallas_call(
        _rw_proj_kernel,
        out_shape=tuple(jax.ShapeDtypeStruct((n, d), F32) for _ in range(7)),
        grid=(pl.cdiv(n, tm),),
        in_specs=[big, big, full(p['rw_mu'])] + [full(w) for w in ws] + [vec] * 4,
        out_specs=tuple(big for _ in range(7)),
        compiler_params=_cparams("parallel"),
        name="rw_proj",
    )(hn, prev, p['rw_mu'], *ws, *vecs)


def _rw_scan_kernel(r_ref, lw_ref, k_ref, v_ref, kn_ref, b_ref, g_ref, s0_ref, rk_ref, lnw_ref, lnb_ref,
                    o_ref, s_ref, *, q, t_len):
    c = pl.program_id(1)

    @pl.when(c == 0)
    def _():
        s_ref[0] = s0_ref[0]

    rows = c * q + lax.broadcasted_iota(jnp.int32, (q, 1), 0)
    valid = rows < t_len
    incl = _tri(q)
    strict = _tri(q, strict=True)
    lw_all = jnp.where(valid, lw_ref[0], 0.0)
    cum_all = _dot_hi(incl.astype(F32), lw_all)
    ones = _head_ones(128, RW_HEAD_DIM)
    lo = lax.broadcasted_iota(jnp.int32, (q, 128), 1) < RW_HEAD_DIM
    blockdiag = ((lax.broadcasted_iota(jnp.int32, (128, 128), 0) < RW_HEAD_DIM)
                 == (lax.broadcasted_iota(jnp.int32, (128, 128), 1) < RW_HEAD_DIM))
    d = lw_all.shape[1]
    pairs = range(d // 128)
    sls = [slice(p * 128, (p + 1) * 128) for p in pairs]
    rs, ks, vs, bends, kends, s_old, cum_ends = [], [], [], [], [], [], []
    pk, pb, u0, y0 = [], [], [], []
    for p in pairs:
        sl = sls[p]
        lw = lw_all[:, sl]
        cum = cum_all[:, sl]
        cum_end = cum[q - 1:q, :]
        r = jnp.where(valid, r_ref[0, :, sl], 0.0)
        k = jnp.where(valid, k_ref[0, :, sl], 0.0)
        v = jnp.where(valid, v_ref[0, :, sl], 0.0)
        kn = jnp.where(valid, kn_ref[0, :, sl], 0.0)
        b = jnp.where(valid, b_ref[0, :, sl], 0.0)
        kq = kn * jnp.exp(cum - lw)
        rq = r * jnp.exp(cum)
        einv = jnp.exp(-cum)
        kd = k * einv
        bd = b * einv
        eend = jnp.exp(cum_end - cum)
        s2 = s_ref[0, sl, :]
        lhs = jnp.concatenate([jnp.where(lo, kq, 0.0), jnp.where(lo, 0.0, kq),
                               jnp.where(lo, rq, 0.0), jnp.where(lo, 0.0, rq)], axis=0)
        pk.append(_dot_nt(lhs, kd))
        pb.append(_dot_nt(lhs, bd))
        u0.append(_dot_nt(kq, s2))
        y0.append(_dot_nt(rq, s2))
        rs.append(r)
        ks.append(k)
        vs.append(v)
        kends.append(k * eend)
        bends.append(b * eend)
        s_old.append(s2)
        cum_ends.append(cum_end)
    ab = [jnp.where(strict, pb[p][e * q:(e + 1) * q], 0.0) for p in pairs for e in range(2)]
    t_inv = _unit_lower_inverse(ab)
    rhs = [u0[p] + jnp.where(lo, _dot(jnp.where(strict, pk[p][0:q], 0.0), vs[p]),
                             _dot(jnp.where(strict, pk[p][q:2 * q], 0.0), vs[p])) for p in pairs]
    us = [jnp.where(lo, _dot3(t_inv[2 * p], rhs[p]), _dot3(t_inv[2 * p + 1], rhs[p])) for p in pairs]
    ys = []
    for p in pairs:
        rk = [jnp.where(incl, pk[p][(2 + e) * q:(3 + e) * q], 0.0) for e in range(2)]
        rb = [jnp.where(incl, pb[p][(2 + e) * q:(3 + e) * q], 0.0) for e in range(2)]
        ys.append(y0[p] + jnp.where(lo, _dot(rk[0], vs[p]) - _dot(rb[0], us[p]),
                                    _dot(rk[1], vs[p]) - _dot(rb[1], us[p])))
    s_new = [s_old[p] * jnp.exp(cum_ends[p]) + _dot_tn(vs[p], kends[p]) - _dot_tn(us[p], bends[p])
             for p in pairs]
    for p in pairs:
        sl = sls[p]
        s_ref[0, sl, :] = jnp.where(blockdiag, s_new[p], 0.0)
        y = ys[p]
        mean = _head_sum(y, ones) * (1.0 / RW_HEAD_DIM)
        yc = y - mean
        var = _head_sum(yc * yc, ones) * (1.0 / RW_HEAD_DIM)
        yn = yc * lax.rsqrt(var + RW_GN_EPS) * lnw_ref[:, sl] + lnb_ref[:, sl]
        bonus = _head_sum(rs[p] * ks[p] * rk_ref[:, sl], ones) * vs[p]
        o_ref[0, :, sl] = ((yn + bonus) * g_ref[0, :, sl]).astype(o_ref.dtype)


def rw_scan(r, lw, k, v, kn, b, g, t_valid, s0, r_k, ln_w, ln_b, *, chunk=64):
    bsz, t_len, d = r.shape
    q = min(chunk, t_len)
    assert q % 8 == 0 and t_len % q == 0 and d % 128 == 0
    n_heads = d // RW_HEAD_DIM
    s5 = s0.reshape(bsz, n_heads // 2, 2, RW_HEAD_DIM, RW_HEAD_DIM)
    s2 = jnp.einsum('bpevk,ef->bpevfk', s5, jnp.eye(2, dtype=F32)).reshape(bsz, d, 128)
    blk = pl.BlockSpec((1, q, d), lambda bb, c: (bb, c, 0))
    vec = pl.BlockSpec((1, d), lambda bb, c: (0, 0))
    st = pl.BlockSpec((1, d, 128), lambda bb, c: (bb, 0, 0))
    o, s_new = pl.pallas_call(
        functools.partial(_rw_scan_kernel, q=q, t_len=t_valid),
        out_shape=(jax.ShapeDtypeStruct((bsz, t_len, d), BF16), jax.ShapeDtypeStruct((bsz, d, 128), F32)),
        grid=(bsz, t_len // q),
        in_specs=[blk] * 7 + [st, vec, vec, vec],
        out_specs=(blk, st),
        compiler_params=_cparams("parallel", "arbitrary"),
        name="rw_scan",
    )(r, lw, k, v, kn, b, g, s2, r_k.reshape(1, d), ln_w.reshape(1, d), ln_b.reshape(1, d))
    s6 = s_new.reshape(bsz, n_heads // 2, 2, RW_HEAD_DIM, 2, RW_HEAD_DIM)
    s_out = jnp.stack([s6[:, :, 0, :, 0, :], s6[:, :, 1, :, 1, :]], axis=2)
    return o, s_out.reshape(bsz, n_heads, RW_HEAD_DIM, RW_HEAD_DIM)


def rwkv_layer_padded(x, t_valid, shift_buf, s0, norm_g, p):
    bsz, t_len, d = x.shape
    x2 = x.reshape(bsz * t_len, d)
    hn = rmsnorm_rows(x2, norm_g).reshape(bsz, t_len, d)
    prev = jnp.concatenate([shift_buf, hn[:, :-1]], axis=1)
    outs = rw_proj(hn.reshape(-1, d), prev.reshape(-1, d), p)
    r, lw, k, v, kn, b, g = (a.reshape(bsz, t_len, d) for a in outs)
    yg, s_new = rw_scan(r, lw, k, v, kn, b, g, t_valid, s0, p['rw_r_k'], p['rw_ln_w'], p['rw_ln_b'])
    out = matmul_res(yg.reshape(bsz * t_len, d), p['rw_w_out'].astype(BF16), x2)
    return out.reshape(bsz, t_len, d), hn[:, t_valid - 1:t_valid], s_new


def _dot3_nt(a, b):
    ah, al = _split(a)
    bh, bl = _split(b)
    dn = (((1,), (1,)), ((), ()))
    f = lambda u, w: lax.dot_general(u, w, dn, preferred_element_type=F32)
    return f(ah, bh) + (f(ah, bl) + f(al, bh))


def _suffix_ones(n):
    r = lax.broadcasted_iota(jnp.int32, (n, n), 0)
    c = lax.broadcasted_iota(jnp.int32, (n, n), 1)
    return jnp.where(r >= c, 1.0, 0.0).astype(BF16)


def _sb_block(q_e, kblk, vblk, carry, acc, lo, suffix, mask):
    outs = []
    new_carry = []
    for e in range(2):
        z = _dot3_nt(q_e[e], kblk)
        lnb = -_softplus(z)
        if mask is not None:
            lnb = jnp.where(mask, lnb, 0.0)
        hi, lw = _split(lnb)
        rsum = _mm(hi, suffix) + _mm(lw, suffix) + carry[e]
        att = jnp.exp(z + rsum)
        if mask is not None:
            att = jnp.where(mask, att, 0.0)
        outs.append(_dot(att, vblk))
        new_carry.append(rsum[:, 0:1])
    return new_carry, acc + jnp.where(lo, outs[0], outs[1])


def _sb_kernel(*refs, bq, bp, n_past, scale):
    if n_past:
        q_ref, kn_ref, vn_ref, kp_ref, vp_ref, o_ref, _ = refs
    else:
        q_ref, kn_ref, vn_ref, o_ref, _ = refs
    i = pl.program_id(2)
    kmax_ref = refs[-1]
    t_new = kn_ref.shape[1]

    def abs_max(ref, n_rows, blk, m8):
        def body(j, m):
            rows = ref[0, pl.ds(pl.multiple_of(j * blk, blk), blk), :]
            return jnp.maximum(m, jnp.max(jnp.abs(rows).reshape(blk // 8, 8, 128), axis=0))
        return lax.fori_loop(0, n_rows // blk, body, m8)

    @pl.when(i == 0)
    def _():
        m8 = abs_max(kn_ref, t_new, bq, jnp.zeros((8, 128), F32))
        if n_past:
            m8 = abs_max(kp_ref, n_past * bp, bp, m8)
        kmax_ref[...] = jnp.max(m8, axis=0, keepdims=True)

    lo = lax.broadcasted_iota(jnp.int32, (bq, 128), 1) < SB_HEAD_DIM
    q = q_ref[0] * scale
    q_e = [jnp.where(lo, q, 0.0), jnp.where(lo, 0.0, q)]
    zb = _head_sum(jnp.abs(q) * kmax_ref[...], _head_ones(128, SB_HEAD_DIM))
    zb0 = zb[:, 0:1]
    zb1 = zb[:, SB_HEAD_DIM:SB_HEAD_DIM + 1]
    suffix = _suffix_ones(bq)
    mask = _tri(bq, strict=True)
    zero_c = jnp.zeros((bq, 1), F32)
    start = pl.multiple_of(i * bq, bq)
    carry, acc = _sb_block(q_e, kn_ref[0, pl.ds(start, bq), :], vn_ref[0, pl.ds(start, bq), :],
                           [zero_c, zero_c], jnp.zeros((bq, 128), F32), lo, suffix, mask)

    def live(c0, c1):
        return (jnp.max(jnp.maximum(c0 + zb0, c1 + zb1)) > SB_LOG_CUT).astype(jnp.int32)

    def walk(k_ref, v_ref, n_blocks, blk, sfx, st):
        def cond(s):
            return (s[0] < n_blocks) & (s[4] > 0)

        def body(s):
            jj, c0, c1, a, _ = s
            off = pl.multiple_of((n_blocks - 1 - jj) * blk, blk)
            (c0, c1), a = _sb_block(q_e, k_ref[0, pl.ds(off, blk), :], v_ref[0, pl.ds(off, blk), :],
                                    [c0, c1], a, lo, sfx, None)
            return jj + 1, c0, c1, a, live(c0, c1)

        return lax.while_loop(cond, body, st)

    st = walk(kn_ref, vn_ref, i, bq, suffix, (jnp.int32(0), carry[0], carry[1], acc, live(*carry)))
    if n_past:
        suffix_p = suffix if bp == bq else _suffix_ones(bp)
        st = walk(kp_ref, vp_ref, n_past, bp, suffix_p, (jnp.int32(0),) + st[1:])
    o_ref[0] = st[3].astype(o_ref.dtype)


def sb_attention(qkv, k_past, v_past, *, block=256):
    bsz, t_len, d3 = qkv.shape
    d = d3 // 3
    assert d % 128 == 0
    n_pairs = d // 128
    bq = min(block, t_len)
    assert t_len % bq == 0 and bq % 8 == 0
    args = [qkv, qkv, qkv]
    in_specs = [pl.BlockSpec((1, bq, 128), lambda b, p, i: (b, i, p)),
                pl.BlockSpec((1, t_len, 128), lambda b, p, i: (b, 0, n_pairs + p)),
                pl.BlockSpec((1, t_len, 128), lambda b, p, i: (b, 0, 2 * n_pairs + p))]
    n_past = bp = 0
    if k_past is not None:
        past_len = k_past.shape[1]
        bp = min(block, past_len)
        assert past_len % bp == 0
        n_past = past_len // bp
        args += [k_past, v_past]
        in_specs += [pl.BlockSpec((1, past_len, 128), lambda b, p, i: (b, 0, p))] * 2
    return pl.pallas_call(
        functools.partial(_sb_kernel, bq=bq, bp=bp, n_past=n_past, scale=SB_HEAD_DIM ** -0.5),
        out_shape=jax.ShapeDtypeStruct((bsz, t_len, d), BF16),
        grid=(bsz, n_pairs, t_len // bq),
        in_specs=in_specs,
        out_specs=pl.BlockSpec((1, bq, 128), lambda b, p, i: (b, i, p)),
        compiler_params=_cparams("parallel", "parallel", "arbitrary"),
        scratch_shapes=[pltpu.VMEM((1, 128), F32)],
        name="sb_attention",
    )(*args)


def sb_layer_padded(x, t_valid, k_past, v_past, norm_g, w_qkv, w_out):
    bsz, t_len, d = x.shape
    x2 = x.reshape(bsz * t_len, d)
    qkv = norm_matmul(x2, norm_g, w_qkv).reshape(bsz, t_len, 3 * d)
    if k_past is not None:
        k_past = k_past.reshape(bsz, -1, d)
        v_past = v_past.reshape(bsz, -1, d)
    o = sb_attention(qkv, k_past, v_past)
    out = matmul_res(o.reshape(bsz * t_len, d), w_out, x2)
    n_heads = d // SB_HEAD_DIM
    k_new = qkv[:, :t_valid, d:2 * d].reshape(bsz, t_valid, n_heads, SB_HEAD_DIM)
    v_new = qkv[:, :t_valid, 2 * d:].reshape(bsz, t_valid, n_heads, SB_HEAD_DIM)
    return out.reshape(bsz, t_len, d), k_new, v_new


def _run_trunk(x, t_valid, st, p):
    bsz, t_len, d = x.shape
    new = {}
    bf = lambda a: a.astype(BF16)

    def ffn_layer(x, i):
        return ffn(x.reshape(bsz * t_len, d), p['norm_ffn'][i], bf(p['ffn_w_gu'][i]),
                   bf(p['ffn_w_down'][i])).reshape(bsz, t_len, d)

    x, new['ssm_conv'], new['ssm'] = mamba2_layer_padded(
        x, t_valid, st['ssm_conv'], st['ssm'], p['norm_mix'][0], bf(p['mb_w_in']), p['mb_conv_w'],
        p['mb_conv_b'], p['mb_dt_bias'], p['mb_a_log'], p['mb_d'], p['mb_norm'], bf(p['mb_w_out']))
    x = ffn_layer(x, 0)
    x, new['gdn_conv'], new['gdn'] = gdn_layer_padded(
        x, t_valid, st['gdn_conv'], st['gdn'], p['norm_mix'][1], bf(p['gdn_w_in']), p['gdn_conv_w'],
        p['gdn_dt_bias'], p['gdn_a_log'], p['gdn_norm'], bf(p['gdn_w_out']))
    x = ffn_layer(x, 1)
    x, new['rwkv_shift'], new['rwkv'] = rwkv_layer_padded(
        x, t_valid, st['rwkv_shift'], st['rwkv'], p['norm_mix'][2], p)
    x = ffn_layer(x, 2)
    x, new['sb_k'], new['sb_v'] = sb_layer_padded(
        x, t_valid, st['sb_k'], st['sb_v'], p['norm_mix'][3], bf(p['sb_w_qkv']), bf(p['sb_w_out']))
    x = ffn_layer(x, 3)
    y = rmsnorm_rows(x.reshape(bsz * t_len, d), p['norm_final']).reshape(bsz, t_len, d)
    return y, new


def kernel(x_prompt, x_sample, state_ssm, state_ssm_conv, state_gdn, state_gdn_conv, state_rwkv, state_rwkv_shift, cache_sb_k, cache_sb_v, meta_tokens, norm_mix, norm_ffn, norm_final, ffn_w_gu, ffn_w_down, mb_w_in, mb_conv_w, mb_conv_b, mb_dt_bias, mb_a_log, mb_d, mb_norm, mb_w_out, gdn_w_in, gdn_conv_w, gdn_dt_bias, gdn_a_log, gdn_norm, gdn_w_out, rw_mu, rw_w_rkv, rw_w0, rw_w1, rw_w2, rw_a0, rw_a1, rw_a2, rw_g1, rw_g2, rw_k_k, rw_k_a, rw_r_k, rw_ln_w, rw_ln_b, rw_w_out, sb_w_qkv, sb_w_out):
    p = dict(
        norm_mix=norm_mix, norm_ffn=norm_ffn, norm_final=norm_final, ffn_w_gu=ffn_w_gu, ffn_w_down=ffn_w_down,
        mb_w_in=mb_w_in, mb_conv_w=mb_conv_w, mb_conv_b=mb_conv_b, mb_dt_bias=mb_dt_bias, mb_a_log=mb_a_log,
        mb_d=mb_d, mb_norm=mb_norm, mb_w_out=mb_w_out,
        gdn_w_in=gdn_w_in, gdn_conv_w=gdn_conv_w, gdn_dt_bias=gdn_dt_bias, gdn_a_log=gdn_a_log,
        gdn_norm=gdn_norm, gdn_w_out=gdn_w_out,
        rw_mu=rw_mu, rw_w_rkv=rw_w_rkv, rw_w0=rw_w0, rw_w1=rw_w1, rw_w2=rw_w2, rw_a0=rw_a0, rw_a1=rw_a1,
        rw_a2=rw_a2, rw_g1=rw_g1, rw_g2=rw_g2, rw_k_k=rw_k_k, rw_k_a=rw_k_a, rw_r_k=rw_r_k, rw_ln_w=rw_ln_w,
        rw_ln_b=rw_ln_b, rw_w_out=rw_w_out, sb_w_qkv=sb_w_qkv, sb_w_out=sb_w_out)
    bsz, seq, d = x_prompt.shape
    n_meta = meta_tokens.shape[0]
    t_valid = n_meta + seq
    t_pad = -(-t_valid // PROMPT_ROW_ALIGN) * PROMPT_ROW_ALIGN
    meta = jnp.broadcast_to(meta_tokens[None], (bsz, n_meta, d))
    x0 = jnp.concatenate([meta, x_prompt, jnp.zeros((bsz, t_pad - t_valid, d), x_prompt.dtype)], axis=1)
    fresh = dict(
        ssm=jnp.zeros((bsz,) + state_ssm.shape[1:], F32), ssm_conv=jnp.zeros((bsz,) + state_ssm_conv.shape[1:], F32),
        gdn=jnp.zeros((bsz,) + state_gdn.shape[1:], F32), gdn_conv=jnp.zeros((bsz,) + state_gdn_conv.shape[1:], F32),
        rwkv=jnp.zeros((bsz,) + state_rwkv.shape[1:], F32),
        rwkv_shift=jnp.zeros((bsz,) + state_rwkv_shift.shape[1:], F32), sb_k=None, sb_v=None)
    y_full, sp = _run_trunk(x0, t_valid, fresh, p)
    y_prompt = y_full[:, n_meta:t_valid]
    past = dict(ssm=state_ssm, ssm_conv=state_ssm_conv, gdn=state_gdn, gdn_conv=state_gdn_conv,
                rwkv=state_rwkv, rwkv_shift=state_rwkv_shift, sb_k=cache_sb_k, sb_v=cache_sb_v)
    y_sample, ss = _run_trunk(x_sample, x_sample.shape[1], past, p)
    return (y_prompt, y_sample,
            sp['ssm'], sp['ssm_conv'], sp['gdn'], sp['gdn_conv'], sp['rwkv'], sp['rwkv_shift'],
            sp['sb_k'], sp['sb_v'],
            ss['ssm'], ss['ssm_conv'], ss['gdn'], ss['gdn_conv'], ss['rwkv'], ss['rwkv_shift'],
            ss['sb_k'], ss['sb_v'])
```

```python
import functools
import math

import jax
import jax.numpy as jnp
from jax import lax
from jax.experimental import pallas as pl
from jax.experimental.pallas import tpu as pltpu

F32 = jnp.float32
BF16 = jnp.bfloat16
EPS = 1e-6
VMEM_LIMIT = 48 * 1024 * 1024
HI = lax.Precision.HIGHEST

PROMPT_ROW_ALIGN = 256
MB_HEAD_DIM = 64
MB_D_STATE = 128
MB_GROUPS = 4
GDN_DK = 128
GDN_DV = 128
RW_HEAD_DIM = 64
RW_GN_EPS = 64e-5
RW_DECAY_SCALE = 0.6065306597126334
SB_HEAD_DIM = 64
SB_LOG_CUT = -100.0


def _cparams(*sem):
    return pltpu.CompilerParams(dimension_semantics=sem, vmem_limit_bytes=VMEM_LIMIT)


def _dot(a, b):
    return jnp.dot(a.astype(BF16), b.astype(BF16), preferred_element_type=F32)


def _dot_nt(a, b):
    return lax.dot_general(a.astype(BF16), b.astype(BF16), (((1,), (1,)), ((), ())),
                           preferred_element_type=F32)


def _dot_tn(a, b):
    return lax.dot_general(a.astype(BF16), b.astype(BF16), (((0,), (0,)), ((), ())),
                           preferred_element_type=F32)


def _dot_hi(a, b):
    return jnp.dot(a, b, preferred_element_type=F32, precision=HI)


def _silu(x):
    return x * jax.nn.sigmoid(x)


def _softplus(x):
    return jnp.maximum(x, 0.0) + jnp.log(1.0 + jnp.exp(-jnp.abs(x)))


def _rmsnorm(x, g):
    return x * lax.rsqrt(jnp.mean(x * x, axis=-1, keepdims=True) + EPS) * g


def _row_tile(n, want):
    return n if n <= want else want


def _norm_matmul_kernel(x_ref, g_ref, w_ref, o_ref):
    h = _rmsnorm(x_ref[...], g_ref[...]).astype(BF16)
    o_ref[...] = jnp.dot(h, w_ref[...], preferred_element_type=F32)


def norm_matmul(x, g, w, *, tm=256):
    n, d = x.shape
    m = w.shape[1]
    tm = _row_tile(n, tm)
    return pl.pallas_call(
        _norm_matmul_kernel,
        out_shape=jax.ShapeDtypeStruct((n, m), F32),
        grid=(pl.cdiv(n, tm),),
        in_specs=[pl.BlockSpec((tm, d), lambda i: (i, 0)),
                  pl.BlockSpec((1, d), lambda i: (0, 0)),
                  pl.BlockSpec((d, m), lambda i: (0, 0))],
        out_specs=pl.BlockSpec((tm, m), lambda i: (i, 0)),
        compiler_params=_cparams("parallel"),
        name="norm_matmul",
    )(x, g.reshape(1, d), w)


def _matmul_res_kernel(a_ref, w_ref, r_ref, o_ref):
    o_ref[...] = r_ref[...] + jnp.dot(a_ref[...].astype(BF16), w_ref[...], preferred_element_type=F32)


def matmul_res(a, w, res, *, tm=512):
    n, k = a.shape
    d = w.shape[1]
    tm = _row_tile(n, tm)
    return pl.pallas_call(
        _matmul_res_kernel,
        out_shape=jax.ShapeDtypeStruct((n, d), F32),
        grid=(pl.cdiv(n, tm),),
        in_specs=[pl.BlockSpec((tm, k), lambda i: (i, 0)),
                  pl.BlockSpec((k, d), lambda i: (0, 0)),
                  pl.BlockSpec((tm, d), lambda i: (i, 0))],
        out_specs=pl.BlockSpec((tm, d), lambda i: (i, 0)),
        compiler_params=_cparams("parallel"),
        name="matmul_res",
    )(a, w, res)


def _ffn_kernel(x_ref, g_ref, wg_ref, wu_ref, wd_ref, o_ref, h_ref, acc_ref):
    f = pl.program_id(1)

    @pl.when(f == 0)
    def _():
        h_ref[...] = _rmsnorm(x_ref[...], g_ref[...]).astype(BF16)
        acc_ref[...] = jnp.zeros_like(acc_ref)

    h = h_ref[...]
    gate = jnp.dot(h, wg_ref[...], preferred_element_type=F32)
    up = jnp.dot(h, wu_ref[...], preferred_element_type=F32)
    act = (_silu(gate) * up).astype(BF16)
    acc_ref[...] += jnp.dot(act, wd_ref[...], preferred_element_type=F32)

    @pl.when(f == pl.num_programs(1) - 1)
    def _():
        o_ref[...] = x_ref[...] + acc_ref[...]


def ffn(x, g, w_gu, w_down, *, tm=512, tf=1408):
    n, d = x.shape
    f = w_down.shape[0]
    tm = _row_tile(n, tm)
    if f % tf:
        tf = f
    nf = f // tf
    return pl.pallas_call(
        _ffn_kernel,
        out_shape=jax.ShapeDtypeStruct((n, d), F32),
        grid=(pl.cdiv(n, tm), nf),
        in_specs=[pl.BlockSpec((tm, d), lambda i, j: (i, 0)),
                  pl.BlockSpec((1, d), lambda i, j: (0, 0)),
                  pl.BlockSpec((d, tf), lambda i, j: (0, j)),
                  pl.BlockSpec((d, tf), lambda i, j: (0, j + nf)),
                  pl.BlockSpec((tf, d), lambda i, j: (j, 0))],
        out_specs=pl.BlockSpec((tm, d), lambda i, j: (i, 0)),
        scratch_shapes=[pltpu.VMEM((tm, d), BF16), pltpu.VMEM((tm, d), F32)],
        compiler_params=_cparams("parallel", "arbitrary"),
        name="ffn",
    )(x, g.reshape(1, d), w_gu, w_gu, w_down)


def _rmsnorm_kernel(x_ref, g_ref, o_ref):
    o_ref[...] = _rmsnorm(x_ref[...], g_ref[...])


def rmsnorm_rows(x, g, *, tm=1024):
    n, d = x.shape
    tm = _row_tile(n, tm)
    return pl.pallas_call(
        _rmsnorm_kernel,
        out_shape=jax.ShapeDtypeStruct((n, d), F32),
        grid=(pl.cdiv(n, tm),),
        in_specs=[pl.BlockSpec((tm, d), lambda i: (i, 0)),
                  pl.BlockSpec((1, d), lambda i: (0, 0))],
        out_specs=pl.BlockSpec((tm, d), lambda i: (i, 0)),
        compiler_params=_cparams("parallel"),
        name="rmsnorm",
    )(x, g.reshape(1, d))


def _causal_conv_silu(raw, tail_ref, buf_ref, w_ref, bias, q):
    del tail_ref
    buf_ref[pl.ds(8, q), :] = raw
    out = buf_ref[pl.ds(5, q), :] * w_ref[0:1, :]
    out = out + buf_ref[pl.ds(6, q), :] * w_ref[1:2, :]
    out = out + buf_ref[pl.ds(7, q), :] * w_ref[2:3, :]
    out = out + raw * w_ref[3:4, :]
    if bias is not None:
        out = out + bias
    buf_ref[pl.ds(5, 3), :] = buf_ref[pl.ds(q + 5, 3), :]
    return _silu(out)


def _tri(q, strict=False):
    r = lax.broadcasted_iota(jnp.int32, (q, q), 0)
    c = lax.broadcasted_iota(jnp.int32, (q, q), 1)
    return (r > c) if strict else (r >= c)


def _mamba_kernel(z_ref, xs_ref, bc_ref, dt_ref, cx_ref, cbc_ref, h0_ref,
                  cwx_ref, cwbc_ref, cbx_ref, cbbc_ref, dtb_ref, a_ref, dskip_ref, nw_ref,
                  y_ref, h_ref, bufx_ref, bufbc_ref, *, q, t_len, n_heads):
    c = pl.program_id(1)
    n_state = MB_D_STATE
    hpg = n_heads // MB_GROUPS
    gw = hpg * MB_HEAD_DIM

    @pl.when(c == 0)
    def _():
        bufx_ref[pl.ds(5, 3), :] = cx_ref[0]
        bufbc_ref[pl.ds(5, 3), :] = cbc_ref[0]
        h_ref[0] = h0_ref[0]

    rows = c * q + lax.broadcasted_iota(jnp.int32, (q, 1), 0)
    valid = rows < t_len
    xs = _causal_conv_silu(jnp.where(valid, xs_ref[0], 0.0), None, bufx_ref, cwx_ref, cbx_ref[...], q)
    bc = _causal_conv_silu(jnp.where(valid, bc_ref[0], 0.0), None, bufbc_ref, cwbc_ref, cbbc_ref[...], q)

    lane = lax.broadcasted_iota(jnp.int32, (q, 128), 1)
    dt = _softplus(jnp.where(valid & (lane < n_heads), dt_ref[0], 0.0) + dtb_ref[...])
    dt = jnp.where(valid & (lane < n_heads), dt, 0.0)
    da = dt * a_ref[...]
    incl = _tri(q)
    acum = _dot_hi(incl.astype(F32), da)
    acum_t = acum.T
    dt_t = dt.T
    last = acum[q - 1:q, :]
    wts = jnp.exp(last - acum) * dt
    e_acum = jnp.exp(acum)
    e_last = jnp.exp(last)

    lo_half = lax.broadcasted_iota(jnp.int32, (q, 128), 1) < MB_HEAD_DIM
    lo_rows = lax.broadcasted_iota(jnp.int32, (128, 128), 0) < MB_HEAD_DIM

    z = z_ref[0]
    for g in range(MB_GROUPS):
        bm = bc[:, g * n_state:(g + 1) * n_state]
        cm = bc[:, (MB_GROUPS + g) * n_state:(MB_GROUPS + g + 1) * n_state]
        cb = _dot_nt(cm, bm)
        hg = h_ref[0, g * gw:(g + 1) * gw, :]
        ch = _dot_nt(cm, hg)
        ys = []
        for p in range(hpg // 2):
            h_e = g * hpg + 2 * p
            x_pair = xs[:, h_e * MB_HEAD_DIM:(h_e + 2) * MB_HEAD_DIM]
            outs = []
            for hh in (h_e, h_e + 1):
                seg = acum[:, hh:hh + 1] - acum_t[hh:hh + 1, :]
                decay = jnp.where(incl, jnp.exp(jnp.where(incl, seg, 0.0)), 0.0)
                m = cb * decay * dt_t[hh:hh + 1, :]
                outs.append(_dot(m, x_pair))
            y_pair = jnp.where(lo_half, outs[0], outs[1])
            e_pair = jnp.where(lo_half, e_acum[:, h_e:h_e + 1], e_acum[:, h_e + 1:h_e + 2])
            y_pair = y_pair + e_pair * ch[:, 2 * p * MB_HEAD_DIM:(2 * p + 2) * MB_HEAD_DIM]
            ys.append(y_pair)
            w_pair = jnp.where(lo_half, wts[:, h_e:h_e + 1], wts[:, h_e + 1:h_e + 2])
            upd = _dot_tn(x_pair * w_pair, bm)
            scale = jnp.where(lo_rows, e_last[:, h_e:h_e + 1], e_last[:, h_e + 1:h_e + 2])
            r0 = h_e * MB_HEAD_DIM
            h_ref[0, r0:r0 + 128, :] = h_ref[0, r0:r0 + 128, :] * scale + upd
        yg = jnp.concatenate(ys, axis=1)
        xg = xs[:, g * gw:(g + 1) * gw]
        yg = yg + xg * dskip_ref[:, g * gw:(g + 1) * gw]
        yg = yg * _silu(z[:, g * gw:(g + 1) * gw])
        yg = yg * lax.rsqrt(jnp.mean(yg * yg, axis=-1, keepdims=True) + EPS)
        y_ref[0, :, g * gw:(g + 1) * gw] = (yg * nw_ref[:, g * gw:(g + 1) * gw]).astype(y_ref.dtype)


def mamba_core(proj, t_valid, conv_buf, h0, conv_w, conv_b, dt_bias, a_log, d_skip, norm_w, *, chunk=128):
    bsz, t_len, _ = proj.shape
    n_heads = dt_bias.shape[0]
    d_inner = n_heads * MB_HEAD_DIM
    gn = MB_GROUPS * MB_D_STATE
    assert d_inner % 1024 == 0 and 2 * gn == 1024 and n_heads <= 128
    q = min(chunk, t_len)
    assert q % 8 == 0 and t_len % q == 0
    n_chunks = t_len // q
    dtb = jnp.zeros((1, 128), F32).at[0, :n_heads].set(dt_bias)
    a_row = jnp.zeros((1, 128), F32).at[0, :n_heads].set(-jnp.exp(a_log))
    dskip = jnp.repeat(d_skip, MB_HEAD_DIM).reshape(1, d_inner)
    kern = functools.partial(_mamba_kernel, q=q, t_len=t_valid, n_heads=n_heads)
    xblk = d_inner // 1024
    y, h = pl.pallas_call(
        kern,
        out_shape=(jax.ShapeDtypeStruct((bsz, t_len, d_inner), BF16),
                   jax.ShapeDtypeStruct((bsz, n_heads * MB_HEAD_DIM, MB_D_STATE), F32)),
        grid=(bsz, n_chunks),
        in_specs=[
            pl.BlockSpec((1, q, d_inner), lambda b, c: (b, c, 0)),
            pl.BlockSpec((1, q, d_inner), lambda b, c: (b, c, 1)),
            pl.BlockSpec((1, q, 2 * gn), lambda b, c: (b, c, 2 * xblk)),
            pl.BlockSpec((1, q, 128), lambda b, c: (b, c, (2 * d_inner + 2 * gn) // 128)),
            pl.BlockSpec((1, 3, d_inner), lambda b, c: (b, 0, 0)),
            pl.BlockSpec((1, 3, 2 * gn), lambda b, c: (b, 0, xblk)),
            pl.BlockSpec((1, n_heads * MB_HEAD_DIM, MB_D_STATE), lambda b, c: (b, 0, 0)),
            pl.BlockSpec((4, d_inner), lambda b, c: (0, 0)),
            pl.BlockSpec((4, 2 * gn), lambda b, c: (0, xblk)),
            pl.BlockSpec((1, d_inner), lambda b, c: (0, 0)),
            pl.BlockSpec((1, 2 * gn), lambda b, c: (0, xblk)),
            pl.BlockSpec((1, 128), lambda b, c: (0, 0)),
            pl.BlockSpec((1, 128), lambda b, c: (0, 0)),
            pl.BlockSpec((1, d_inner), lambda b, c: (0, 0)),
            pl.BlockSpec((1, d_inner), lambda b, c: (0, 0)),
        ],
        out_specs=(pl.BlockSpec((1, q, d_inner), lambda b, c: (b, c, 0)),
                   pl.BlockSpec((1, n_heads * MB_HEAD_DIM, MB_D_STATE), lambda b, c: (b, 0, 0))),
        scratch_shapes=[pltpu.VMEM((q + 8, d_inner), F32), pltpu.VMEM((q + 8, 2 * gn), F32)],
        compiler_params=_cparams("parallel", "arbitrary"),
        name="mamba_core",
    )(proj, proj, proj, proj, conv_buf, conv_buf, h0.reshape(bsz, n_heads * MB_HEAD_DIM, MB_D_STATE),
      conv_w, conv_w, conv_b.reshape(1, -1), conv_b.reshape(1, -1), dtb, a_row, dskip,
      norm_w.reshape(1, d_inner))
    return y, h.reshape(bsz, n_heads, MB_HEAD_DIM, MB_D_STATE)


def mamba2_layer(x, conv_buf, ssm_state, norm_g, w_in, conv_w, conv_b, dt_bias, a_log, d_skip,
                 norm_w, w_out):
    return mamba2_layer_padded(x, x.shape[1], conv_buf, ssm_state, norm_g, w_in, conv_w, conv_b, dt_bias,
                               a_log, d_skip, norm_w, w_out)


def mamba2_layer_padded(x, t_valid, conv_buf, ssm_state, norm_g, w_in, conv_w, conv_b, dt_bias, a_log,
                        d_skip, norm_w, w_out):
    bsz, t_len, d = x.shape
    n_heads = dt_bias.shape[0]
    d_inner = n_heads * MB_HEAD_DIM
    conv_dim = conv_w.shape[1]
    x2 = x.reshape(bsz * t_len, d)
    proj = norm_matmul(x2, norm_g, w_in).reshape(bsz, t_len, -1)
    y, h_new = mamba_core(proj, t_valid, conv_buf, ssm_state, conv_w, conv_b, dt_bias, a_log, d_skip, norm_w)
    conv_new = proj[:, t_valid - 3:t_valid, d_inner:d_inner + conv_dim]
    out = matmul_res(y.reshape(bsz * t_len, d_inner), w_out, x2)
    return out.reshape(bsz, t_len, d), conv_new, h_new


def _split(a):
    hi = a.astype(BF16)
    lo = (a - hi.astype(F32)).astype(BF16)
    return hi, lo


def _mm(a, b):
    return jnp.dot(a, b, preferred_element_type=F32)


def _dot3(a, b):
    ah, al = _split(a)
    bh, bl = _split(b)
    return _mm(ah, bh) + (_mm(ah, bl) + _mm(al, bh))


INV_BASE = 8


def _unit_lower_inverse(mats):
    n = mats[0].shape[0]
    r = lax.broadcasted_iota(jnp.int32, (n, n), 0)
    c = lax.broadcasted_iota(jnp.int32, (n, n), 1)
    eye = jnp.where(r == c, 1.0, 0.0)
    same = (r // INV_BASE) == (c // INV_BASE)
    diag = [jnp.where(same, a, 0.0) for a in mats]
    xs = [eye - dg for dg in diag]
    ps = diag
    k = 1
    while 2 * k < INV_BASE:
        ps = [_dot(p, p) for p in ps]
        xs = [x + _dot(x, p) for x, p in zip(xs, ps)]
        k *= 2
    w = INV_BASE
    while w < n:
        off = ((r // (2 * w)) == (c // (2 * w))) != ((r // w) == (c // w))
        ts = [_dot(jnp.where(off, a, 0.0), x) for a, x in zip(mats, xs)]
        xs = [x - _dot(x, t) for x, t in zip(xs, ts)]
        w *= 2
    return xs


def _solve_unit_lower(mats, rhs, apply):
    x_inv = _unit_lower_inverse(mats)
    d0 = apply(x_inv, rhs, _dot)
    t0 = apply(mats, d0, _dot3)
    res = [b - d - t for b, d, t in zip(rhs, d0, t0)]
    corr = apply(x_inv, res, _dot)
    return [d + e for d, e in zip(d0, corr)]


def _gdn_kernel(q_ref, k_ref, v_ref, z_ref, ba_ref, cq_ref, ck_ref, cv_ref, s0_ref,
                cwq_ref, cwk_ref, cwv_ref, dtb_ref, al_ref, nw_ref,
                o_ref, s_ref, bufq_ref, bufk_ref, bufv_ref, *, q, t_len, n_hv, n_hk):
    c = pl.program_id(1)
    rep = n_hv // n_hk

    @pl.when(c == 0)
    def _():
        bufq_ref[pl.ds(5, 3), :] = cq_ref[0]
        bufk_ref[pl.ds(5, 3), :] = ck_ref[0]
        bufv_ref[pl.ds(5, 3), :] = cv_ref[0]
        s_ref[0] = s0_ref[0]

    rows = c * q + lax.broadcasted_iota(jnp.int32, (q, 1), 0)
    valid = rows < t_len
    qc = _causal_conv_silu(jnp.where(valid, q_ref[0], 0.0), None, bufq_ref, cwq_ref, None, q)
    kc = _causal_conv_silu(jnp.where(valid, k_ref[0], 0.0), None, bufk_ref, cwk_ref, None, q)
    vc = _causal_conv_silu(jnp.where(valid, v_ref[0], 0.0), None, bufv_ref, cwv_ref, None, q)

    lane = lax.broadcasted_iota(jnp.int32, (q, 128), 1)
    ba = jnp.where(valid & (lane < 2 * n_hv), ba_ref[0], 0.0)
    beta = jnp.where(valid & (lane < n_hv), jax.nn.sigmoid(ba), 0.0)
    g = jnp.where(valid & (lane >= n_hv) & (lane < 2 * n_hv),
                  -jnp.exp(al_ref[...]) * _softplus(ba + dtb_ref[...]), 0.0)
    incl = _tri(q)
    strict = _tri(q, strict=True)
    gcum = _dot_hi(incl.astype(F32), g)
    gcum_t = gcum.T
    eg = jnp.exp(gcum)
    glast = gcum[q - 1:q, :]
    w_last = jnp.exp(glast - gcum)
    e_last = jnp.exp(glast)

    z = z_ref[0]
    qn, kn, kk, qk0 = [], [], [], []
    for kh in range(n_hk):
        qh = qc[:, kh * GDN_DK:(kh + 1) * GDN_DK]
        kh_ = kc[:, kh * GDN_DK:(kh + 1) * GDN_DK]
        qh = qh * (lax.rsqrt(jnp.sum(qh * qh, axis=-1, keepdims=True) + 1e-6) * (GDN_DK ** -0.5))
        kh_ = kh_ * lax.rsqrt(jnp.sum(kh_ * kh_, axis=-1, keepdims=True) + 1e-6)
        qn.append(qh)
        kn.append(kh_)
        kk.append(_dot_nt(kh_, kh_))
        qk0.append(_dot_nt(qh, kh_))
    heads = range(n_hv)
    s_old = [s_ref[0, h * GDN_DK:(h + 1) * GDN_DK, :] for h in heads]
    ks = [_dot(kn[h // rep], s_old[h]) for h in heads]
    qs = [_dot(qn[h // rep], s_old[h]) for h in heads]
    dec, a_mat, rhs = [], [], []
    for h in heads:
        hl = n_hv + h
        seg = gcum[:, hl:hl + 1] - gcum_t[hl:hl + 1, :]
        dec.append(jnp.where(incl, jnp.exp(jnp.where(incl, seg, 0.0)), 0.0))
        bt = beta[:, h:h + 1]
        a_mat.append(jnp.where(strict, kk[h // rep] * dec[h], 0.0) * bt)
        rhs.append(bt * (vc[:, h * GDN_DV:(h + 1) * GDN_DV] - eg[:, hl:hl + 1] * ks[h]))
    delta = _solve_unit_lower(a_mat, rhs, lambda ms, vs, dot: [dot(m, v) for m, v in zip(ms, vs)])
    o_mm = [_dot(qk0[h // rep] * dec[h], delta[h]) for h in heads]
    upd = [_dot_tn(kn[h // rep] * w_last[:, n_hv + h:n_hv + h + 1], delta[h]) for h in heads]
    for h in heads:
        hl = n_hv + h
        s_ref[0, h * GDN_DK:(h + 1) * GDN_DK, :] = s_old[h] * e_last[:, hl:hl + 1] + upd[h]
        o = eg[:, hl:hl + 1] * qs[h] + o_mm[h]
        o = o * lax.rsqrt(jnp.mean(o * o, axis=-1, keepdims=True) + EPS) * nw_ref[...]
        o = o * _silu(z[:, h * GDN_DV:(h + 1) * GDN_DV])
        o_ref[0, :, h * GDN_DV:(h + 1) * GDN_DV] = o.astype(o_ref.dtype)


def gdn_core(proj, t_valid, conv_buf, s0, conv_w, dt_bias, a_log, norm_w, *, chunk=64):
    bsz, t_len, _ = proj.shape
    n_hv = dt_bias.shape[0]
    val_dim = n_hv * GDN_DV
    key_dim = (conv_w.shape[1] - val_dim) // 2
    n_hk = key_dim // GDN_DK
    assert val_dim == 2 * key_dim and key_dim % 128 == 0 and 2 * n_hv <= 128
    q = min(chunk, t_len)
    assert q % 8 == 0 and t_len % q == 0
    n_chunks = t_len // q
    pad = jnp.zeros((n_hv,), F32)
    dtb = jnp.zeros((1, 128), F32).at[0, n_hv:2 * n_hv].set(dt_bias)
    alog = jnp.zeros((1, 128), F32).at[0, n_hv:2 * n_hv].set(a_log)
    del pad
    kern = functools.partial(_gdn_kernel, q=q, t_len=t_valid, n_hv=n_hv, n_hk=n_hk)
    ba_blk = (2 * key_dim + 2 * val_dim) // 128
    s0 = s0.reshape(bsz, n_hv * GDN_DK, GDN_DV)
    o, s = pl.pallas_call(
        kern,
        out_shape=(jax.ShapeDtypeStruct((bsz, t_len, val_dim), BF16),
                   jax.ShapeDtypeStruct(s0.shape, F32)),
        grid=(bsz, n_chunks),
        in_specs=[
            pl.BlockSpec((1, q, key_dim), lambda b, c: (b, c, 0)),
            pl.BlockSpec((1, q, key_dim), lambda b, c: (b, c, 1)),
            pl.BlockSpec((1, q, val_dim), lambda b, c: (b, c, 1)),
            pl.BlockSpec((1, q, val_dim), lambda b, c: (b, c, 2)),
            pl.BlockSpec((1, q, 128), lambda b, c: (b, c, ba_blk)),
            pl.BlockSpec((1, 3, key_dim), lambda b, c: (b, 0, 0)),
            pl.BlockSpec((1, 3, key_dim), lambda b, c: (b, 0, 1)),
            pl.BlockSpec((1, 3, val_dim), lambda b, c: (b, 0, 1)),
            pl.BlockSpec((1,) + s0.shape[1:], lambda b, c: (b, 0, 0)),
            pl.BlockSpec((4, key_dim), lambda b, c: (0, 0)),
            pl.BlockSpec((4, key_dim), lambda b, c: (0, 1)),
            pl.BlockSpec((4, val_dim), lambda b, c: (0, 1)),
            pl.BlockSpec((1, 128), lambda b, c: (0, 0)),
            pl.BlockSpec((1, 128), lambda b, c: (0, 0)),
            pl.BlockSpec((1, GDN_DV), lambda b, c: (0, 0)),
        ],
        out_specs=(pl.BlockSpec((1, q, val_dim), lambda b, c: (b, c, 0)),
                   pl.BlockSpec((1,) + s0.shape[1:], lambda b, c: (b, 0, 0))),
        scratch_shapes=[pltpu.VMEM((q + 8, key_dim), F32), pltpu.VMEM((q + 8, key_dim), F32),
                        pltpu.VMEM((q + 8, val_dim), F32)],
        compiler_params=_cparams("parallel", "arbitrary"),
        name="gdn_core",
    )(proj, proj, proj, proj, proj, conv_buf, conv_buf, conv_buf, s0,
      conv_w, conv_w, conv_w, dtb, alog, norm_w.reshape(1, GDN_DV))
    return o, s.reshape(bsz, n_hv, GDN_DK, GDN_DV)


def gdn_layer(x, conv_buf, s0, norm_g, w_in, conv_w, dt_bias, a_log, norm_w, w_out):
    return gdn_layer_padded(x, x.shape[1], conv_buf, s0, norm_g, w_in, conv_w, dt_bias, a_log, norm_w, w_out)


def gdn_layer_padded(x, t_valid, conv_buf, s0, norm_g, w_in, conv_w, dt_bias, a_log, norm_w, w_out):
    bsz, t_len, d = x.shape
    conv_dim = conv_w.shape[1]
    x2 = x.reshape(bsz * t_len, d)
    proj = norm_matmul(x2, norm_g, w_in).reshape(bsz, t_len, -1)
    o, s_new = gdn_core(proj, t_valid, conv_buf, s0, conv_w, dt_bias, a_log, norm_w)
    conv_new = proj[:, t_valid - 3:t_valid, :conv_dim]
    out = matmul_res(o.reshape(bsz * t_len, -1), w_out, x2)
    return out.reshape(bsz, t_len, d), conv_new, s_new


def _head_ones(width, head):
    r = lax.broadcasted_iota(jnp.int32, (width, width), 0) // head
    c = lax.broadcasted_iota(jnp.int32, (width, width), 1) // head
    return jnp.where(r == c, 1.0, 0.0).astype(BF16)


def _head_sum(x, ones):
    hi, lo = _split(x)
    return _mm(hi, ones) + _mm(lo, ones)


def _rw_proj_kernel(hn_ref, pv_ref, mu_ref, wr_ref, wk_ref, wv_ref, w1_ref, w2_ref, a1_ref, a2_ref,
                    g1_ref, g2_ref, w0_ref, a0_ref, kk_ref, ka_ref,
                    r_ref, lw_ref, k_ref, v_ref, kn_ref, b_ref, g_ref):
    hn = hn_ref[...]
    dlt = pv_ref[...] - hn

    def mix(c):
        return (hn + dlt * mu_ref[c:c + 1, :]).astype(BF16)

    r_ref[...] = _mm(mix(0), wr_ref[...])
    k = _mm(mix(1), wk_ref[...])
    v_ref[...] = _mm(mix(2), wv_ref[...])
    dec = w0_ref[...] + _mm(jnp.tanh(_mm(mix(3), w1_ref[...])).astype(BF16), w2_ref[...])
    lw_ref[...] = -RW_DECAY_SCALE * jax.nn.sigmoid(dec)
    a = jax.nn.sigmoid(a0_ref[...] + _mm(_mm(mix(4), a1_ref[...]).astype(BF16), a2_ref[...]))
    g_ref[...] = _mm(jax.nn.sigmoid(_mm(mix(5), g1_ref[...])).astype(BF16), g2_ref[...])
    ones = _head_ones(128, RW_HEAD_DIM)
    kn = k * kk_ref[...]
    d = kn.shape[1]
    for j in range(d // 128):
        sl = slice(j * 128, (j + 1) * 128)
        knj = kn[:, sl]
        knj = knj * lax.rsqrt(_head_sum(knj * knj, ones) + 1e-6)
        kn_ref[:, sl] = knj
        b_ref[:, sl] = knj * a[:, sl]
    k_ref[...] = k * (1.0 + (a - 1.0) * ka_ref[...])


def rw_proj(hn, prev, p, *, tm=256):
    n, d = hn.shape
    tm = _row_tile(n, tm)
    row = lambda i: (i, 0)
    fix = lambda i: (0, 0)
    big = pl.BlockSpec((tm, d), row)
    vec = pl.BlockSpec((1, d), fix)

    def full(a):
        return pl.BlockSpec(a.shape, fix)

    ws = [p['rw_w_rkv'][0].astype(BF16), p['rw_w_rkv'][1].astype(BF16), p['rw_w_rkv'][2].astype(BF16),
          p['rw_w1'].astype(BF16), p['rw_w2'].astype(BF16), p['rw_a1'].astype(BF16), p['rw_a2'].astype(BF16),
          p['rw_g1'].astype(BF16), p['rw_g2'].astype(BF16)]
    vecs = [p['rw_w0'].reshape(1, d), p['rw_a0'].reshape(1, d), p['rw_k_k'].reshape(1, d),
            p['rw_k_a'].reshape(1, d)]
    return pl.pallas_call(
        _rw_proj_kernel,
        out_shape=tuple(jax.ShapeDtypeStruct((n, d), F32) for _ in range(7)),
        grid=(pl.cdiv(n, tm),),
        in_specs=[big, big, full(p['rw_mu'])] + [full(w) for w in ws] + [vec] * 4,
        out_specs=tuple(big for _ in range(7)),
        compiler_params=_cparams("parallel"),
        name="rw_proj",
    )(hn, prev, p['rw_mu'], *ws, *vecs)


def _rw_scan_kernel(r_ref, lw_ref, k_ref, v_ref, kn_ref, b_ref, g_ref, s0_ref, rk_ref, lnw_ref, lnb_ref,
                    o_ref, s_ref, *, q, t_len):
    c = pl.program_id(1)

    @pl.when(c == 0)
    def _():
        s_ref[0] = s0_ref[0]

    rows = c * q + lax.broadcasted_iota(jnp.int32, (q, 1), 0)
    valid = rows < t_len
    incl = _tri(q)
    strict = _tri(q, strict=True)
    lw_all = jnp.where(valid, lw_ref[0], 0.0)
    cum_all = _dot_hi(incl.astype(F32), lw_all)
    ones = _head_ones(128, RW_HEAD_DIM)
    lo = lax.broadcasted_iota(jnp.int32, (q, 128), 1) < RW_HEAD_DIM
    blockdiag = ((lax.broadcasted_iota(jnp.int32, (128, 128), 0) < RW_HEAD_DIM)
                 == (lax.broadcasted_iota(jnp.int32, (128, 128), 1) < RW_HEAD_DIM))
    d = lw_all.shape[1]
    pairs = range(d // 128)
    sls = [slice(p * 128, (p + 1) * 128) for p in pairs]
    rs, ks, vs, bends, kends, s_old, cum_ends = [], [], [], [], [], [], []
    pk, pb, u0, y0 = [], [], [], []
    for p in pairs:
        sl = sls[p]
        lw = lw_all[:, sl]
        cum = cum_all[:, sl]
        cum_end = cum[q - 1:q, :]
        r = jnp.where(valid, r_ref[0, :, sl], 0.0)
        k = jnp.where(valid, k_ref[0, :, sl], 0.0)
        v = jnp.where(valid, v_ref[0, :, sl], 0.0)
        kn = jnp.where(valid, kn_ref[0, :, sl], 0.0)
        b = jnp.where(valid, b_ref[0, :, sl], 0.0)
        kq = kn * jnp.exp(cum - lw)
        rq = r * jnp.exp(cum)
        einv = jnp.exp(-cum)
        kd = k * einv
        bd = b * einv
        eend = jnp.exp(cum_end - cum)
        s2 = s_ref[0, sl, :]
        lhs = jnp.concatenate([jnp.where(lo, kq, 0.0), jnp.where(lo, 0.0, kq),
                               jnp.where(lo, rq, 0.0), jnp.where(lo, 0.0, rq)], axis=0)
        pk.append(_dot_nt(lhs, kd))
        pb.append(_dot_nt(lhs, bd))
        u0.append(_dot_nt(kq, s2))
        y0.append(_dot_nt(rq, s2))
        rs.append(r)
        ks.append(k)
        vs.append(v)
        kends.append(k * eend)
        bends.append(b * eend)
        s_old.append(s2)
        cum_ends.append(cum_end)
    ab = [jnp.where(strict, pb[p][e * q:(e + 1) * q], 0.0) for p in pairs for e in range(2)]
    rhs = [u0[p] + jnp.where(lo, _dot(jnp.where(strict, pk[p][0:q], 0.0), vs[p]),
                             _dot(jnp.where(strict, pk[p][q:2 * q], 0.0), vs[p])) for p in pairs]

    def per_head(ms, vecs, dot):
        return [jnp.where(lo, dot(ms[2 * p], vecs[p]), dot(ms[2 * p + 1], vecs[p])) for p in pairs]

    us = _solve_unit_lower(ab, rhs, per_head)
    ys = []
    for p in pairs:
        rk = [jnp.where(incl, pk[p][(2 + e) * q:(3 + e) * q], 0.0) for e in range(2)]
        rb = [jnp.where(incl, pb[p][(2 + e) * q:(3 + e) * q], 0.0) for e in range(2)]
        ys.append(y0[p] + jnp.where(lo, _dot(rk[0], vs[p]) - _dot(rb[0], us[p]),
                                    _dot(rk[1], vs[p]) - _dot(rb[1], us[p])))
    s_new = [s_old[p] * jnp.exp(cum_ends[p]) + _dot_tn(vs[p], kends[p]) - _dot_tn(us[p], bends[p])
             for p in pairs]
    for p in pairs:
        sl = sls[p]
        s_ref[0, sl, :] = jnp.where(blockdiag, s_new[p], 0.0)
        y = ys[p]
        mean = _head_sum(y, ones) * (1.0 / RW_HEAD_DIM)
        yc = y - mean
        var = _head_sum(yc * yc, ones) * (1.0 / RW_HEAD_DIM)
        yn = yc * lax.rsqrt(var + RW_GN_EPS) * lnw_ref[:, sl] + lnb_ref[:, sl]
        bonus = _head_sum(rs[p] * ks[p] * rk_ref[:, sl], ones) * vs[p]
        o_ref[0, :, sl] = ((yn + bonus) * g_ref[0, :, sl]).astype(o_ref.dtype)


def rw_scan(r, lw, k, v, kn, b, g, t_valid, s0, r_k, ln_w, ln_b, *, chunk=64):
    bsz, t_len, d = r.shape
    q = min(chunk, t_len)
    assert q % 8 == 0 and t_len % q == 0 and d % 128 == 0
    n_heads = d // RW_HEAD_DIM
    s5 = s0.reshape(bsz, n_heads // 2, 2, RW_HEAD_DIM, RW_HEAD_DIM)
    s2 = jnp.einsum('bpevk,ef->bpevfk', s5, jnp.eye(2, dtype=F32)).reshape(bsz, d, 128)
    blk = pl.BlockSpec((1, q, d), lambda bb, c: (bb, c, 0))
    vec = pl.BlockSpec((1, d), lambda bb, c: (0, 0))
    st = pl.BlockSpec((1, d, 128), lambda bb, c: (bb, 0, 0))
    o, s_new = pl.pallas_call(
        functools.partial(_rw_scan_kernel, q=q, t_len=t_valid),
        out_shape=(jax.ShapeDtypeStruct((bsz, t_len, d), BF16), jax.ShapeDtypeStruct((bsz, d, 128), F32)),
        grid=(bsz, t_len // q),
        in_specs=[blk] * 7 + [st, vec, vec, vec],
        out_specs=(blk, st),
        compiler_params=_cparams("parallel", "arbitrary"),
        name="rw_scan",
    )(r, lw, k, v, kn, b, g, s2, r_k.reshape(1, d), ln_w.reshape(1, d), ln_b.reshape(1, d))
    s6 = s_new.reshape(bsz, n_heads // 2, 2, RW_HEAD_DIM, 2, RW_HEAD_DIM)
    s_out = jnp.stack([s6[:, :, 0, :, 0, :], s6[:, :, 1, :, 1, :]], axis=2)
    return o, s_out.reshape(bsz, n_heads, RW_HEAD_DIM, RW_HEAD_DIM)


def rwkv_layer_padded(x, t_valid, shift_buf, s0, norm_g, p):
    bsz, t_len, d = x.shape
    x2 = x.reshape(bsz * t_len, d)
    hn = rmsnorm_rows(x2, norm_g).reshape(bsz, t_len, d)
    prev = jnp.concatenate([shift_buf, hn[:, :-1]], axis=1)
    outs = rw_proj(hn.reshape(-1, d), prev.reshape(-1, d), p)
    r, lw, k, v, kn, b, g = (a.reshape(bsz, t_len, d) for a in outs)
    yg, s_new = rw_scan(r, lw, k, v, kn, b, g, t_valid, s0, p['rw_r_k'], p['rw_ln_w'], p['rw_ln_b'])
    out = matmul_res(yg.reshape(bsz * t_len, d), p['rw_w_out'].astype(BF16), x2)
    return out.reshape(bsz, t_len, d), hn[:, t_valid - 1:t_valid], s_new


def _dot3_nt(a, b):
    ah, al = _split(a)
    bh, bl = _split(b)
    dn = (((1,), (1,)), ((), ()))
    f = lambda u, w: lax.dot_general(u, w, dn, preferred_element_type=F32)
    return f(ah, bh) + (f(ah, bl) + f(al, bh))


def _suffix_ones(n):
    r = lax.broadcasted_iota(jnp.int32, (n, n), 0)
    c = lax.broadcasted_iota(jnp.int32, (n, n), 1)
    return jnp.where(r >= c, 1.0, 0.0).astype(BF16)


def _sb_block(q_e, kblk, vblk, carry, acc, lo, suffix, mask):
    outs = []
    new_carry = []
    for e in range(2):
        z = _dot3_nt(q_e[e], kblk)
        lnb = -_softplus(z)
        if mask is not None:
            lnb = jnp.where(mask, lnb, 0.0)
        hi, lw = _split(lnb)
        rsum = _mm(hi, suffix) + _mm(lw, suffix) + carry[e]
        att = jnp.exp(z + rsum)
        if mask is not None:
            att = jnp.where(mask, att, 0.0)
        outs.append(_dot(att, vblk))
        new_carry.append(rsum[:, 0:1])
    return new_carry, acc + jnp.where(lo, outs[0], outs[1])


def _sb_kernel(q_ref, kn_ref, vn_ref, o_ref, kmax_ref, *, bq, scale):
    i = pl.program_id(2)
    t_new = kn_ref.shape[1]

    @pl.when(i == 0)
    def _():
        def body(j, m):
            rows = kn_ref[0, pl.ds(pl.multiple_of(j * bq, bq), bq), :]
            return jnp.maximum(m, jnp.max(jnp.abs(rows).reshape(bq // 8, 8, 128), axis=0))

        m8 = lax.fori_loop(0, t_new // bq, body, jnp.zeros((8, 128), F32))
        kmax_ref[...] = jnp.max(m8, axis=0, keepdims=True)

    lo = lax.broadcasted_iota(jnp.int32, (bq, 128), 1) < SB_HEAD_DIM
    q = q_ref[0] * scale
    q_e = [jnp.where(lo, q, 0.0), jnp.where(lo, 0.0, q)]
    zb = _head_sum(jnp.abs(q) * kmax_ref[...], _head_ones(128, SB_HEAD_DIM))
    zb0 = zb[:, 0:1]
    zb1 = zb[:, SB_HEAD_DIM:SB_HEAD_DIM + 1]
    suffix = _suffix_ones(bq)
    mask = _tri(bq, strict=True)
    zero_c = jnp.zeros((bq, 1), F32)
    start = pl.multiple_of(i * bq, bq)
    carry, acc = _sb_block(q_e, kn_ref[0, pl.ds(start, bq), :], vn_ref[0, pl.ds(start, bq), :],
                           [zero_c, zero_c], jnp.zeros((bq, 128), F32), lo, suffix, mask)

    def live(c0, c1):
        return (jnp.max(jnp.maximum(c0 + zb0, c1 + zb1)) > SB_LOG_CUT).astype(jnp.int32)

    def walk(k_ref, v_ref, n_blocks, blk, sfx, st):
        def cond(s):
            return (s[0] < n_blocks) & (s[4] > 0)

        def body(s):
            jj, c0, c1, a, _ = s
            off = pl.multiple_of((n_blocks - 1 - jj) * blk, blk)
            (c0, c1), a = _sb_block(q_e, k_ref[0, pl.ds(off, blk), :], v_ref[0, pl.ds(off, blk), :],
                                    [c0, c1], a, lo, sfx, None)
            return jj + 1, c0, c1, a, live(c0, c1)

        return lax.while_loop(cond, body, st)

    st = walk(kn_ref, vn_ref, i, bq, suffix, (jnp.int32(0), carry[0], carry[1], acc, live(*carry)))
    o_ref[0] = st[3].astype(o_ref.dtype)


def sb_attention(qkv, *, block=256):
    bsz, t_len, d3 = qkv.shape
    d = d3 // 3
    assert d % 128 == 0
    n_pairs = d // 128
    bq = min(block, t_len)
    assert t_len % bq == 0 and bq % 8 == 0
    return pl.pallas_call(
        functools.partial(_sb_kernel, bq=bq, scale=SB_HEAD_DIM ** -0.5),
        out_shape=jax.ShapeDtypeStruct((bsz, t_len, d), BF16),
        grid=(bsz, n_pairs, t_len // bq),
        in_specs=[pl.BlockSpec((1, bq, 128), lambda b, p, i: (b, i, p)),
                  pl.BlockSpec((1, t_len, 128), lambda b, p, i: (b, 0, n_pairs + p)),
                  pl.BlockSpec((1, t_len, 128), lambda b, p, i: (b, 0, 2 * n_pairs + p))],
        out_specs=pl.BlockSpec((1, bq, 128), lambda b, p, i: (b, i, p)),
        compiler_params=_cparams("parallel", "parallel", "arbitrary"),
        scratch_shapes=[pltpu.VMEM((1, 128), F32)],
        name="sb_attention",
    )(qkv, qkv, qkv)


def _sb_decode_kernel(qkv_ref, kc_ref, vc_ref, o_ref, q_s, acc_s, carry_s, *, t, n_heads, bp, scale):
    j = pl.program_id(1)
    hd = SB_HEAD_DIM
    d = n_heads * hd
    rows = n_heads * t
    heads = range(n_heads)

    def scores(keys):
        return jnp.concatenate([_dot3_nt(q_s[h * t:(h + 1) * t, :], keys(h)) for h in heads], axis=0)

    def weights(z, carry, suffix, mask):
        lnb = -_softplus(z)
        if mask is not None:
            lnb = jnp.where(mask, lnb, 0.0)
        hi, lw = _split(lnb)
        rsum = _mm(hi, suffix) + _mm(lw, suffix) + carry
        att = jnp.exp(z + rsum)
        if mask is not None:
            att = jnp.where(mask, att, 0.0)
        return att, rsum[:, 0:1]

    @pl.when(j == 0)
    def _():
        for h in heads:
            q_s[h * t:(h + 1) * t, :] = qkv_ref[0, :, h * hd:(h + 1) * hd] * scale
        z = scores(lambda h: qkv_ref[0, :, d + h * hd:d + (h + 1) * hd])
        qi = lax.rem(lax.broadcasted_iota(jnp.int32, (rows, t), 0), t)
        mask = lax.broadcasted_iota(jnp.int32, (rows, t), 1) < qi
        att, carry = weights(z, 0.0, _suffix_ones(t), mask)
        for h in heads:
            acc_s[h * t:(h + 1) * t, :] = _dot(att[h * t:(h + 1) * t],
                                               qkv_ref[0, :, 2 * d + h * hd:2 * d + (h + 1) * hd])
        carry_s[...] = carry

    kmax = jnp.max(jnp.abs(kc_ref[0]).reshape(bp, n_heads, hd), axis=0)
    zb = jnp.concatenate([jnp.sum(jnp.abs(q_s[h * t:(h + 1) * t, :]) * kmax[h:h + 1, :], axis=-1, keepdims=True)
                          for h in heads], axis=0)

    @pl.when(jnp.max(carry_s[...] + zb) > SB_LOG_CUT)
    def _():
        z = scores(lambda h: kc_ref[0, pl.ds(h, bp, stride=n_heads), :])
        att, carry = weights(z, carry_s[...], _suffix_ones(bp), None)
        for h in heads:
            acc_s[h * t:(h + 1) * t, :] += _dot(att[h * t:(h + 1) * t],
                                                vc_ref[0, pl.ds(h, bp, stride=n_heads), :])
        carry_s[...] = carry

    @pl.when(j == pl.num_programs(1) - 1)
    def _():
        o_ref[0] = jnp.concatenate([acc_s[h * t:(h + 1) * t, :] for h in heads], axis=1).astype(o_ref.dtype)


def sb_decode(qkv, k_cache, v_cache, *, block=512):
    bsz, t_len, d3 = qkv.shape
    _, past_len, n_heads, hd = k_cache.shape
    assert hd == SB_HEAD_DIM and n_heads * hd * 3 == d3 and t_len % 8 == 0
    bp = min(block, past_len)
    assert past_len % bp == 0
    n_blk = past_len // bp
    k_cache = k_cache.reshape(bsz, past_len * n_heads, hd)
    v_cache = v_cache.reshape(bsz, past_len * n_heads, hd)
    cache_spec = pl.BlockSpec((1, bp * n_heads, hd), lambda b, j: (b, n_blk - 1 - j, 0))
    return pl.pallas_call(
        functools.partial(_sb_decode_kernel, t=t_len, n_heads=n_heads, bp=bp, scale=hd ** -0.5),
        out_shape=jax.ShapeDtypeStruct((bsz, t_len, n_heads * hd), BF16),
        grid=(bsz, n_blk),
        in_specs=[pl.BlockSpec((1, t_len, d3), lambda b, j: (b, 0, 0)), cache_spec, cache_spec],
        out_specs=pl.BlockSpec((1, t_len, n_heads * hd), lambda b, j: (b, 0, 0)),
        scratch_shapes=[pltpu.VMEM((n_heads * t_len, hd), F32), pltpu.VMEM((n_heads * t_len, hd), F32),
                        pltpu.VMEM((n_heads * t_len, 1), F32)],
        compiler_params=_cparams("parallel", "arbitrary"),
        name="sb_decode",
    )(qkv, k_cache, v_cache)


def sb_layer_padded(x, t_valid, k_past, v_past, norm_g, w_qkv, w_out):
    bsz, t_len, d = x.shape
    x2 = x.reshape(bsz * t_len, d)
    qkv = norm_matmul(x2, norm_g, w_qkv).reshape(bsz, t_len, 3 * d)
    o = sb_attention(qkv) if k_past is None else sb_decode(qkv, k_past, v_past)
    out = matmul_res(o.reshape(bsz * t_len, d), w_out, x2)
    n_heads = d // SB_HEAD_DIM
    k_new = qkv[:, :t_valid, d:2 * d].reshape(bsz, t_valid, n_heads, SB_HEAD_DIM)
    v_new = qkv[:, :t_valid, 2 * d:].reshape(bsz, t_valid, n_heads, SB_HEAD_DIM)
    return out.reshape(bsz, t_len, d), k_new, v_new


def _run_trunk(x, t_valid, st, p):
    bsz, t_len, d = x.shape
    new = {}
    bf = lambda a: a.astype(BF16)

    def ffn_layer(x, i):
        return ffn(x.reshape(bsz * t_len, d), p['norm_ffn'][i], bf(p['ffn_w_gu'][i]),
                   bf(p['ffn_w_down'][i])).reshape(bsz, t_len, d)

    x, new['ssm_conv'], new['ssm'] = mamba2_layer_padded(
        x, t_valid, st['ssm_conv'], st['ssm'], p['norm_mix'][0], bf(p['mb_w_in']), p['mb_conv_w'],
        p['mb_conv_b'], p['mb_dt_bias'], p['mb_a_log'], p['mb_d'], p['mb_norm'], bf(p['mb_w_out']))
    x = ffn_layer(x, 0)
    x, new['gdn_conv'], new['gdn'] = gdn_layer_padded(
        x, t_valid, st['gdn_conv'], st['gdn'], p['norm_mix'][1], bf(p['gdn_w_in']), p['gdn_conv_w'],
        p['gdn_dt_bias'], p['gdn_a_log'], p['gdn_norm'], bf(p['gdn_w_out']))
    x = ffn_layer(x, 1)
    x, new['rwkv_shift'], new['rwkv'] = rwkv_layer_padded(
        x, t_valid, st['rwkv_shift'], st['rwkv'], p['norm_mix'][2], p)
    x = ffn_layer(x, 2)
    x, new['sb_k'], new['sb_v'] = sb_layer_padded(
        x, t_valid, st['sb_k'], st['sb_v'], p['norm_mix'][3], bf(p['sb_w_qkv']), bf(p['sb_w_out']))
    x = ffn_layer(x, 3)
    y = rmsnorm_rows(x.reshape(bsz * t_len, d), p['norm_final']).reshape(bsz, t_len, d)
    return y, new


def kernel(x_prompt, x_sample, state_ssm, state_ssm_conv, state_gdn, state_gdn_conv, state_rwkv, state_rwkv_shift, cache_sb_k, cache_sb_v, meta_tokens, norm_mix, norm_ffn, norm_final, ffn_w_gu, ffn_w_down, mb_w_in, mb_conv_w, mb_conv_b, mb_dt_bias, mb_a_log, mb_d, mb_norm, mb_w_out, gdn_w_in, gdn_conv_w, gdn_dt_bias, gdn_a_log, gdn_norm, gdn_w_out, rw_mu, rw_w_rkv, rw_w0, rw_w1, rw_w2, rw_a0, rw_a1, rw_a2, rw_g1, rw_g2, rw_k_k, rw_k_a, rw_r_k, rw_ln_w, rw_ln_b, rw_w_out, sb_w_qkv, sb_w_out):
    p = dict(
        norm_mix=norm_mix, norm_ffn=norm_ffn, norm_final=norm_final, ffn_w_gu=ffn_w_gu, ffn_w_down=ffn_w_down,
        mb_w_in=mb_w_in, mb_conv_w=mb_conv_w, mb_conv_b=mb_conv_b, mb_dt_bias=mb_dt_bias, mb_a_log=mb_a_log,
        mb_d=mb_d, mb_norm=mb_norm, mb_w_out=mb_w_out,
        gdn_w_in=gdn_w_in, gdn_conv_w=gdn_conv_w, gdn_dt_bias=gdn_dt_bias, gdn_a_log=gdn_a_log,
        gdn_norm=gdn_norm, gdn_w_out=gdn_w_out,
        rw_mu=rw_mu, rw_w_rkv=rw_w_rkv, rw_w0=rw_w0, rw_w1=rw_w1, rw_w2=rw_w2, rw_a0=rw_a0, rw_a1=rw_a1,
        rw_a2=rw_a2, rw_g1=rw_g1, rw_g2=rw_g2, rw_k_k=rw_k_k, rw_k_a=rw_k_a, rw_r_k=rw_r_k, rw_ln_w=rw_ln_w,
        rw_ln_b=rw_ln_b, rw_w_out=rw_w_out, sb_w_qkv=sb_w_qkv, sb_w_out=sb_w_out)
    bsz, seq, d = x_prompt.shape
    n_meta = meta_tokens.shape[0]
    t_valid = n_meta + seq
    t_pad = -(-t_valid // PROMPT_ROW_ALIGN) * PROMPT_ROW_ALIGN
    meta = jnp.broadcast_to(meta_tokens[None], (bsz, n_meta, d))
    x0 = jnp.concatenate([meta, x_prompt, jnp.zeros((bsz, t_pad - t_valid, d), x_prompt.dtype)], axis=1)
    fresh = dict(
        ssm=jnp.zeros((bsz,) + state_ssm.shape[1:], F32), ssm_conv=jnp.zeros((bsz,) + state_ssm_conv.shape[1:], F32),
        gdn=jnp.zeros((bsz,) + state_gdn.shape[1:], F32), gdn_conv=jnp.zeros((bsz,) + state_gdn_conv.shape[1:], F32),
        rwkv=jnp.zeros((bsz,) + state_rwkv.shape[1:], F32),
        rwkv_shift=jnp.zeros((bsz,) + state_rwkv_shift.shape[1:], F32), sb_k=None, sb_v=None)
    y_full, sp = _run_trunk(x0, t_valid, fresh, p)
    y_prompt = y_full[:, n_meta:t_valid]
    past = dict(ssm=state_ssm, ssm_conv=state_ssm_conv, gdn=state_gdn, gdn_conv=state_gdn_conv,
                rwkv=state_rwkv, rwkv_shift=state_rwkv_shift, sb_k=cache_sb_k, sb_v=cache_sb_v)
    y_sample, ss = _run_trunk(x_sample, x_sample.shape[1], past, p)
    return (y_prompt, y_sample,
            sp['ssm'], sp['ssm_conv'], sp['gdn'], sp['gdn_conv'], sp['rwkv'], sp['rwkv_shift'],
            sp['sb_k'], sp['sb_v'],
            ss['ssm'], ss['ssm_conv'], ss['gdn'], ss['gdn_conv'], ss['rwkv'], ss['rwkv_shift'],
            ss['sb_k'], ss['sb_v'])
```

```python
import functools
import math

import jax
import jax.numpy as jnp
from jax import lax
from jax.experimental import pallas as pl
from jax.experimental.pallas import tpu as pltpu

F32 = jnp.float32
BF16 = jnp.bfloat16
EPS = 1e-6
VMEM_LIMIT = 48 * 1024 * 1024
HI = lax.Precision.HIGHEST

PROMPT_ROW_ALIGN = 256
MB_HEAD_DIM = 64
MB_D_STATE = 128
MB_GROUPS = 4
GDN_DK = 128
GDN_DV = 128
RW_HEAD_DIM = 64
RW_GN_EPS = 64e-5
RW_DECAY_SCALE = 0.6065306597126334
SB_HEAD_DIM = 64
SB_LOG_CUT = -100.0
SB_BOUND_SLACK = 1.01


def _cparams(*sem):
    return pltpu.CompilerParams(dimension_semantics=sem, vmem_limit_bytes=VMEM_LIMIT)


def _dot(a, b):
    return jnp.dot(a.astype(BF16), b.astype(BF16), preferred_element_type=F32)


def _dot_nt(a, b):
    return lax.dot_general(a.astype(BF16), b.astype(BF16), (((1,), (1,)), ((), ())),
                           preferred_element_type=F32)


def _dot_tn(a, b):
    return lax.dot_general(a.astype(BF16), b.astype(BF16), (((0,), (0,)), ((), ())),
                           preferred_element_type=F32)


def _dot_hi(a, b):
    return jnp.dot(a, b, preferred_element_type=F32, precision=HI)


def _silu(x):
    return x * jax.nn.sigmoid(x)


def _softplus(x):
    return jnp.maximum(x, 0.0) + jnp.log(1.0 + jnp.exp(-jnp.abs(x)))


def _rmsnorm(x, g):
    return x * lax.rsqrt(jnp.mean(x * x, axis=-1, keepdims=True) + EPS) * g


def _row_tile(n, want):
    return n if n <= want else want


def _norm_matmul_kernel(x_ref, g_ref, w_ref, o_ref):
    h = _rmsnorm(x_ref[...], g_ref[...]).astype(BF16)
    o_ref[...] = jnp.dot(h, w_ref[...], preferred_element_type=F32)


def norm_matmul(x, g, w, *, tm=256):
    n, d = x.shape
    m = w.shape[1]
    tm = _row_tile(n, tm)
    return pl.pallas_call(
        _norm_matmul_kernel,
        out_shape=jax.ShapeDtypeStruct((n, m), F32),
        grid=(pl.cdiv(n, tm),),
        in_specs=[pl.BlockSpec((tm, d), lambda i: (i, 0)),
                  pl.BlockSpec((1, d), lambda i: (0, 0)),
                  pl.BlockSpec((d, m), lambda i: (0, 0))],
        out_specs=pl.BlockSpec((tm, m), lambda i: (i, 0)),
        compiler_params=_cparams("parallel"),
        name="norm_matmul",
    )(x, g.reshape(1, d), w)


def _matmul_res_kernel(a_ref, w_ref, r_ref, o_ref):
    o_ref[...] = r_ref[...] + jnp.dot(a_ref[...].astype(BF16), w_ref[...], preferred_element_type=F32)


def matmul_res(a, w, res, *, tm=512):
    n, k = a.shape
    d = w.shape[1]
    tm = _row_tile(n, tm)
    return pl.pallas_call(
        _matmul_res_kernel,
        out_shape=jax.ShapeDtypeStruct((n, d), F32),
        grid=(pl.cdiv(n, tm),),
        in_specs=[pl.BlockSpec((tm, k), lambda i: (i, 0)),
                  pl.BlockSpec((k, d), lambda i: (0, 0)),
                  pl.BlockSpec((tm, d), lambda i: (i, 0))],
        out_specs=pl.BlockSpec((tm, d), lambda i: (i, 0)),
        compiler_params=_cparams("parallel"),
        name="matmul_res",
    )(a, w, res)


def _ffn_kernel(x_ref, g_ref, wg_ref, wu_ref, wd_ref, o_ref, h_ref, acc_ref):
    f = pl.program_id(1)

    @pl.when(f == 0)
    def _():
        h_ref[...] = _rmsnorm(x_ref[...], g_ref[...]).astype(BF16)
        acc_ref[...] = jnp.zeros_like(acc_ref)

    h = h_ref[...]
    gate = jnp.dot(h, wg_ref[...], preferred_element_type=F32)
    up = jnp.dot(h, wu_ref[...], preferred_element_type=F32)
    act = (_silu(gate) * up).astype(BF16)
    acc_ref[...] += jnp.dot(act, wd_ref[...], preferred_element_type=F32)

    @pl.when(f == pl.num_programs(1) - 1)
    def _():
        o_ref[...] = x_ref[...] + acc_ref[...]


def ffn(x, g, w_gu, w_down, *, tm=512, tf=1408):
    n, d = x.shape
    f = w_down.shape[0]
    tm = _row_tile(n, tm)
    if f % tf:
        tf = f
    nf = f // tf
    return pl.pallas_call(
        _ffn_kernel,
        out_shape=jax.ShapeDtypeStruct((n, d), F32),
        grid=(pl.cdiv(n, tm), nf),
        in_specs=[pl.BlockSpec((tm, d), lambda i, j: (i, 0)),
                  pl.BlockSpec((1, d), lambda i, j: (0, 0)),
                  pl.BlockSpec((d, tf), lambda i, j: (0, j)),
                  pl.BlockSpec((d, tf), lambda i, j: (0, j + nf)),
                  pl.BlockSpec((tf, d), lambda i, j: (j, 0))],
        out_specs=pl.BlockSpec((tm, d), lambda i, j: (i, 0)),
        scratch_shapes=[pltpu.VMEM((tm, d), BF16), pltpu.VMEM((tm, d), F32)],
        compiler_params=_cparams("parallel", "arbitrary"),
        name="ffn",
    )(x, g.reshape(1, d), w_gu, w_gu, w_down)


def _rmsnorm_kernel(x_ref, g_ref, o_ref):
    o_ref[...] = _rmsnorm(x_ref[...], g_ref[...])


def rmsnorm_rows(x, g, *, tm=1024):
    n, d = x.shape
    tm = _row_tile(n, tm)
    return pl.pallas_call(
        _rmsnorm_kernel,
        out_shape=jax.ShapeDtypeStruct((n, d), F32),
        grid=(pl.cdiv(n, tm),),
        in_specs=[pl.BlockSpec((tm, d), lambda i: (i, 0)),
                  pl.BlockSpec((1, d), lambda i: (0, 0))],
        out_specs=pl.BlockSpec((tm, d), lambda i: (i, 0)),
        compiler_params=_cparams("parallel"),
        name="rmsnorm",
    )(x, g.reshape(1, d))


def _causal_conv_silu(raw, tail_ref, buf_ref, w_ref, bias, q):
    del tail_ref
    buf_ref[pl.ds(8, q), :] = raw
    out = buf_ref[pl.ds(5, q), :] * w_ref[0:1, :]
    out = out + buf_ref[pl.ds(6, q), :] * w_ref[1:2, :]
    out = out + buf_ref[pl.ds(7, q), :] * w_ref[2:3, :]
    out = out + raw * w_ref[3:4, :]
    if bias is not None:
        out = out + bias
    buf_ref[pl.ds(5, 3), :] = buf_ref[pl.ds(q + 5, 3), :]
    return _silu(out)


def _tri(q, strict=False):
    r = lax.broadcasted_iota(jnp.int32, (q, q), 0)
    c = lax.broadcasted_iota(jnp.int32, (q, q), 1)
    return (r > c) if strict else (r >= c)


def _mamba_kernel(z_ref, xs_ref, bc_ref, dt_ref, cx_ref, cbc_ref, h0_ref,
                  cwx_ref, cwbc_ref, cbx_ref, cbbc_ref, dtb_ref, a_ref, dskip_ref, nw_ref,
                  y_ref, h_ref, bufx_ref, bufbc_ref, *, q, t_len, n_heads):
    c = pl.program_id(1)
    n_state = MB_D_STATE
    hpg = n_heads // MB_GROUPS
    gw = hpg * MB_HEAD_DIM

    @pl.when(c == 0)
    def _():
        bufx_ref[pl.ds(5, 3), :] = cx_ref[0]
        bufbc_ref[pl.ds(5, 3), :] = cbc_ref[0]
        h_ref[0] = h0_ref[0]

    rows = c * q + lax.broadcasted_iota(jnp.int32, (q, 1), 0)
    valid = rows < t_len
    xs = _causal_conv_silu(jnp.where(valid, xs_ref[0], 0.0), None, bufx_ref, cwx_ref, cbx_ref[...], q)
    bc = _causal_conv_silu(jnp.where(valid, bc_ref[0], 0.0), None, bufbc_ref, cwbc_ref, cbbc_ref[...], q)

    lane = lax.broadcasted_iota(jnp.int32, (q, 128), 1)
    dt = _softplus(jnp.where(valid & (lane < n_heads), dt_ref[0], 0.0) + dtb_ref[...])
    dt = jnp.where(valid & (lane < n_heads), dt, 0.0)
    da = dt * a_ref[...]
    incl = _tri(q)
    acum = _dot_hi(incl.astype(F32), da)
    acum_t = acum.T
    dt_t = dt.T
    last = acum[q - 1:q, :]
    wts = jnp.exp(last - acum) * dt
    e_acum = jnp.exp(acum)
    e_last = jnp.exp(last)

    lo_half = lax.broadcasted_iota(jnp.int32, (q, 128), 1) < MB_HEAD_DIM
    lo_rows = lax.broadcasted_iota(jnp.int32, (128, 128), 0) < MB_HEAD_DIM

    z = z_ref[0]
    for g in range(MB_GROUPS):
        bm = bc[:, g * n_state:(g + 1) * n_state]
        cm = bc[:, (MB_GROUPS + g) * n_state:(MB_GROUPS + g + 1) * n_state]
        cb = _dot_nt(cm, bm)
        hg = h_ref[0, g * gw:(g + 1) * gw, :]
        ch = _dot_nt(cm, hg)
        ys = []
        for p in range(hpg // 2):
            h_e = g * hpg + 2 * p
            x_pair = xs[:, h_e * MB_HEAD_DIM:(h_e + 2) * MB_HEAD_DIM]
            outs = []
            for hh in (h_e, h_e + 1):
                seg = acum[:, hh:hh + 1] - acum_t[hh:hh + 1, :]
                decay = jnp.where(incl, jnp.exp(jnp.where(incl, seg, 0.0)), 0.0)
                m = cb * decay * dt_t[hh:hh + 1, :]
                outs.append(_dot(m, x_pair))
            y_pair = jnp.where(lo_half, outs[0], outs[1])
            e_pair = jnp.where(lo_half, e_acum[:, h_e:h_e + 1], e_acum[:, h_e + 1:h_e + 2])
            y_pair = y_pair + e_pair * ch[:, 2 * p * MB_HEAD_DIM:(2 * p + 2) * MB_HEAD_DIM]
            ys.append(y_pair)
            w_pair = jnp.where(lo_half, wts[:, h_e:h_e + 1], wts[:, h_e + 1:h_e + 2])
            upd = _dot_tn(x_pair * w_pair, bm)
            scale = jnp.where(lo_rows, e_last[:, h_e:h_e + 1], e_last[:, h_e + 1:h_e + 2])
            r0 = h_e * MB_HEAD_DIM
            h_ref[0, r0:r0 + 128, :] = h_ref[0, r0:r0 + 128, :] * scale + upd
        yg = jnp.concatenate(ys, axis=1)
        xg = xs[:, g * gw:(g + 1) * gw]
        yg = yg + xg * dskip_ref[:, g * gw:(g + 1) * gw]
        yg = yg * _silu(z[:, g * gw:(g + 1) * gw])
        yg = yg * lax.rsqrt(jnp.mean(yg * yg, axis=-1, keepdims=True) + EPS)
        y_ref[0, :, g * gw:(g + 1) * gw] = (yg * nw_ref[:, g * gw:(g + 1) * gw]).astype(y_ref.dtype)


def mamba_core(proj, t_valid, conv_buf, h0, conv_w, conv_b, dt_bias, a_log, d_skip, norm_w, *, chunk=128):
    bsz, t_len, _ = proj.shape
    n_heads = dt_bias.shape[0]
    d_inner = n_heads * MB_HEAD_DIM
    gn = MB_GROUPS * MB_D_STATE
    assert d_inner % 1024 == 0 and 2 * gn == 1024 and n_heads <= 128
    q = min(chunk, t_len)
    assert q % 8 == 0 and t_len % q == 0
    n_chunks = t_len // q
    dtb = jnp.zeros((1, 128), F32).at[0, :n_heads].set(dt_bias)
    a_row = jnp.zeros((1, 128), F32).at[0, :n_heads].set(-jnp.exp(a_log))
    dskip = jnp.repeat(d_skip, MB_HEAD_DIM).reshape(1, d_inner)
    kern = functools.partial(_mamba_kernel, q=q, t_len=t_valid, n_heads=n_heads)
    xblk = d_inner // 1024
    y, h = pl.pallas_call(
        kern,
        out_shape=(jax.ShapeDtypeStruct((bsz, t_len, d_inner), BF16),
                   jax.ShapeDtypeStruct((bsz, n_heads * MB_HEAD_DIM, MB_D_STATE), F32)),
        grid=(bsz, n_chunks),
        in_specs=[
            pl.BlockSpec((1, q, d_inner), lambda b, c: (b, c, 0)),
            pl.BlockSpec((1, q, d_inner), lambda b, c: (b, c, 1)),
            pl.BlockSpec((1, q, 2 * gn), lambda b, c: (b, c, 2 * xblk)),
            pl.BlockSpec((1, q, 128), lambda b, c: (b, c, (2 * d_inner + 2 * gn) // 128)),
            pl.BlockSpec((1, 3, d_inner), lambda b, c: (b, 0, 0)),
            pl.BlockSpec((1, 3, 2 * gn), lambda b, c: (b, 0, xblk)),
            pl.BlockSpec((1, n_heads * MB_HEAD_DIM, MB_D_STATE), lambda b, c: (b, 0, 0)),
            pl.BlockSpec((4, d_inner), lambda b, c: (0, 0)),
            pl.BlockSpec((4, 2 * gn), lambda b, c: (0, xblk)),
            pl.BlockSpec((1, d_inner), lambda b, c: (0, 0)),
            pl.BlockSpec((1, 2 * gn), lambda b, c: (0, xblk)),
            pl.BlockSpec((1, 128), lambda b, c: (0, 0)),
            pl.BlockSpec((1, 128), lambda b, c: (0, 0)),
            pl.BlockSpec((1, d_inner), lambda b, c: (0, 0)),
            pl.BlockSpec((1, d_inner), lambda b, c: (0, 0)),
        ],
        out_specs=(pl.BlockSpec((1, q, d_inner), lambda b, c: (b, c, 0)),
                   pl.BlockSpec((1, n_heads * MB_HEAD_DIM, MB_D_STATE), lambda b, c: (b, 0, 0))),
        scratch_shapes=[pltpu.VMEM((q + 8, d_inner), F32), pltpu.VMEM((q + 8, 2 * gn), F32)],
        compiler_params=_cparams("parallel", "arbitrary"),
        name="mamba_core",
    )(proj, proj, proj, proj, conv_buf, conv_buf, h0.reshape(bsz, n_heads * MB_HEAD_DIM, MB_D_STATE),
      conv_w, conv_w, conv_b.reshape(1, -1), conv_b.reshape(1, -1), dtb, a_row, dskip,
      norm_w.reshape(1, d_inner))
    return y, h.reshape(bsz, n_heads, MB_HEAD_DIM, MB_D_STATE)


def mamba2_layer(x, conv_buf, ssm_state, norm_g, w_in, conv_w, conv_b, dt_bias, a_log, d_skip,
                 norm_w, w_out):
    return mamba2_layer_padded(x, x.shape[1], conv_buf, ssm_state, norm_g, w_in, conv_w, conv_b, dt_bias,
                               a_log, d_skip, norm_w, w_out)


def mamba2_layer_padded(x, t_valid, conv_buf, ssm_state, norm_g, w_in, conv_w, conv_b, dt_bias, a_log,
                        d_skip, norm_w, w_out):
    bsz, t_len, d = x.shape
    n_heads = dt_bias.shape[0]
    d_inner = n_heads * MB_HEAD_DIM
    conv_dim = conv_w.shape[1]
    x2 = x.reshape(bsz * t_len, d)
    proj = norm_matmul(x2, norm_g, w_in).reshape(bsz, t_len, -1)
    y, h_new = mamba_core(proj, t_valid, conv_buf, ssm_state, conv_w, conv_b, dt_bias, a_log, d_skip, norm_w)
    conv_new = proj[:, t_valid - 3:t_valid, d_inner:d_inner + conv_dim]
    out = matmul_res(y.reshape(bsz * t_len, d_inner), w_out, x2)
    return out.reshape(bsz, t_len, d), conv_new, h_new


def _split(a):
    hi = a.astype(BF16)
    lo = (a - hi.astype(F32)).astype(BF16)
    return hi, lo


def _mm(a, b):
    return jnp.dot(a, b, preferred_element_type=F32)


def _dot3(a, b):
    ah, al = _split(a)
    bh, bl = _split(b)
    return _mm(ah, bh) + (_mm(ah, bl) + _mm(al, bh))


INV_BASE = 8


def _unit_lower_inverse(mats):
    n = mats[0].shape[0]
    r = lax.broadcasted_iota(jnp.int32, (n, n), 0)
    c = lax.broadcasted_iota(jnp.int32, (n, n), 1)
    eye = jnp.where(r == c, 1.0, 0.0)
    same = (r // INV_BASE) == (c // INV_BASE)
    diag = [jnp.where(same, a, 0.0) for a in mats]
    xs = [eye - dg for dg in diag]
    ps = diag
    k = 1
    while 2 * k < INV_BASE:
        ps = [_dot(p, p) for p in ps]
        xs = [x + _dot(x, p) for x, p in zip(xs, ps)]
        k *= 2
    w = INV_BASE
    while w < n:
        off = ((r // (2 * w)) == (c // (2 * w))) != ((r // w) == (c // w))
        ts = [_dot(jnp.where(off, a, 0.0), x) for a, x in zip(mats, xs)]
        xs = [x - _dot(x, t) for x, t in zip(xs, ts)]
        w *= 2
    return xs


def _solve_unit_lower(mats, rhs, apply):
    x_inv = _unit_lower_inverse(mats)
    d0 = apply(x_inv, rhs, _dot)
    t0 = apply(mats, d0, _dot3)
    res = [b - d - t for b, d, t in zip(rhs, d0, t0)]
    corr = apply(x_inv, res, _dot)
    return [d + e for d, e in zip(d0, corr)]


def _gdn_kernel(q_ref, k_ref, v_ref, z_ref, ba_ref, cq_ref, ck_ref, cv_ref, s0_ref,
                cwq_ref, cwk_ref, cwv_ref, dtb_ref, al_ref, nw_ref,
                o_ref, s_ref, bufq_ref, bufk_ref, bufv_ref, *, q, t_len, n_hv, n_hk):
    c = pl.program_id(1)
    rep = n_hv // n_hk

    @pl.when(c == 0)
    def _():
        bufq_ref[pl.ds(5, 3), :] = cq_ref[0]
        bufk_ref[pl.ds(5, 3), :] = ck_ref[0]
        bufv_ref[pl.ds(5, 3), :] = cv_ref[0]
        s_ref[0] = s0_ref[0]

    rows = c * q + lax.broadcasted_iota(jnp.int32, (q, 1), 0)
    valid = rows < t_len
    qc = _causal_conv_silu(jnp.where(valid, q_ref[0], 0.0), None, bufq_ref, cwq_ref, None, q)
    kc = _causal_conv_silu(jnp.where(valid, k_ref[0], 0.0), None, bufk_ref, cwk_ref, None, q)
    vc = _causal_conv_silu(jnp.where(valid, v_ref[0], 0.0), None, bufv_ref, cwv_ref, None, q)

    lane = lax.broadcasted_iota(jnp.int32, (q, 128), 1)
    ba = jnp.where(valid & (lane < 2 * n_hv), ba_ref[0], 0.0)
    beta = jnp.where(valid & (lane < n_hv), jax.nn.sigmoid(ba), 0.0)
    g = jnp.where(valid & (lane >= n_hv) & (lane < 2 * n_hv),
                  -jnp.exp(al_ref[...]) * _softplus(ba + dtb_ref[...]), 0.0)
    incl = _tri(q)
    strict = _tri(q, strict=True)
    gcum = _dot_hi(incl.astype(F32), g)
    gcum_t = gcum.T
    eg = jnp.exp(gcum)
    glast = gcum[q - 1:q, :]
    w_last = jnp.exp(glast - gcum)
    e_last = jnp.exp(glast)

    z = z_ref[0]
    qn, kn, kk, qk0 = [], [], [], []
    for kh in range(n_hk):
        qh = qc[:, kh * GDN_DK:(kh + 1) * GDN_DK]
        kh_ = kc[:, kh * GDN_DK:(kh + 1) * GDN_DK]
        qh = qh * (lax.rsqrt(jnp.sum(qh * qh, axis=-1, keepdims=True) + 1e-6) * (GDN_DK ** -0.5))
        kh_ = kh_ * lax.rsqrt(jnp.sum(kh_ * kh_, axis=-1, keepdims=True) + 1e-6)
        qn.append(qh)
        kn.append(kh_)
        kk.append(_dot_nt(kh_, kh_))
        qk0.append(_dot_nt(qh, kh_))
    heads = range(n_hv)
    s_old = [s_ref[0, h * GDN_DK:(h + 1) * GDN_DK, :] for h in heads]
    ks = [_dot(kn[h // rep], s_old[h]) for h in heads]
    qs = [_dot(qn[h // rep], s_old[h]) for h in heads]
    dec, a_mat, rhs = [], [], []
    for h in heads:
        hl = n_hv + h
        seg = gcum[:, hl:hl + 1] - gcum_t[hl:hl + 1, :]
        dec.append(jnp.where(incl, jnp.exp(jnp.where(incl, seg, 0.0)), 0.0))
        bt = beta[:, h:h + 1]
        a_mat.append(jnp.where(strict, kk[h // rep] * dec[h], 0.0) * bt)
        rhs.append(bt * (vc[:, h * GDN_DV:(h + 1) * GDN_DV] - eg[:, hl:hl + 1] * ks[h]))
    delta = _solve_unit_lower(a_mat, rhs, lambda ms, vs, dot: [dot(m, v) for m, v in zip(ms, vs)])
    o_mm = [_dot(qk0[h // rep] * dec[h], delta[h]) for h in heads]
    upd = [_dot_tn(kn[h // rep] * w_last[:, n_hv + h:n_hv + h + 1], delta[h]) for h in heads]
    for h in heads:
        hl = n_hv + h
        s_ref[0, h * GDN_DK:(h + 1) * GDN_DK, :] = s_old[h] * e_last[:, hl:hl + 1] + upd[h]
        o = eg[:, hl:hl + 1] * qs[h] + o_mm[h]
        o = o * lax.rsqrt(jnp.mean(o * o, axis=-1, keepdims=True) + EPS) * nw_ref[...]
        o = o * _silu(z[:, h * GDN_DV:(h + 1) * GDN_DV])
        o_ref[0, :, h * GDN_DV:(h + 1) * GDN_DV] = o.astype(o_ref.dtype)


def gdn_core(proj, t_valid, conv_buf, s0, conv_w, dt_bias, a_log, norm_w, *, chunk=64):
    bsz, t_len, _ = proj.shape
    n_hv = dt_bias.shape[0]
    val_dim = n_hv * GDN_DV
    key_dim = (conv_w.shape[1] - val_dim) // 2
    n_hk = key_dim // GDN_DK
    assert val_dim == 2 * key_dim and key_dim % 128 == 0 and 2 * n_hv <= 128
    q = min(chunk, t_len)
    assert q % 8 == 0 and t_len % q == 0
    n_chunks = t_len // q
    pad = jnp.zeros((n_hv,), F32)
    dtb = jnp.zeros((1, 128), F32).at[0, n_hv:2 * n_hv].set(dt_bias)
    alog = jnp.zeros((1, 128), F32).at[0, n_hv:2 * n_hv].set(a_log)
    del pad
    kern = functools.partial(_gdn_kernel, q=q, t_len=t_valid, n_hv=n_hv, n_hk=n_hk)
    ba_blk = (2 * key_dim + 2 * val_dim) // 128
    s0 = s0.reshape(bsz, n_hv * GDN_DK, GDN_DV)
    o, s = pl.pallas_call(
        kern,
        out_shape=(jax.ShapeDtypeStruct((bsz, t_len, val_dim), BF16),
                   jax.ShapeDtypeStruct(s0.shape, F32)),
        grid=(bsz, n_chunks),
        in_specs=[
            pl.BlockSpec((1, q, key_dim), lambda b, c: (b, c, 0)),
            pl.BlockSpec((1, q, key_dim), lambda b, c: (b, c, 1)),
            pl.BlockSpec((1, q, val_dim), lambda b, c: (b, c, 1)),
            pl.BlockSpec((1, q, val_dim), lambda b, c: (b, c, 2)),
            pl.BlockSpec((1, q, 128), lambda b, c: (b, c, ba_blk)),
            pl.BlockSpec((1, 3, key_dim), lambda b, c: (b, 0, 0)),
            pl.BlockSpec((1, 3, key_dim), lambda b, c: (b, 0, 1)),
            pl.BlockSpec((1, 3, val_dim), lambda b, c: (b, 0, 1)),
            pl.BlockSpec((1,) + s0.shape[1:], lambda b, c: (b, 0, 0)),
            pl.BlockSpec((4, key_dim), lambda b, c: (0, 0)),
            pl.BlockSpec((4, key_dim), lambda b, c: (0, 1)),
            pl.BlockSpec((4, val_dim), lambda b, c: (0, 1)),
            pl.BlockSpec((1, 128), lambda b, c: (0, 0)),
            pl.BlockSpec((1, 128), lambda b, c: (0, 0)),
            pl.BlockSpec((1, GDN_DV), lambda b, c: (0, 0)),
        ],
        out_specs=(pl.BlockSpec((1, q, val_dim), lambda b, c: (b, c, 0)),
                   pl.BlockSpec((1,) + s0.shape[1:], lambda b, c: (b, 0, 0))),
        scratch_shapes=[pltpu.VMEM((q + 8, key_dim), F32), pltpu.VMEM((q + 8, key_dim), F32),
                        pltpu.VMEM((q + 8, val_dim), F32)],
        compiler_params=_cparams("parallel", "arbitrary"),
        name="gdn_core",
    )(proj, proj, proj, proj, proj, conv_buf, conv_buf, conv_buf, s0,
      conv_w, conv_w, conv_w, dtb, alog, norm_w.reshape(1, GDN_DV))
    return o, s.reshape(bsz, n_hv, GDN_DK, GDN_DV)


def gdn_layer(x, conv_buf, s0, norm_g, w_in, conv_w, dt_bias, a_log, norm_w, w_out):
    return gdn_layer_padded(x, x.shape[1], conv_buf, s0, norm_g, w_in, conv_w, dt_bias, a_log, norm_w, w_out)


def gdn_layer_padded(x, t_valid, conv_buf, s0, norm_g, w_in, conv_w, dt_bias, a_log, norm_w, w_out):
    bsz, t_len, d = x.shape
    conv_dim = conv_w.shape[1]
    x2 = x.reshape(bsz * t_len, d)
    proj = norm_matmul(x2, norm_g, w_in).reshape(bsz, t_len, -1)
    o, s_new = gdn_core(proj, t_valid, conv_buf, s0, conv_w, dt_bias, a_log, norm_w)
    conv_new = proj[:, t_valid - 3:t_valid, :conv_dim]
    out = matmul_res(o.reshape(bsz * t_len, -1), w_out, x2)
    return out.reshape(bsz, t_len, d), conv_new, s_new


def _head_ones(width, head):
    r = lax.broadcasted_iota(jnp.int32, (width, width), 0) // head
    c = lax.broadcasted_iota(jnp.int32, (width, width), 1) // head
    return jnp.where(r == c, 1.0, 0.0).astype(BF16)


def _head_sum(x, ones):
    hi, lo = _split(x)
    return _mm(hi, ones) + _mm(lo, ones)


def _rw_proj_kernel(hn_ref, pv_ref, mu_ref, wr_ref, wk_ref, wv_ref, w1_ref, w2_ref, a1_ref, a2_ref,
                    g1_ref, g2_ref, w0_ref, a0_ref, kk_ref, ka_ref,
                    r_ref, lw_ref, k_ref, v_ref, kn_ref, b_ref, g_ref):
    hn = hn_ref[...]
    dlt = pv_ref[...] - hn

    def mix(c):
        return (hn + dlt * mu_ref[c:c + 1, :]).astype(BF16)

    r_ref[...] = _mm(mix(0), wr_ref[...])
    k = _mm(mix(1), wk_ref[...])
    v_ref[...] = _mm(mix(2), wv_ref[...])
    dec = w0_ref[...] + _mm(jnp.tanh(_mm(mix(3), w1_ref[...])).astype(BF16), w2_ref[...])
    lw_ref[...] = -RW_DECAY_SCALE * jax.nn.sigmoid(dec)
    a = jax.nn.sigmoid(a0_ref[...] + _mm(_mm(mix(4), a1_ref[...]).astype(BF16), a2_ref[...]))
    g_ref[...] = _mm(jax.nn.sigmoid(_mm(mix(5), g1_ref[...])).astype(BF16), g2_ref[...])
    ones = _head_ones(128, RW_HEAD_DIM)
    kn = k * kk_ref[...]
    d = kn.shape[1]
    for j in range(d // 128):
        sl = slice(j * 128, (j + 1) * 128)
        knj = kn[:, sl]
        knj = knj * lax.rsqrt(_head_sum(knj * knj, ones) + 1e-6)
        kn_ref[:, sl] = knj
        b_ref[:, sl] = knj * a[:, sl]
    k_ref[...] = k * (1.0 + (a - 1.0) * ka_ref[...])


def rw_proj(hn, prev, p, *, tm=256):
    n, d = hn.shape
    tm = _row_tile(n, tm)
    row = lambda i: (i, 0)
    fix = lambda i: (0, 0)
    big = pl.BlockSpec((tm, d), row)
    vec = pl.BlockSpec((1, d), fix)

    def full(a):
        return pl.BlockSpec(a.shape, fix)

    ws = [p['rw_w_rkv'][0].astype(BF16), p['rw_w_rkv'][1].astype(BF16), p['rw_w_rkv'][2].astype(BF16),
          p['rw_w1'].astype(BF16), p['rw_w2'].astype(BF16), p['rw_a1'].astype(BF16), p['rw_a2'].astype(BF16),
          p['rw_g1'].astype(BF16), p['rw_g2'].astype(BF16)]
    vecs = [p['rw_w0'].reshape(1, d), p['rw_a0'].reshape(1, d), p['rw_k_k'].reshape(1, d),
            p['rw_k_a'].reshape(1, d)]
    return pl.pallas_call(
        _rw_proj_kernel,
        out_shape=tuple(jax.ShapeDtypeStruct((n, d), F32) for _ in range(7)),
        grid=(pl.cdiv(n, tm),),
        in_specs=[big, big, full(p['rw_mu'])] + [full(w) for w in ws] + [vec] * 4,
        out_specs=tuple(big for _ in range(7)),
        compiler_params=_cparams("parallel"),
        name="rw_proj",
    )(hn, prev, p['rw_mu'], *ws, *vecs)


def _rw_scan_kernel(r_ref, lw_ref, k_ref, v_ref, kn_ref, b_ref, g_ref, s0_ref, rk_ref, lnw_ref, lnb_ref,
                    o_ref, s_ref, *, q, t_len):
    c = pl.program_id(1)

    @pl.when(c == 0)
    def _():
        s_ref[0] = s0_ref[0]

    rows = c * q + lax.broadcasted_iota(jnp.int32, (q, 1), 0)
    valid = rows < t_len
    incl = _tri(q)
    strict = _tri(q, strict=True)
    lw_all = jnp.where(valid, lw_ref[0], 0.0)
    cum_all = _dot_hi(incl.astype(F32), lw_all)
    ones = _head_ones(128, RW_HEAD_DIM)
    lo = lax.broadcasted_iota(jnp.int32, (q, 128), 1) < RW_HEAD_DIM
    blockdiag = ((lax.broadcasted_iota(jnp.int32, (128, 128), 0) < RW_HEAD_DIM)
                 == (lax.broadcasted_iota(jnp.int32, (128, 128), 1) < RW_HEAD_DIM))
    d = lw_all.shape[1]
    pairs = range(d // 128)
    sls = [slice(p * 128, (p + 1) * 128) for p in pairs]
    rs, ks, vs, bends, kends, s_old, cum_ends = [], [], [], [], [], [], []
    pk, pb, u0, y0 = [], [], [], []
    for p in pairs:
        sl = sls[p]
        lw = lw_all[:, sl]
        cum = cum_all[:, sl]
        cum_end = cum[q - 1:q, :]
        r = jnp.where(valid, r_ref[0, :, sl], 0.0)
        k = jnp.where(valid, k_ref[0, :, sl], 0.0)
        v = jnp.where(valid, v_ref[0, :, sl], 0.0)
        kn = jnp.where(valid, kn_ref[0, :, sl], 0.0)
        b = jnp.where(valid, b_ref[0, :, sl], 0.0)
        kq = kn * jnp.exp(cum - lw)
        rq = r * jnp.exp(cum)
        einv = jnp.exp(-cum)
        kd = k * einv
        bd = b * einv
        eend = jnp.exp(cum_end - cum)
        s2 = s_ref[0, sl, :]
        lhs = jnp.concatenate([jnp.where(lo, kq, 0.0), jnp.where(lo, 0.0, kq),
                               jnp.where(lo, rq, 0.0), jnp.where(lo, 0.0, rq)], axis=0)
        pk.append(_dot_nt(lhs, kd))
        pb.append(_dot_nt(lhs, bd))
        u0.append(_dot_nt(kq, s2))
        y0.append(_dot_nt(rq, s2))
        rs.append(r)
        ks.append(k)
        vs.append(v)
        kends.append(k * eend)
        bends.append(b * eend)
        s_old.append(s2)
        cum_ends.append(cum_end)
    ab = [jnp.where(strict, pb[p][e * q:(e + 1) * q], 0.0) for p in pairs for e in range(2)]
    rhs = [u0[p] + jnp.where(lo, _dot(jnp.where(strict, pk[p][0:q], 0.0), vs[p]),
                             _dot(jnp.where(strict, pk[p][q:2 * q], 0.0), vs[p])) for p in pairs]

    def per_head(ms, vecs, dot):
        return [jnp.where(lo, dot(ms[2 * p], vecs[p]), dot(ms[2 * p + 1], vecs[p])) for p in pairs]

    us = _solve_unit_lower(ab, rhs, per_head)
    ys = []
    for p in pairs:
        rk = [jnp.where(incl, pk[p][(2 + e) * q:(3 + e) * q], 0.0) for e in range(2)]
        rb = [jnp.where(incl, pb[p][(2 + e) * q:(3 + e) * q], 0.0) for e in range(2)]
        ys.append(y0[p] + jnp.where(lo, _dot(rk[0], vs[p]) - _dot(rb[0], us[p]),
                                    _dot(rk[1], vs[p]) - _dot(rb[1], us[p])))
    s_new = [s_old[p] * jnp.exp(cum_ends[p]) + _dot_tn(vs[p], kends[p]) - _dot_tn(us[p], bends[p])
             for p in pairs]
    for p in pairs:
        sl = sls[p]
        s_ref[0, sl, :] = jnp.where(blockdiag, s_new[p], 0.0)
        y = ys[p]
        mean = _head_sum(y, ones) * (1.0 / RW_HEAD_DIM)
        yc = y - mean
        var = _head_sum(yc * yc, ones) * (1.0 / RW_HEAD_DIM)
        yn = yc * lax.rsqrt(var + RW_GN_EPS) * lnw_ref[:, sl] + lnb_ref[:, sl]
        bonus = _head_sum(rs[p] * ks[p] * rk_ref[:, sl], ones) * vs[p]
        o_ref[0, :, sl] = ((yn + bonus) * g_ref[0, :, sl]).astype(o_ref.dtype)


def rw_scan(r, lw, k, v, kn, b, g, t_valid, s0, r_k, ln_w, ln_b, *, chunk=64):
    bsz, t_len, d = r.shape
    q = min(chunk, t_len)
    assert q % 8 == 0 and t_len % q == 0 and d % 128 == 0
    n_heads = d // RW_HEAD_DIM
    s5 = s0.reshape(bsz, n_heads // 2, 2, RW_HEAD_DIM, RW_HEAD_DIM)
    s2 = jnp.einsum('bpevk,ef->bpevfk', s5, jnp.eye(2, dtype=F32)).reshape(bsz, d, 128)
    blk = pl.BlockSpec((1, q, d), lambda bb, c: (bb, c, 0))
    vec = pl.BlockSpec((1, d), lambda bb, c: (0, 0))
    st = pl.BlockSpec((1, d, 128), lambda bb, c: (bb, 0, 0))
    o, s_new = pl.pallas_call(
        functools.partial(_rw_scan_kernel, q=q, t_len=t_valid),
        out_shape=(jax.ShapeDtypeStruct((bsz, t_len, d), BF16), jax.ShapeDtypeStruct((bsz, d, 128), F32)),
        grid=(bsz, t_len // q),
        in_specs=[blk] * 7 + [st, vec, vec, vec],
        out_specs=(blk, st),
        compiler_params=_cparams("parallel", "arbitrary"),
        name="rw_scan",
    )(r, lw, k, v, kn, b, g, s2, r_k.reshape(1, d), ln_w.reshape(1, d), ln_b.reshape(1, d))
    s6 = s_new.reshape(bsz, n_heads // 2, 2, RW_HEAD_DIM, 2, RW_HEAD_DIM)
    s_out = jnp.stack([s6[:, :, 0, :, 0, :], s6[:, :, 1, :, 1, :]], axis=2)
    return o, s_out.reshape(bsz, n_heads, RW_HEAD_DIM, RW_HEAD_DIM)


def rwkv_layer_padded(x, t_valid, shift_buf, s0, norm_g, p):
    bsz, t_len, d = x.shape
    x2 = x.reshape(bsz * t_len, d)
    hn = rmsnorm_rows(x2, norm_g).reshape(bsz, t_len, d)
    prev = jnp.concatenate([shift_buf, hn[:, :-1]], axis=1)
    outs = rw_proj(hn.reshape(-1, d), prev.reshape(-1, d), p)
    r, lw, k, v, kn, b, g = (a.reshape(bsz, t_len, d) for a in outs)
    yg, s_new = rw_scan(r, lw, k, v, kn, b, g, t_valid, s0, p['rw_r_k'], p['rw_ln_w'], p['rw_ln_b'])
    out = matmul_res(yg.reshape(bsz * t_len, d), p['rw_w_out'].astype(BF16), x2)
    return out.reshape(bsz, t_len, d), hn[:, t_valid - 1:t_valid], s_new


def _dot3_nt(a, b):
    ah, al = _split(a)
    bh, bl = _split(b)
    dn = (((1,), (1,)), ((), ()))
    f = lambda u, w: lax.dot_general(u, w, dn, preferred_element_type=F32)
    return f(ah, bh) + (f(ah, bl) + f(al, bh))


def _suffix_ones(n):
    r = lax.broadcasted_iota(jnp.int32, (n, n), 0)
    c = lax.broadcasted_iota(jnp.int32, (n, n), 1)
    return jnp.where(r >= c, 1.0, 0.0).astype(BF16)


def _sb_block(q_e, kblk, vblk, carry, acc, lo, suffix, mask):
    outs = []
    new_carry = []
    for e in range(2):
        z = _dot3_nt(q_e[e], kblk)
        lnb = -_softplus(z)
        if mask is not None:
            lnb = jnp.where(mask, lnb, 0.0)
        hi, lw = _split(lnb)
        rsum = _mm(hi, suffix) + _mm(lw, suffix) + carry[e]
        att = jnp.exp(z + rsum)
        if mask is not None:
            att = jnp.where(mask, att, 0.0)
        outs.append(_dot(att, vblk))
        new_carry.append(rsum[:, 0:1])
    return new_carry, acc + jnp.where(lo, outs[0], outs[1])


def _sb_kernel(q_ref, kn_ref, vn_ref, o_ref, kmax_ref, *, bq, scale):
    i = pl.program_id(2)
    t_new = kn_ref.shape[1]

    @pl.when(i == 0)
    def _():
        def body(j, m):
            rows = kn_ref[0, pl.ds(pl.multiple_of(j * bq, bq), bq), :]
            return jnp.maximum(m, jnp.max(jnp.abs(rows).reshape(bq // 8, 8, 128), axis=0))

        m8 = lax.fori_loop(0, t_new // bq, body, jnp.zeros((8, 128), F32))
        kmax_ref[...] = jnp.max(m8, axis=0, keepdims=True)

    lo = lax.broadcasted_iota(jnp.int32, (bq, 128), 1) < SB_HEAD_DIM
    q = q_ref[0] * scale
    q_e = [jnp.where(lo, q, 0.0), jnp.where(lo, 0.0, q)]
    zb = _head_sum(jnp.abs(q) * kmax_ref[...], _head_ones(128, SB_HEAD_DIM))
    zb0 = zb[:, 0:1]
    zb1 = zb[:, SB_HEAD_DIM:SB_HEAD_DIM + 1]
    suffix = _suffix_ones(bq)
    mask = _tri(bq, strict=True)
    zero_c = jnp.zeros((bq, 1), F32)
    start = pl.multiple_of(i * bq, bq)
    carry, acc = _sb_block(q_e, kn_ref[0, pl.ds(start, bq), :], vn_ref[0, pl.ds(start, bq), :],
                           [zero_c, zero_c], jnp.zeros((bq, 128), F32), lo, suffix, mask)

    def live(c0, c1):
        return (jnp.max(jnp.maximum(c0 + zb0, c1 + zb1)) > SB_LOG_CUT).astype(jnp.int32)

    def walk(k_ref, v_ref, n_blocks, blk, sfx, st):
        def cond(s):
            return (s[0] < n_blocks) & (s[4] > 0)

        def body(s):
            jj, c0, c1, a, _ = s
            off = pl.multiple_of((n_blocks - 1 - jj) * blk, blk)
            (c0, c1), a = _sb_block(q_e, k_ref[0, pl.ds(off, blk), :], v_ref[0, pl.ds(off, blk), :],
                                    [c0, c1], a, lo, sfx, None)
            return jj + 1, c0, c1, a, live(c0, c1)

        return lax.while_loop(cond, body, st)

    st = walk(kn_ref, vn_ref, i, bq, suffix, (jnp.int32(0), carry[0], carry[1], acc, live(*carry)))
    o_ref[0] = st[3].astype(o_ref.dtype)


def sb_attention(qkv, *, block=256):
    bsz, t_len, d3 = qkv.shape
    d = d3 // 3
    assert d % 128 == 0
    n_pairs = d // 128
    bq = min(block, t_len)
    assert t_len % bq == 0 and bq % 8 == 0
    return pl.pallas_call(
        functools.partial(_sb_kernel, bq=bq, scale=SB_HEAD_DIM ** -0.5),
        out_shape=jax.ShapeDtypeStruct((bsz, t_len, d), BF16),
        grid=(bsz, n_pairs, t_len // bq),
        in_specs=[pl.BlockSpec((1, bq, 128), lambda b, p, i: (b, i, p)),
                  pl.BlockSpec((1, t_len, 128), lambda b, p, i: (b, 0, n_pairs + p)),
                  pl.BlockSpec((1, t_len, 128), lambda b, p, i: (b, 0, 2 * n_pairs + p))],
        out_specs=pl.BlockSpec((1, bq, 128), lambda b, p, i: (b, i, p)),
        compiler_params=_cparams("parallel", "parallel", "arbitrary"),
        scratch_shapes=[pltpu.VMEM((1, 128), F32)],
        name="sb_attention",
    )(qkv, qkv, qkv)


def _sb_decode_kernel(qkv_ref, kc_ref, vc_ref, o_ref, q_s, acc_s, carry_s, *, t, n_heads, bp, scale):
    j = pl.program_id(1)
    hd = SB_HEAD_DIM
    d = n_heads * hd
    rows = n_heads * t
    heads = range(n_heads)

    def scores(keys):
        return jnp.concatenate([_dot3_nt(q_s[h * t:(h + 1) * t, :], keys(h)) for h in heads], axis=0)

    def weights(z, carry, suffix, mask):
        lnb = -_softplus(z)
        if mask is not None:
            lnb = jnp.where(mask, lnb, 0.0)
        hi, lw = _split(lnb)
        rsum = _mm(hi, suffix) + _mm(lw, suffix) + carry
        att = jnp.exp(z + rsum)
        if mask is not None:
            att = jnp.where(mask, att, 0.0)
        return att, rsum[:, 0:1]

    @pl.when(j == 0)
    def _():
        for h in heads:
            q_s[h * t:(h + 1) * t, :] = qkv_ref[0, :, h * hd:(h + 1) * hd] * scale
        z = scores(lambda h: qkv_ref[0, :, d + h * hd:d + (h + 1) * hd])
        qi = lax.rem(lax.broadcasted_iota(jnp.int32, (rows, t), 0), t)
        mask = lax.broadcasted_iota(jnp.int32, (rows, t), 1) < qi
        att, carry = weights(z, 0.0, _suffix_ones(t), mask)
        for h in heads:
            acc_s[h * t:(h + 1) * t, :] = _dot(att[h * t:(h + 1) * t],
                                               qkv_ref[0, :, 2 * d + h * hd:2 * d + (h + 1) * hd])
        carry_s[...] = carry

    ub = jnp.concatenate([_dot(jnp.abs(q_s[h * t:(h + 1) * t, :]), jnp.abs(kc_ref[0, h])) for h in heads], axis=0)
    zb = SB_BOUND_SLACK * jnp.max(ub, axis=-1, keepdims=True)

    @pl.when(jnp.max(carry_s[...] + zb) > SB_LOG_CUT)
    def _():
        z = jnp.concatenate([_dot3(q_s[h * t:(h + 1) * t, :], kc_ref[0, h]) for h in heads], axis=0)
        att, carry = weights(z, carry_s[...], _suffix_ones(bp), None)
        for h in heads:
            acc_s[h * t:(h + 1) * t, :] += _dot_nt(att[h * t:(h + 1) * t], vc_ref[0, h])
        carry_s[...] = carry

    @pl.when(j == pl.num_programs(1) - 1)
    def _():
        o_ref[0] = jnp.concatenate([acc_s[h * t:(h + 1) * t, :] for h in heads], axis=1).astype(o_ref.dtype)


def sb_decode(qkv, k_cache, v_cache, *, block=512):
    bsz, t_len, d3 = qkv.shape
    _, past_len, n_heads, hd = k_cache.shape
    assert hd == SB_HEAD_DIM and n_heads * hd * 3 == d3 and t_len % 8 == 0
    bp = min(block, past_len)
    assert past_len % bp == 0
    n_blk = past_len // bp
    k_cache = jnp.transpose(k_cache, (0, 2, 3, 1))
    v_cache = jnp.transpose(v_cache, (0, 2, 3, 1))
    cache_spec = pl.BlockSpec((1, n_heads, hd, bp), lambda b, j: (b, 0, 0, n_blk - 1 - j))
    return pl.pallas_call(
        functools.partial(_sb_decode_kernel, t=t_len, n_heads=n_heads, bp=bp, scale=hd ** -0.5),
        out_shape=jax.ShapeDtypeStruct((bsz, t_len, n_heads * hd), BF16),
        grid=(bsz, n_blk),
        in_specs=[pl.BlockSpec((1, t_len, d3), lambda b, j: (b, 0, 0)), cache_spec, cache_spec],
        out_specs=pl.BlockSpec((1, t_len, n_heads * hd), lambda b, j: (b, 0, 0)),
        scratch_shapes=[pltpu.VMEM((n_heads * t_len, hd), F32), pltpu.VMEM((n_heads * t_len, hd), F32),
                        pltpu.VMEM((n_heads * t_len, 1), F32)],
        compiler_params=_cparams("parallel", "arbitrary"),
        name="sb_decode",
    )(qkv, k_cache, v_cache)


def sb_layer_padded(x, t_valid, k_past, v_past, norm_g, w_qkv, w_out):
    bsz, t_len, d = x.shape
    x2 = x.reshape(bsz * t_len, d)
    qkv = norm_matmul(x2, norm_g, w_qkv).reshape(bsz, t_len, 3 * d)
    o = sb_attention(qkv) if k_past is None else sb_decode(qkv, k_past, v_past)
    out = matmul_res(o.reshape(bsz * t_len, d), w_out, x2)
    n_heads = d // SB_HEAD_DIM
    k_new = qkv[:, :t_valid, d:2 * d].reshape(bsz, t_valid, n_heads, SB_HEAD_DIM)
    v_new = qkv[:, :t_valid, 2 * d:].reshape(bsz, t_valid, n_heads, SB_HEAD_DIM)
    return out.reshape(bsz, t_len, d), k_new, v_new


def _run_trunk(x, t_valid, st, p):
    bsz, t_len, d = x.shape
    new = {}
    bf = lambda a: a.astype(BF16)

    def ffn_layer(x, i):
        return ffn(x.reshape(bsz * t_len, d), p['norm_ffn'][i], bf(p['ffn_w_gu'][i]),
                   bf(p['ffn_w_down'][i])).reshape(bsz, t_len, d)

    x, new['ssm_conv'], new['ssm'] = mamba2_layer_padded(
        x, t_valid, st['ssm_conv'], st['ssm'], p['norm_mix'][0], bf(p['mb_w_in']), p['mb_conv_w'],
        p['mb_conv_b'], p['mb_dt_bias'], p['mb_a_log'], p['mb_d'], p['mb_norm'], bf(p['mb_w_out']))
    x = ffn_layer(x, 0)
    x, new['gdn_conv'], new['gdn'] = gdn_layer_padded(
        x, t_valid, st['gdn_conv'], st['gdn'], p['norm_mix'][1], bf(p['gdn_w_in']), p['gdn_conv_w'],
        p['gdn_dt_bias'], p['gdn_a_log'], p['gdn_norm'], bf(p['gdn_w_out']))
    x = ffn_layer(x, 1)
    x, new['rwkv_shift'], new['rwkv'] = rwkv_layer_padded(
        x, t_valid, st['rwkv_shift'], st['rwkv'], p['norm_mix'][2], p)
    x = ffn_layer(x, 2)
    x, new['sb_k'], new['sb_v'] = sb_layer_padded(
        x, t_valid, st['sb_k'], st['sb_v'], p['norm_mix'][3], bf(p['sb_w_qkv']), bf(p['sb_w_out']))
    x = ffn_layer(x, 3)
    y = rmsnorm_rows(x.reshape(bsz * t_len, d), p['norm_final']).reshape(bsz, t_len, d)
    return y, new


def kernel(x_prompt, x_sample, state_ssm, state_ssm_conv, state_gdn, state_gdn_conv, state_rwkv, state_rwkv_shift, cache_sb_k, cache_sb_v, meta_tokens, norm_mix, norm_ffn, norm_final, ffn_w_gu, ffn_w_down, mb_w_in, mb_conv_w, mb_conv_b, mb_dt_bias, mb_a_log, mb_d, mb_norm, mb_w_out, gdn_w_in, gdn_conv_w, gdn_dt_bias, gdn_a_log, gdn_norm, gdn_w_out, rw_mu, rw_w_rkv, rw_w0, rw_w1, rw_w2, rw_a0, rw_a1, rw_a2, rw_g1, rw_g2, rw_k_k, rw_k_a, rw_r_k, rw_ln_w, rw_ln_b, rw_w_out, sb_w_qkv, sb_w_out):
    p = dict(
        norm_mix=norm_mix, norm_ffn=norm_ffn, norm_final=norm_final, ffn_w_gu=ffn_w_gu, ffn_w_down=ffn_w_down,
        mb_w_in=mb_w_in, mb_conv_w=mb_conv_w, mb_conv_b=mb_conv_b, mb_dt_bias=mb_dt_bias, mb_a_log=mb_a_log,
        mb_d=mb_d, mb_norm=mb_norm, mb_w_out=mb_w_out,
        gdn_w_in=gdn_w_in, gdn_conv_w=gdn_conv_w, gdn_dt_bias=gdn_dt_bias, gdn_a_log=gdn_a_log,
        gdn_norm=gdn_norm, gdn_w_out=gdn_w_out,
        rw_mu=rw_mu, rw_w_rkv=rw_w_rkv, rw_w0=rw_w0, rw_w1=rw_w1, rw_w2=rw_w2, rw_a0=rw_a0, rw_a1=rw_a1,
        rw_a2=rw_a2, rw_g1=rw_g1, rw_g2=rw_g2, rw_k_k=rw_k_k, rw_k_a=rw_k_a, rw_r_k=rw_r_k, rw_ln_w=rw_ln_w,
        rw_ln_b=rw_ln_b, rw_w_out=rw_w_out, sb_w_qkv=sb_w_qkv, sb_w_out=sb_w_out)
    bsz, seq, d = x_prompt.shape
    n_meta = meta_tokens.shape[0]
    t_valid = n_meta + seq
    t_pad = -(-t_valid // PROMPT_ROW_ALIGN) * PROMPT_ROW_ALIGN
    meta = jnp.broadcast_to(meta_tokens[None], (bsz, n_meta, d))
    x0 = jnp.concatenate([meta, x_prompt, jnp.zeros((bsz, t_pad - t_valid, d), x_prompt.dtype)], axis=1)
    fresh = dict(
        ssm=jnp.zeros((bsz,) + state_ssm.shape[1:], F32), ssm_conv=jnp.zeros((bsz,) + state_ssm_conv.shape[1:], F32),
        gdn=jnp.zeros((bsz,) + state_gdn.shape[1:], F32), gdn_conv=jnp.zeros((bsz,) + state_gdn_conv.shape[1:], F32),
        rwkv=jnp.zeros((bsz,) + state_rwkv.shape[1:], F32),
        rwkv_shift=jnp.zeros((bsz,) + state_rwkv_shift.shape[1:], F32), sb_k=None, sb_v=None)
    y_full, sp = _run_trunk(x0, t_valid, fresh, p)
    y_prompt = y_full[:, n_meta:t_valid]
    past = dict(ssm=state_ssm, ssm_conv=state_ssm_conv, gdn=state_gdn, gdn_conv=state_gdn_conv,
                rwkv=state_rwkv, rwkv_shift=state_rwkv_shift, sb_k=cache_sb_k, sb_v=cache_sb_v)
    y_sample, ss = _run_trunk(x_sample, x_sample.shape[1], past, p)
    return (y_prompt, y_sample,
            sp['ssm'], sp['ssm_conv'], sp['gdn'], sp['gdn_conv'], sp['rwkv'], sp['rwkv_shift'],
            sp['sb_k'], sp['sb_v'],
            ss['ssm'], ss['ssm_conv'], ss['gdn'], ss['gdn_conv'], ss['rwkv'], ss['rwkv_shift'],
            ss['sb_k'], ss['sb_v'])
```

```python
import functools
import math

import jax
import jax.numpy as jnp
from jax import lax
from jax.experimental import pallas as pl
from jax.experimental.pallas import tpu as pltpu

F32 = jnp.float32
BF16 = jnp.bfloat16
EPS = 1e-6
VMEM_LIMIT = 48 * 1024 * 1024
HI = lax.Precision.HIGHEST

PROJ_COL_CHUNK = 512
PROMPT_ROW_ALIGN = 256
MB_HEAD_DIM = 64
MB_D_STATE = 128
MB_GROUPS = 4
GDN_DK = 128
GDN_DV = 128
RW_HEAD_DIM = 64
RW_GN_EPS = 64e-5
RW_DECAY_SCALE = 0.6065306597126334
SB_HEAD_DIM = 64
SB_LOG_CUT = -100.0
SB_BOUND_SLACK = 1.01


def _cparams(*sem):
    return pltpu.CompilerParams(dimension_semantics=sem, vmem_limit_bytes=VMEM_LIMIT)


def _dot(a, b):
    return jnp.dot(a.astype(BF16), b.astype(BF16), preferred_element_type=F32)


def _dot_nt(a, b):
    return lax.dot_general(a.astype(BF16), b.astype(BF16), (((1,), (1,)), ((), ())),
                           preferred_element_type=F32)


def _dot_tn(a, b):
    return lax.dot_general(a.astype(BF16), b.astype(BF16), (((0,), (0,)), ((), ())),
                           preferred_element_type=F32)


def _dot_hi(a, b):
    return jnp.dot(a, b, preferred_element_type=F32, precision=HI)


def _silu(x):
    return x * jax.nn.sigmoid(x)


def _softplus(x):
    return jnp.maximum(x, 0.0) + jnp.log(1.0 + jnp.exp(-jnp.abs(x)))


def _rmsnorm(x, g):
    return x * lax.rsqrt(jnp.mean(x * x, axis=-1, keepdims=True) + EPS) * g


def _row_tile(n, want):
    return n if n <= want else want


def _resident(shape):
    zeros = (0,) * len(shape)
    return pl.BlockSpec(shape, lambda *_: zeros, pipeline_mode=pl.Buffered(1))


def _norm_matmul_kernel(x_ref, g_ref, w_ref, o_ref):
    h = _rmsnorm(x_ref[...], g_ref[...]).astype(BF16)
    o_ref[...] = jnp.dot(h, w_ref[...], preferred_element_type=F32)


def norm_matmul(x, g, w, *, tm=256):
    n, d = x.shape
    m = w.shape[1]
    tm = _row_tile(n, tm)
    return pl.pallas_call(
        _norm_matmul_kernel,
        out_shape=jax.ShapeDtypeStruct((n, m), F32),
        grid=(pl.cdiv(n, tm),),
        in_specs=[pl.BlockSpec((tm, d), lambda i: (i, 0)), _resident((1, d)), _resident((d, m))],
        out_specs=pl.BlockSpec((tm, m), lambda i: (i, 0)),
        compiler_params=_cparams("parallel"),
        name="norm_matmul",
    )(x, g.reshape(1, d), w)


def _proj_conv_kernel(*refs, tm, col0, cc, state_tile, state_row, has_bias):
    if has_bias:
        x_ref, g_ref, w_ref, cst_ref, cw_ref, cb_ref, o_ref, cnew_ref, buf_ref = refs
    else:
        x_ref, g_ref, w_ref, cst_ref, cw_ref, o_ref, cnew_ref, buf_ref = refs
    ti = pl.program_id(1)
    h = _rmsnorm(x_ref[0], g_ref[...]).astype(BF16)
    m = w_ref.shape[1]

    @pl.when(ti == 0)
    def _():
        buf_ref[pl.ds(5, 3), :] = cst_ref[0]

    for c0 in range(0, m, PROJ_COL_CHUNK):
        c1 = min(c0 + PROJ_COL_CHUNK, m)
        raw = jnp.dot(h, w_ref[:, c0:c1], preferred_element_type=F32)
        if c0 < col0 or c0 >= col0 + cc:
            o_ref[0, :, c0:c1] = raw
            continue
        b0, b1 = c0 - col0, c1 - col0
        buf_ref[pl.ds(8, tm), b0:b1] = raw
        out = buf_ref[pl.ds(5, tm), b0:b1] * cw_ref[0:1, b0:b1]
        out = out + buf_ref[pl.ds(6, tm), b0:b1] * cw_ref[1:2, b0:b1]
        out = out + buf_ref[pl.ds(7, tm), b0:b1] * cw_ref[2:3, b0:b1]
        out = out + raw * cw_ref[3:4, b0:b1]
        if has_bias:
            out = out + cb_ref[:, b0:b1]
        o_ref[0, :, c0:c1] = _silu(out)

    @pl.when(ti == state_tile)
    def _():
        cnew_ref[0] = buf_ref[pl.ds(8 + state_row, 3), :]

    buf_ref[pl.ds(5, 3), :] = buf_ref[pl.ds(tm + 5, 3), :]


def proj_conv(x, t_valid, g, w, conv_state, conv_w, conv_b, col0, *, tm=256):
    bsz, t_len, d = x.shape
    m = w.shape[1]
    cc = conv_w.shape[1]
    tm = _row_tile(t_len, tm)
    assert t_len % tm == 0 and col0 % PROJ_COL_CHUNK == 0 and cc % PROJ_COL_CHUNK == 0 and t_valid >= 3
    state_tile, state_row = divmod(t_valid - 3, tm)
    assert state_row + 3 <= tm, "the last three valid rows must sit in one row tile"
    has_bias = conv_b is not None
    args = [x, g.reshape(1, d), w, conv_state, conv_w] + ([conv_b.reshape(1, cc)] if has_bias else [])
    in_specs = [pl.BlockSpec((1, tm, d), lambda b, i: (b, i, 0)), _resident((1, d)), _resident((d, m)),
                pl.BlockSpec((1, 3, cc), lambda b, i: (b, 0, 0)), _resident((4, cc))]
    if has_bias:
        in_specs.append(_resident((1, cc)))
    return pl.pallas_call(
        functools.partial(_proj_conv_kernel, tm=tm, col0=col0, cc=cc, state_tile=state_tile,
                          state_row=state_row, has_bias=has_bias),
        out_shape=(jax.ShapeDtypeStruct((bsz, t_len, m), F32), jax.ShapeDtypeStruct((bsz, 3, cc), F32)),
        grid=(bsz, t_len // tm),
        in_specs=in_specs,
        out_specs=(pl.BlockSpec((1, tm, m), lambda b, i: (b, i, 0)),
                   pl.BlockSpec((1, 3, cc), lambda b, i: (b, 0, 0))),
        scratch_shapes=[pltpu.VMEM((tm + 8, cc), F32)],
        compiler_params=_cparams("parallel", "arbitrary"),
        name="proj_conv",
    )(*args)


def _matmul_res_kernel(a_ref, w_ref, r_ref, o_ref):
    o_ref[...] = r_ref[...] + jnp.dot(a_ref[...].astype(BF16), w_ref[...], preferred_element_type=F32)


def matmul_res(a, w, res, *, tm=512):
    n, k = a.shape
    d = w.shape[1]
    tm = _row_tile(n, tm)
    return pl.pallas_call(
        _matmul_res_kernel,
        out_shape=jax.ShapeDtypeStruct((n, d), F32),
        grid=(pl.cdiv(n, tm),),
        in_specs=[pl.BlockSpec((tm, k), lambda i: (i, 0)),
                  pl.BlockSpec((k, d), lambda i: (0, 0)),
                  pl.BlockSpec((tm, d), lambda i: (i, 0))],
        out_specs=pl.BlockSpec((tm, d), lambda i: (i, 0)),
        compiler_params=_cparams("parallel"),
        name="matmul_res",
    )(a, w, res)


def _ffn_kernel(x_ref, g_ref, wgu_ref, wd_ref, o_ref, *, f, tf):
    x = x_ref[...]
    h = _rmsnorm(x, g_ref[...]).astype(BF16)
    acc = x
    for c in range(f // tf):
        gate = jnp.dot(h, wgu_ref[:, c * tf:(c + 1) * tf], preferred_element_type=F32)
        up = jnp.dot(h, wgu_ref[:, f + c * tf:f + (c + 1) * tf], preferred_element_type=F32)
        act = (_silu(gate) * up).astype(BF16)
        acc = acc + jnp.dot(act, wd_ref[c * tf:(c + 1) * tf, :], preferred_element_type=F32)
    o_ref[...] = acc


def ffn(x, g, w_gu, w_down, *, tm=512, tf=1408):
    n, d = x.shape
    f = w_down.shape[0]
    tm = _row_tile(n, tm)
    if f % tf:
        tf = f
    return pl.pallas_call(
        functools.partial(_ffn_kernel, f=f, tf=tf),
        out_shape=jax.ShapeDtypeStruct((n, d), F32),
        grid=(pl.cdiv(n, tm),),
        in_specs=[pl.BlockSpec((tm, d), lambda i: (i, 0)), _resident((1, d)), _resident((d, 2 * f)),
                  _resident((f, d))],
        out_specs=pl.BlockSpec((tm, d), lambda i: (i, 0)),
        compiler_params=_cparams("parallel"),
        name="ffn",
    )(x, g.reshape(1, d), w_gu, w_down)


def _rmsnorm_kernel(x_ref, g_ref, o_ref):
    o_ref[...] = _rmsnorm(x_ref[...], g_ref[...])


def rmsnorm_rows(x, g, *, tm=1024):
    n, d = x.shape
    tm = _row_tile(n, tm)
    return pl.pallas_call(
        _rmsnorm_kernel,
        out_shape=jax.ShapeDtypeStruct((n, d), F32),
        grid=(pl.cdiv(n, tm),),
        in_specs=[pl.BlockSpec((tm, d), lambda i: (i, 0)),
                  pl.BlockSpec((1, d), lambda i: (0, 0))],
        out_specs=pl.BlockSpec((tm, d), lambda i: (i, 0)),
        compiler_params=_cparams("parallel"),
        name="rmsnorm",
    )(x, g.reshape(1, d))


def _tri(q, strict=False):
    r = lax.broadcasted_iota(jnp.int32, (q, q), 0)
    c = lax.broadcasted_iota(jnp.int32, (q, q), 1)
    return (r > c) if strict else (r >= c)


def _mamba_kernel(z_ref, xs_ref, bc_ref, dt_ref, h0_ref, dtb_ref, a_ref, dskip_ref, nw_ref,
                  y_ref, h_ref, *, q, t_len, n_heads):
    c = pl.program_id(1)
    n_state = MB_D_STATE
    hpg = n_heads // MB_GROUPS
    gw = hpg * MB_HEAD_DIM

    @pl.when(c == 0)
    def _():
        h_ref[0] = h0_ref[0]

    rows = c * q + lax.broadcasted_iota(jnp.int32, (q, 1), 0)
    valid = rows < t_len
    xs = xs_ref[0]
    bc = bc_ref[0]

    lane = lax.broadcasted_iota(jnp.int32, (q, 128), 1)
    dt = _softplus(jnp.where(valid & (lane < n_heads), dt_ref[0], 0.0) + dtb_ref[...])
    dt = jnp.where(valid & (lane < n_heads), dt, 0.0)
    da = dt * a_ref[...]
    incl = _tri(q)
    acum = _dot_hi(incl.astype(F32), da)
    acum_t = acum.T
    dt_t = dt.T
    last = acum[q - 1:q, :]
    wts = jnp.exp(last - acum) * dt
    e_acum = jnp.exp(acum)
    e_last = jnp.exp(last)

    lo_half = lax.broadcasted_iota(jnp.int32, (q, 128), 1) < MB_HEAD_DIM
    lo_rows = lax.broadcasted_iota(jnp.int32, (128, 128), 0) < MB_HEAD_DIM

    z = z_ref[0]
    for g in range(MB_GROUPS):
        bm = bc[:, g * n_state:(g + 1) * n_state]
        cm = bc[:, (MB_GROUPS + g) * n_state:(MB_GROUPS + g + 1) * n_state]
        cb = _dot_nt(cm, bm)
        hg = h_ref[0, g * gw:(g + 1) * gw, :]
        ch = _dot_nt(cm, hg)
        ys = []
        for p in range(hpg // 2):
            h_e = g * hpg + 2 * p
            x_pair = xs[:, h_e * MB_HEAD_DIM:(h_e + 2) * MB_HEAD_DIM]
            outs = []
            for hh in (h_e, h_e + 1):
                seg = acum[:, hh:hh + 1] - acum_t[hh:hh + 1, :]
                decay = jnp.where(incl, jnp.exp(jnp.where(incl, seg, 0.0)), 0.0)
                m = cb * decay * dt_t[hh:hh + 1, :]
                outs.append(_dot(m, x_pair))
            y_pair = jnp.where(lo_half, outs[0], outs[1])
            e_pair = jnp.where(lo_half, e_acum[:, h_e:h_e + 1], e_acum[:, h_e + 1:h_e + 2])
            y_pair = y_pair + e_pair * ch[:, 2 * p * MB_HEAD_DIM:(2 * p + 2) * MB_HEAD_DIM]
            ys.append(y_pair)
            w_pair = jnp.where(lo_half, wts[:, h_e:h_e + 1], wts[:, h_e + 1:h_e + 2])
            upd = _dot_tn(x_pair * w_pair, bm)
            scale = jnp.where(lo_rows, e_last[:, h_e:h_e + 1], e_last[:, h_e + 1:h_e + 2])
            r0 = h_e * MB_HEAD_DIM
            h_ref[0, r0:r0 + 128, :] = h_ref[0, r0:r0 + 128, :] * scale + upd
        yg = jnp.concatenate(ys, axis=1)
        xg = xs[:, g * gw:(g + 1) * gw]
        yg = yg + xg * dskip_ref[:, g * gw:(g + 1) * gw]
        yg = yg * _silu(z[:, g * gw:(g + 1) * gw])
        yg = yg * lax.rsqrt(jnp.mean(yg * yg, axis=-1, keepdims=True) + EPS)
        y_ref[0, :, g * gw:(g + 1) * gw] = (yg * nw_ref[:, g * gw:(g + 1) * gw]).astype(y_ref.dtype)


def mamba_core(proj, t_valid, h0, dt_bias, a_log, d_skip, norm_w, *, chunk=128):
    bsz, t_len, _ = proj.shape
    n_heads = dt_bias.shape[0]
    d_inner = n_heads * MB_HEAD_DIM
    gn = MB_GROUPS * MB_D_STATE
    assert d_inner % 1024 == 0 and 2 * gn == 1024 and n_heads <= 128
    q = min(chunk, t_len)
    assert q % 8 == 0 and t_len % q == 0
    n_chunks = t_len // q
    dtb = jnp.zeros((1, 128), F32).at[0, :n_heads].set(dt_bias)
    a_row = jnp.zeros((1, 128), F32).at[0, :n_heads].set(-jnp.exp(a_log))
    dskip = jnp.repeat(d_skip, MB_HEAD_DIM).reshape(1, d_inner)
    kern = functools.partial(_mamba_kernel, q=q, t_len=t_valid, n_heads=n_heads)
    xblk = d_inner // 1024
    y, h = pl.pallas_call(
        kern,
        out_shape=(jax.ShapeDtypeStruct((bsz, t_len, d_inner), BF16),
                   jax.ShapeDtypeStruct((bsz, n_heads * MB_HEAD_DIM, MB_D_STATE), F32)),
        grid=(bsz, n_chunks),
        in_specs=[
            pl.BlockSpec((1, q, d_inner), lambda b, c: (b, c, 0)),
            pl.BlockSpec((1, q, d_inner), lambda b, c: (b, c, 1)),
            pl.BlockSpec((1, q, 2 * gn), lambda b, c: (b, c, 2 * xblk)),
            pl.BlockSpec((1, q, 128), lambda b, c: (b, c, (2 * d_inner + 2 * gn) // 128)),
            pl.BlockSpec((1, n_heads * MB_HEAD_DIM, MB_D_STATE), lambda b, c: (b, 0, 0)),
            _resident((1, 128)), _resident((1, 128)), _resident((1, d_inner)), _resident((1, d_inner)),
        ],
        out_specs=(pl.BlockSpec((1, q, d_inner), lambda b, c: (b, c, 0)),
                   pl.BlockSpec((1, n_heads * MB_HEAD_DIM, MB_D_STATE), lambda b, c: (b, 0, 0))),
        compiler_params=_cparams("parallel", "arbitrary"),
        name="mamba_core",
    )(proj, proj, proj, proj, h0.reshape(bsz, n_heads * MB_HEAD_DIM, MB_D_STATE), dtb, a_row, dskip,
      norm_w.reshape(1, d_inner))
    return y, h.reshape(bsz, n_heads, MB_HEAD_DIM, MB_D_STATE)


def mamba2_layer_padded(x, t_valid, conv_buf, ssm_state, norm_g, w_in, conv_w, conv_b, dt_bias, a_log,
                        d_skip, norm_w, w_out):
    bsz, t_len, d = x.shape
    d_inner = dt_bias.shape[0] * MB_HEAD_DIM
    proj, conv_new = proj_conv(x, t_valid, norm_g, w_in, conv_buf, conv_w, conv_b, d_inner)
    y, h_new = mamba_core(proj, t_valid, ssm_state, dt_bias, a_log, d_skip, norm_w)
    out = matmul_res(y.reshape(bsz * t_len, d_inner), w_out, x.reshape(bsz * t_len, d))
    return out.reshape(bsz, t_len, d), conv_new, h_new


def _split(a):
    hi = a.astype(BF16)
    lo = (a - hi.astype(F32)).astype(BF16)
    return hi, lo


def _mm(a, b):
    return jnp.dot(a, b, preferred_element_type=F32)


def _dot3(a, b):
    ah, al = _split(a)
    bh, bl = _split(b)
    return _mm(ah, bh) + (_mm(ah, bl) + _mm(al, bh))


INV_BASE = 8


def _unit_lower_inverse(mats):
    n = mats[0].shape[0]
    r = lax.broadcasted_iota(jnp.int32, (n, n), 0)
    c = lax.broadcasted_iota(jnp.int32, (n, n), 1)
    eye = jnp.where(r == c, 1.0, 0.0)
    same = (r // INV_BASE) == (c // INV_BASE)
    diag = [jnp.where(same, a, 0.0) for a in mats]
    xs = [eye - dg for dg in diag]
    ps = diag
    k = 1
    while 2 * k < INV_BASE:
        ps = [_dot(p, p) for p in ps]
        xs = [x + _dot(x, p) for x, p in zip(xs, ps)]
        k *= 2
    w = INV_BASE
    while w < n:
        off = ((r // (2 * w)) == (c // (2 * w))) != ((r // w) == (c // w))
        ts = [_dot(jnp.where(off, a, 0.0), x) for a, x in zip(mats, xs)]
        xs = [x - _dot(x, t) for x, t in zip(xs, ts)]
        w *= 2
    return xs


def _solve_unit_lower(mats, rhs, apply):
    x_inv = _unit_lower_inverse(mats)
    d0 = apply(x_inv, rhs, _dot)
    t0 = apply(mats, d0, _dot3)
    res = [b - d - t for b, d, t in zip(rhs, d0, t0)]
    corr = apply(x_inv, res, _dot)
    return [d + e for d, e in zip(d0, corr)]


def _gdn_kernel(q_ref, k_ref, v_ref, z_ref, ba_ref, s0_ref, dtb_ref, al_ref, nw_ref,
                o_ref, s_ref, *, q, t_len, n_hv, n_hk):
    c = pl.program_id(1)
    rep = n_hv // n_hk

    @pl.when(c == 0)
    def _():
        s_ref[0] = s0_ref[0]

    rows = c * q + lax.broadcasted_iota(jnp.int32, (q, 1), 0)
    valid = rows < t_len
    qc = q_ref[0]
    kc = k_ref[0]
    vc = v_ref[0]

    lane = lax.broadcasted_iota(jnp.int32, (q, 128), 1)
    ba = jnp.where(valid & (lane < 2 * n_hv), ba_ref[0], 0.0)
    beta = jnp.where(valid & (lane < n_hv), jax.nn.sigmoid(ba), 0.0)
    g = jnp.where(valid & (lane >= n_hv) & (lane < 2 * n_hv),
                  -jnp.exp(al_ref[...]) * _softplus(ba + dtb_ref[...]), 0.0)
    incl = _tri(q)
    strict = _tri(q, strict=True)
    gcum = _dot_hi(incl.astype(F32), g)
    gcum_t = gcum.T
    eg = jnp.exp(gcum)
    glast = gcum[q - 1:q, :]
    w_last = jnp.exp(glast - gcum)
    e_last = jnp.exp(glast)

    z = z_ref[0]
    qn, kn, kk, qk0 = [], [], [], []
    for kh in range(n_hk):
        qh = qc[:, kh * GDN_DK:(kh + 1) * GDN_DK]
        kh_ = kc[:, kh * GDN_DK:(kh + 1) * GDN_DK]
        qh = qh * (lax.rsqrt(jnp.sum(qh * qh, axis=-1, keepdims=True) + 1e-6) * (GDN_DK ** -0.5))
        kh_ = kh_ * lax.rsqrt(jnp.sum(kh_ * kh_, axis=-1, keepdims=True) + 1e-6)
        qn.append(qh)
        kn.append(kh_)
        kk.append(_dot_nt(kh_, kh_))
        qk0.append(_dot_nt(qh, kh_))
    heads = range(n_hv)
    s_old = [s_ref[0, h * GDN_DK:(h + 1) * GDN_DK, :] for h in heads]
    ks = [_dot(kn[h // rep], s_old[h]) for h in heads]
    qs = [_dot(qn[h // rep], s_old[h]) for h in heads]
    dec, a_mat, rhs = [], [], []
    for h in heads:
        hl = n_hv + h
        seg = gcum[:, hl:hl + 1] - gcum_t[hl:hl + 1, :]
        dec.append(jnp.where(incl, jnp.exp(jnp.where(incl, seg, 0.0)), 0.0))
        bt = beta[:, h:h + 1]
        a_mat.append(jnp.where(strict, kk[h // rep] * dec[h], 0.0) * bt)
        rhs.append(bt * (vc[:, h * GDN_DV:(h + 1) * GDN_DV] - eg[:, hl:hl + 1] * ks[h]))
    delta = _solve_unit_lower(a_mat, rhs, lambda ms, vs, dot: [dot(m, v) for m, v in zip(ms, vs)])
    o_mm = [_dot(qk0[h // rep] * dec[h], delta[h]) for h in heads]
    upd = [_dot_tn(kn[h // rep] * w_last[:, n_hv + h:n_hv + h + 1], delta[h]) for h in heads]
    for h in heads:
        hl = n_hv + h
        s_ref[0, h * GDN_DK:(h + 1) * GDN_DK, :] = s_old[h] * e_last[:, hl:hl + 1] + upd[h]
        o = eg[:, hl:hl + 1] * qs[h] + o_mm[h]
        o = o * lax.rsqrt(jnp.mean(o * o, axis=-1, keepdims=True) + EPS) * nw_ref[...]
        o = o * _silu(z[:, h * GDN_DV:(h + 1) * GDN_DV])
        o_ref[0, :, h * GDN_DV:(h + 1) * GDN_DV] = o.astype(o_ref.dtype)


def gdn_core(proj, t_valid, s0, dt_bias, a_log, norm_w, *, chunk=64):
    bsz, t_len, width = proj.shape
    n_hv = dt_bias.shape[0]
    val_dim = n_hv * GDN_DV
    key_dim = (width - 2 * n_hv - 2 * val_dim) // 2
    n_hk = key_dim // GDN_DK
    assert val_dim == 2 * key_dim and key_dim % 128 == 0 and 2 * n_hv <= 128
    q = min(chunk, t_len)
    assert q % 8 == 0 and t_len % q == 0
    n_chunks = t_len // q
    dtb = jnp.zeros((1, 128), F32).at[0, n_hv:2 * n_hv].set(dt_bias)
    alog = jnp.zeros((1, 128), F32).at[0, n_hv:2 * n_hv].set(a_log)
    kern = functools.partial(_gdn_kernel, q=q, t_len=t_valid, n_hv=n_hv, n_hk=n_hk)
    ba_blk = (2 * key_dim + 2 * val_dim) // 128
    s0 = s0.reshape(bsz, n_hv * GDN_DK, GDN_DV)
    o, s = pl.pallas_call(
        kern,
        out_shape=(jax.ShapeDtypeStruct((bsz, t_len, val_dim), BF16),
                   jax.ShapeDtypeStruct(s0.shape, F32)),
        grid=(bsz, n_chunks),
        in_specs=[
            pl.BlockSpec((1, q, key_dim), lambda b, c: (b, c, 0)),
            pl.BlockSpec((1, q, key_dim), lambda b, c: (b, c, 1)),
            pl.BlockSpec((1, q, val_dim), lambda b, c: (b, c, 1)),
            pl.BlockSpec((1, q, val_dim), lambda b, c: (b, c, 2)),
            pl.BlockSpec((1, q, 128), lambda b, c: (b, c, ba_blk)),
            pl.BlockSpec((1,) + s0.shape[1:], lambda b, c: (b, 0, 0)),
            _resident((1, 128)), _resident((1, 128)), _resident((1, GDN_DV)),
        ],
        out_specs=(pl.BlockSpec((1, q, val_dim), lambda b, c: (b, c, 0)),
                   pl.BlockSpec((1,) + s0.shape[1:], lambda b, c: (b, 0, 0))),
        compiler_params=_cparams("parallel", "arbitrary"),
        name="gdn_core",
    )(proj, proj, proj, proj, proj, s0, dtb, alog, norm_w.reshape(1, GDN_DV))
    return o, s.reshape(bsz, n_hv, GDN_DK, GDN_DV)


def gdn_layer_padded(x, t_valid, conv_buf, s0, norm_g, w_in, conv_w, dt_bias, a_log, norm_w, w_out):
    bsz, t_len, d = x.shape
    proj, conv_new = proj_conv(x, t_valid, norm_g, w_in, conv_buf, conv_w, None, 0)
    o, s_new = gdn_core(proj, t_valid, s0, dt_bias, a_log, norm_w)
    out = matmul_res(o.reshape(bsz * t_len, -1), w_out, x.reshape(bsz * t_len, d))
    return out.reshape(bsz, t_len, d), conv_new, s_new


def _head_ones(width, head):
    r = lax.broadcasted_iota(jnp.int32, (width, width), 0) // head
    c = lax.broadcasted_iota(jnp.int32, (width, width), 1) // head
    return jnp.where(r == c, 1.0, 0.0).astype(BF16)


def _head_sum(x, ones):
    hi, lo = _split(x)
    return _mm(hi, ones) + _mm(lo, ones)


def _rw_proj_kernel(hn_ref, pv_ref, mu_ref, wr_ref, wk_ref, wv_ref, w1_ref, w2_ref, a1_ref, a2_ref,
                    g1_ref, g2_ref, w0_ref, a0_ref, kk_ref, ka_ref,
                    r_ref, lw_ref, k_ref, v_ref, kn_ref, b_ref, g_ref):
    hn = hn_ref[...]
    dlt = pv_ref[...] - hn

    def mix(c):
        return (hn + dlt * mu_ref[c:c + 1, :]).astype(BF16)

    r_ref[...] = _mm(mix(0), wr_ref[...])
    k = _mm(mix(1), wk_ref[...])
    v_ref[...] = _mm(mix(2), wv_ref[...])
    dec = w0_ref[...] + _mm(jnp.tanh(_mm(mix(3), w1_ref[...])).astype(BF16), w2_ref[...])
    lw_ref[...] = -RW_DECAY_SCALE * jax.nn.sigmoid(dec)
    a = jax.nn.sigmoid(a0_ref[...] + _mm(_mm(mix(4), a1_ref[...]).astype(BF16), a2_ref[...]))
    g_ref[...] = _mm(jax.nn.sigmoid(_mm(mix(5), g1_ref[...])).astype(BF16), g2_ref[...])
    ones = _head_ones(128, RW_HEAD_DIM)
    kn = k * kk_ref[...]
    d = kn.shape[1]
    for j in range(d // 128):
        sl = slice(j * 128, (j + 1) * 128)
        knj = kn[:, sl]
        knj = knj * lax.rsqrt(_head_sum(knj * knj, ones) + 1e-6)
        kn_ref[:, sl] = knj
        b_ref[:, sl] = knj * a[:, sl]
    k_ref[...] = k * (1.0 + (a - 1.0) * ka_ref[...])


def rw_proj(hn, prev, p, *, tm=256):
    n, d = hn.shape
    tm = _row_tile(n, tm)
    row = lambda i: (i, 0)
    fix = lambda i: (0, 0)
    big = pl.BlockSpec((tm, d), row)
    vec = pl.BlockSpec((1, d), fix)

    def full(a):
        return pl.BlockSpec(a.shape, fix)

    ws = [p['rw_w_rkv'][0].astype(BF16), p['rw_w_rkv'][1].astype(BF16), p['rw_w_rkv'][2].astype(BF16),
          p['rw_w1'].astype(BF16), p['rw_w2'].astype(BF16), p['rw_a1'].astype(BF16), p['rw_a2'].astype(BF16),
          p['rw_g1'].astype(BF16), p['rw_g2'].astype(BF16)]
    vecs = [p['rw_w0'].reshape(1, d), p['rw_a0'].reshape(1, d), p['rw_k_k'].reshape(1, d),
            p['rw_k_a'].reshape(1, d)]
    return pl.pallas_call(
        _rw_proj_kernel,
        out_shape=tuple(jax.ShapeDtypeStruct((n, d), F32) for _ in range(7)),
        grid=(pl.cdiv(n, tm),),
        in_specs=[big, big, full(p['rw_mu'])] + [full(w) for w in ws] + [vec] * 4,
        out_specs=tuple(big for _ in range(7)),
        compiler_params=_cparams("parallel"),
        name="rw_proj",
    )(hn, prev, p['rw_mu'], *ws, *vecs)


def _rw_scan_kernel(r_ref, lw_ref, k_ref, v_ref, kn_ref, b_ref, g_ref, s0_ref, rk_ref, lnw_ref, lnb_ref,
                    o_ref, s_ref, *, q, t_len):
    c = pl.program_id(1)

    @pl.when(c == 0)
    def _():
        s_ref[0] = s0_ref[0]

    rows = c * q + lax.broadcasted_iota(jnp.int32, (q, 1), 0)
    valid = rows < t_len
    incl = _tri(q)
    strict = _tri(q, strict=True)
    lw_all = jnp.where(valid, lw_ref[0], 0.0)
    cum_all = _dot_hi(incl.astype(F32), lw_all)
    ones = _head_ones(128, RW_HEAD_DIM)
    lo = lax.broadcasted_iota(jnp.int32, (q, 128), 1) < RW_HEAD_DIM
    blockdiag = ((lax.broadcasted_iota(jnp.int32, (128, 128), 0) < RW_HEAD_DIM)
                 == (lax.broadcasted_iota(jnp.int32, (128, 128), 1) < RW_HEAD_DIM))
    d = lw_all.shape[1]
    pairs = range(d // 128)
    sls = [slice(p * 128, (p + 1) * 128) for p in pairs]
    rs, ks, vs, bends, kends, s_old, cum_ends = [], [], [], [], [], [], []
    pk, pb, u0, y0 = [], [], [], []
    for p in pairs:
        sl = sls[p]
        lw = lw_all[:, sl]
        cum = cum_all[:, sl]
        cum_end = cum[q - 1:q, :]
        r = jnp.where(valid, r_ref[0, :, sl], 0.0)
        k = jnp.where(valid, k_ref[0, :, sl], 0.0)
        v = jnp.where(valid, v_ref[0, :, sl], 0.0)
        kn = jnp.where(valid, kn_ref[0, :, sl], 0.0)
        b = jnp.where(valid, b_ref[0, :, sl], 0.0)
        kq = kn * jnp.exp(cum - lw)
        rq = r * jnp.exp(cum)
        einv = jnp.exp(-cum)
        kd = k * einv
        bd = b * einv
        eend = jnp.exp(cum_end - cum)
        s2 = s_ref[0, sl, :]
        lhs = jnp.concatenate([jnp.where(lo, kq, 0.0), jnp.where(lo, 0.0, kq),
                               jnp.where(lo, rq, 0.0), jnp.where(lo, 0.0, rq)], axis=0)
        pk.append(_dot_nt(lhs, kd))
        pb.append(_dot_nt(lhs, bd))
        u0.append(_dot_nt(kq, s2))
        y0.append(_dot_nt(rq, s2))
        rs.append(r)
        ks.append(k)
        vs.append(v)
        kends.append(k * eend)
        bends.append(b * eend)
        s_old.append(s2)
        cum_ends.append(cum_end)
    ab = [jnp.where(strict, pb[p][e * q:(e + 1) * q], 0.0) for p in pairs for e in range(2)]
    rhs = [u0[p] + jnp.where(lo, _dot(jnp.where(strict, pk[p][0:q], 0.0), vs[p]),
                             _dot(jnp.where(strict, pk[p][q:2 * q], 0.0), vs[p])) for p in pairs]

    def per_head(ms, vecs, dot):
        return [jnp.where(lo, dot(ms[2 * p], vecs[p]), dot(ms[2 * p + 1], vecs[p])) for p in pairs]

    us = _solve_unit_lower(ab, rhs, per_head)
    ys = []
    for p in pairs:
        rk = [jnp.where(incl, pk[p][(2 + e) * q:(3 + e) * q], 0.0) for e in range(2)]
        rb = [jnp.where(incl, pb[p][(2 + e) * q:(3 + e) * q], 0.0) for e in range(2)]
        ys.append(y0[p] + jnp.where(lo, _dot(rk[0], vs[p]) - _dot(rb[0], us[p]),
                                    _dot(rk[1], vs[p]) - _dot(rb[1], us[p])))
    s_new = [s_old[p] * jnp.exp(cum_ends[p]) + _dot_tn(vs[p], kends[p]) - _dot_tn(us[p], bends[p])
             for p in pairs]
    for p in pairs:
        sl = sls[p]
        s_ref[0, sl, :] = jnp.where(blockdiag, s_new[p], 0.0)
        y = ys[p]
        mean = _head_sum(y, ones) * (1.0 / RW_HEAD_DIM)
        yc = y - mean
        var = _head_sum(yc * yc, ones) * (1.0 / RW_HEAD_DIM)
        yn = yc * lax.rsqrt(var + RW_GN_EPS) * lnw_ref[:, sl] + lnb_ref[:, sl]
        bonus = _head_sum(rs[p] * ks[p] * rk_ref[:, sl], ones) * vs[p]
        o_ref[0, :, sl] = ((yn + bonus) * g_ref[0, :, sl]).astype(o_ref.dtype)


def rw_scan(r, lw, k, v, kn, b, g, t_valid, s0, r_k, ln_w, ln_b, *, chunk=64):
    bsz, t_len, d = r.shape
    q = min(chunk, t_len)
    assert q % 8 == 0 and t_len % q == 0 and d % 128 == 0
    n_heads = d // RW_HEAD_DIM
    s5 = s0.reshape(bsz, n_heads // 2, 2, RW_HEAD_DIM, RW_HEAD_DIM)
    s2 = jnp.einsum('bpevk,ef->bpevfk', s5, jnp.eye(2, dtype=F32)).reshape(bsz, d, 128)
    blk = pl.BlockSpec((1, q, d), lambda bb, c: (bb, c, 0))
    vec = pl.BlockSpec((1, d), lambda bb, c: (0, 0))
    st = pl.BlockSpec((1, d, 128), lambda bb, c: (bb, 0, 0))
    o, s_new = pl.pallas_call(
        functools.partial(_rw_scan_kernel, q=q, t_len=t_valid),
        out_shape=(jax.ShapeDtypeStruct((bsz, t_len, d), BF16), jax.ShapeDtypeStruct((bsz, d, 128), F32)),
        grid=(bsz, t_len // q),
        in_specs=[blk] * 7 + [st, vec, vec, vec],
        out_specs=(blk, st),
        compiler_params=_cparams("parallel", "arbitrary"),
        name="rw_scan",
    )(r, lw, k, v, kn, b, g, s2, r_k.reshape(1, d), ln_w.reshape(1, d), ln_b.reshape(1, d))
    s6 = s_new.reshape(bsz, n_heads // 2, 2, RW_HEAD_DIM, 2, RW_HEAD_DIM)
    s_out = jnp.stack([s6[:, :, 0, :, 0, :], s6[:, :, 1, :, 1, :]], axis=2)
    return o, s_out.reshape(bsz, n_heads, RW_HEAD_DIM, RW_HEAD_DIM)


def rwkv_layer_padded(x, t_valid, shift_buf, s0, norm_g, p):
    bsz, t_len, d = x.shape
    x2 = x.reshape(bsz * t_len, d)
    hn = rmsnorm_rows(x2, norm_g).reshape(bsz, t_len, d)
    prev = jnp.concatenate([shift_buf, hn[:, :-1]], axis=1)
    outs = rw_proj(hn.reshape(-1, d), prev.reshape(-1, d), p)
    r, lw, k, v, kn, b, g = (a.reshape(bsz, t_len, d) for a in outs)
    yg, s_new = rw_scan(r, lw, k, v, kn, b, g, t_valid, s0, p['rw_r_k'], p['rw_ln_w'], p['rw_ln_b'])
    out = matmul_res(yg.reshape(bsz * t_len, d), p['rw_w_out'].astype(BF16), x2)
    return out.reshape(bsz, t_len, d), hn[:, t_valid - 1:t_valid], s_new


def _dot3_nt(a, b):
    ah, al = _split(a)
    bh, bl = _split(b)
    dn = (((1,), (1,)), ((), ()))
    f = lambda u, w: lax.dot_general(u, w, dn, preferred_element_type=F32)
    return f(ah, bh) + (f(ah, bl) + f(al, bh))


def _suffix_ones(n):
    r = lax.broadcasted_iota(jnp.int32, (n, n), 0)
    c = lax.broadcasted_iota(jnp.int32, (n, n), 1)
    return jnp.where(r >= c, 1.0, 0.0).astype(BF16)


def _sb_block(q_e, kblk, vblk, carry, acc, lo, suffix, mask):
    outs = []
    new_carry = []
    for e in range(2):
        z = _dot3_nt(q_e[e], kblk)
        lnb = -_softplus(z)
        if mask is not None:
            lnb = jnp.where(mask, lnb, 0.0)
        hi, lw = _split(lnb)
        rsum = _mm(hi, suffix) + _mm(lw, suffix) + carry[e]
        att = jnp.exp(z + rsum)
        if mask is not None:
            att = jnp.where(mask, att, 0.0)
        outs.append(_dot(att, vblk))
        new_carry.append(rsum[:, 0:1])
    return new_carry, acc + jnp.where(lo, outs[0], outs[1])


def _sb_kernel(q_ref, kn_ref, vn_ref, o_ref, kmax_ref, *, bq, scale):
    i = pl.program_id(2)
    t_new = kn_ref.shape[1]

    @pl.when(i == 0)
    def _():
        def body(j, m):
            rows = kn_ref[0, pl.ds(pl.multiple_of(j * bq, bq), bq), :]
            return jnp.maximum(m, jnp.max(jnp.abs(rows).reshape(bq // 8, 8, 128), axis=0))

        m8 = lax.fori_loop(0, t_new // bq, body, jnp.zeros((8, 128), F32))
        kmax_ref[...] = jnp.max(m8, axis=0, keepdims=True)

    lo = lax.broadcasted_iota(jnp.int32, (bq, 128), 1) < SB_HEAD_DIM
    q = q_ref[0] * scale
    q_e = [jnp.where(lo, q, 0.0), jnp.where(lo, 0.0, q)]
    zb = _head_sum(jnp.abs(q) * kmax_ref[...], _head_ones(128, SB_HEAD_DIM))
    zb0 = zb[:, 0:1]
    zb1 = zb[:, SB_HEAD_DIM:SB_HEAD_DIM + 1]
    suffix = _suffix_ones(bq)
    mask = _tri(bq, strict=True)
    zero_c = jnp.zeros((bq, 1), F32)
    start = pl.multiple_of(i * bq, bq)
    carry, acc = _sb_block(q_e, kn_ref[0, pl.ds(start, bq), :], vn_ref[0, pl.ds(start, bq), :],
                           [zero_c, zero_c], jnp.zeros((bq, 128), F32), lo, suffix, mask)

    def live(c0, c1):
        return (jnp.max(jnp.maximum(c0 + zb0, c1 + zb1)) > SB_LOG_CUT).astype(jnp.int32)

    def walk(k_ref, v_ref, n_blocks, blk, sfx, st):
        def cond(s):
            return (s[0] < n_blocks) & (s[4] > 0)

        def body(s):
            jj, c0, c1, a, _ = s
            off = pl.multiple_of((n_blocks - 1 - jj) * blk, blk)
            (c0, c1), a = _sb_block(q_e, k_ref[0, pl.ds(off, blk), :], v_ref[0, pl.ds(off, blk), :],
                                    [c0, c1], a, lo, sfx, None)
            return jj + 1, c0, c1, a, live(c0, c1)

        return lax.while_loop(cond, body, st)

    st = walk(kn_ref, vn_ref, i, bq, suffix, (jnp.int32(0), carry[0], carry[1], acc, live(*carry)))
    o_ref[0] = st[3].astype(o_ref.dtype)


def sb_attention(qkv, *, block=256):
    bsz, t_len, d3 = qkv.shape
    d = d3 // 3
    assert d % 128 == 0
    n_pairs = d // 128
    bq = min(block, t_len)
    assert t_len % bq == 0 and bq % 8 == 0
    return pl.pallas_call(
        functools.partial(_sb_kernel, bq=bq, scale=SB_HEAD_DIM ** -0.5),
        out_shape=jax.ShapeDtypeStruct((bsz, t_len, d), BF16),
        grid=(bsz, n_pairs, t_len // bq),
        in_specs=[pl.BlockSpec((1, bq, 128), lambda b, p, i: (b, i, p)),
                  pl.BlockSpec((1, t_len, 128), lambda b, p, i: (b, 0, n_pairs + p)),
                  pl.BlockSpec((1, t_len, 128), lambda b, p, i: (b, 0, 2 * n_pairs + p))],
        out_specs=pl.BlockSpec((1, bq, 128), lambda b, p, i: (b, i, p)),
        compiler_params=_cparams("parallel", "parallel", "arbitrary"),
        scratch_shapes=[pltpu.VMEM((1, 128), F32)],
        name="sb_attention",
    )(qkv, qkv, qkv)


def _sb_decode_kernel(qkv_ref, kc_ref, vc_ref, o_ref, q_s, acc_s, carry_s, *, t, n_heads, bp, scale):
    j = pl.program_id(1)
    hd = SB_HEAD_DIM
    d = n_heads * hd
    rows = n_heads * t
    heads = range(n_heads)

    def scores(keys):
        return jnp.concatenate([_dot3_nt(q_s[h * t:(h + 1) * t, :], keys(h)) for h in heads], axis=0)

    def weights(z, carry, suffix, mask):
        lnb = -_softplus(z)
        if mask is not None:
            lnb = jnp.where(mask, lnb, 0.0)
        hi, lw = _split(lnb)
        rsum = _mm(hi, suffix) + _mm(lw, suffix) + carry
        att = jnp.exp(z + rsum)
        if mask is not None:
            att = jnp.where(mask, att, 0.0)
        return att, rsum[:, 0:1]

    @pl.when(j == 0)
    def _():
        for h in heads:
            q_s[h * t:(h + 1) * t, :] = qkv_ref[0, :, h * hd:(h + 1) * hd] * scale
        z = scores(lambda h: qkv_ref[0, :, d + h * hd:d + (h + 1) * hd])
        qi = lax.rem(lax.broadcasted_iota(jnp.int32, (rows, t), 0), t)
        mask = lax.broadcasted_iota(jnp.int32, (rows, t), 1) < qi
        att, carry = weights(z, 0.0, _suffix_ones(t), mask)
        for h in heads:
            acc_s[h * t:(h + 1) * t, :] = _dot(att[h * t:(h + 1) * t],
                                               qkv_ref[0, :, 2 * d + h * hd:2 * d + (h + 1) * hd])
        carry_s[...] = carry

    ub = jnp.concatenate([_dot(jnp.abs(q_s[h * t:(h + 1) * t, :]), jnp.abs(kc_ref[0, h])) for h in heads], axis=0)
    zb = SB_BOUND_SLACK * jnp.max(ub, axis=-1, keepdims=True)

    @pl.when(jnp.max(carry_s[...] + zb) > SB_LOG_CUT)
    def _():
        z = jnp.concatenate([_dot3(q_s[h * t:(h + 1) * t, :], kc_ref[0, h]) for h in heads], axis=0)
        att, carry = weights(z, carry_s[...], _suffix_ones(bp), None)
        for h in heads:
            acc_s[h * t:(h + 1) * t, :] += _dot_nt(att[h * t:(h + 1) * t], vc_ref[0, h])
        carry_s[...] = carry

    @pl.when(j == pl.num_programs(1) - 1)
    def _():
        o_ref[0] = jnp.concatenate([acc_s[h * t:(h + 1) * t, :] for h in heads], axis=1).astype(o_ref.dtype)


def sb_decode(qkv, k_cache, v_cache, *, block=512):
    bsz, t_len, d3 = qkv.shape
    _, past_len, n_heads, hd = k_cache.shape
    assert hd == SB_HEAD_DIM and n_heads * hd * 3 == d3 and t_len % 8 == 0
    bp = min(block, past_len)
    assert past_len % bp == 0
    n_blk = past_len // bp
    k_cache = jnp.transpose(k_cache, (0, 2, 3, 1))
    v_cache = jnp.transpose(v_cache, (0, 2, 3, 1))
    cache_spec = pl.BlockSpec((1, n_heads, hd, bp), lambda b, j: (b, 0, 0, n_blk - 1 - j))
    return pl.pallas_call(
        functools.partial(_sb_decode_kernel, t=t_len, n_heads=n_heads, bp=bp, scale=hd ** -0.5),
        out_shape=jax.ShapeDtypeStruct((bsz, t_len, n_heads * hd), BF16),
        grid=(bsz, n_blk),
        in_specs=[pl.BlockSpec((1, t_len, d3), lambda b, j: (b, 0, 0)), cache_spec, cache_spec],
        out_specs=pl.BlockSpec((1, t_len, n_heads * hd), lambda b, j: (b, 0, 0)),
        scratch_shapes=[pltpu.VMEM((n_heads * t_len, hd), F32), pltpu.VMEM((n_heads * t_len, hd), F32),
                        pltpu.VMEM((n_heads * t_len, 1), F32)],
        compiler_params=_cparams("parallel", "arbitrary"),
        name="sb_decode",
    )(qkv, k_cache, v_cache)


def sb_layer_padded(x, t_valid, k_past, v_past, norm_g, w_qkv, w_out):
    bsz, t_len, d = x.shape
    x2 = x.reshape(bsz * t_len, d)
    qkv = norm_matmul(x2, norm_g, w_qkv).reshape(bsz, t_len, 3 * d)
    o = sb_attention(qkv) if k_past is None else sb_decode(qkv, k_past, v_past)
    out = matmul_res(o.reshape(bsz * t_len, d), w_out, x2)
    n_heads = d // SB_HEAD_DIM
    k_new = qkv[:, :t_valid, d:2 * d].reshape(bsz, t_valid, n_heads, SB_HEAD_DIM)
    v_new = qkv[:, :t_valid, 2 * d:].reshape(bsz, t_valid, n_heads, SB_HEAD_DIM)
    return out.reshape(bsz, t_len, d), k_new, v_new


def _run_trunk(x, t_valid, st, p):
    bsz, t_len, d = x.shape
    new = {}
    bf = lambda a: a.astype(BF16)

    def ffn_layer(x, i):
        return ffn(x.reshape(bsz * t_len, d), p['norm_ffn'][i], bf(p['ffn_w_gu'][i]),
                   bf(p['ffn_w_down'][i])).reshape(bsz, t_len, d)

    x, new['ssm_conv'], new['ssm'] = mamba2_layer_padded(
        x, t_valid, st['ssm_conv'], st['ssm'], p['norm_mix'][0], bf(p['mb_w_in']), p['mb_conv_w'],
        p['mb_conv_b'], p['mb_dt_bias'], p['mb_a_log'], p['mb_d'], p['mb_norm'], bf(p['mb_w_out']))
    x = ffn_layer(x, 0)
    x, new['gdn_conv'], new['gdn'] = gdn_layer_padded(
        x, t_valid, st['gdn_conv'], st['gdn'], p['norm_mix'][1], bf(p['gdn_w_in']), p['gdn_conv_w'],
        p['gdn_dt_bias'], p['gdn_a_log'], p['gdn_norm'], bf(p['gdn_w_out']))
    x = ffn_layer(x, 1)
    x, new['rwkv_shift'], new['rwkv'] = rwkv_layer_padded(
        x, t_valid, st['rwkv_shift'], st['rwkv'], p['norm_mix'][2], p)
    x = ffn_layer(x, 2)
    x, new['sb_k'], new['sb_v'] = sb_layer_padded(
        x, t_valid, st['sb_k'], st['sb_v'], p['norm_mix'][3], bf(p['sb_w_qkv']), bf(p['sb_w_out']))
    x = ffn_layer(x, 3)
    y = rmsnorm_rows(x.reshape(bsz * t_len, d), p['norm_final']).reshape(bsz, t_len, d)
    return y, new


def kernel(x_prompt, x_sample, state_ssm, state_ssm_conv, state_gdn, state_gdn_conv, state_rwkv, state_rwkv_shift, cache_sb_k, cache_sb_v, meta_tokens, norm_mix, norm_ffn, norm_final, ffn_w_gu, ffn_w_down, mb_w_in, mb_conv_w, mb_conv_b, mb_dt_bias, mb_a_log, mb_d, mb_norm, mb_w_out, gdn_w_in, gdn_conv_w, gdn_dt_bias, gdn_a_log, gdn_norm, gdn_w_out, rw_mu, rw_w_rkv, rw_w0, rw_w1, rw_w2, rw_a0, rw_a1, rw_a2, rw_g1, rw_g2, rw_k_k, rw_k_a, rw_r_k, rw_ln_w, rw_ln_b, rw_w_out, sb_w_qkv, sb_w_out):
    p = dict(
        norm_mix=norm_mix, norm_ffn=norm_ffn, norm_final=norm_final, ffn_w_gu=ffn_w_gu, ffn_w_down=ffn_w_down,
        mb_w_in=mb_w_in, mb_conv_w=mb_conv_w, mb_conv_b=mb_conv_b, mb_dt_bias=mb_dt_bias, mb_a_log=mb_a_log,
        mb_d=mb_d, mb_norm=mb_norm, mb_w_out=mb_w_out,
        gdn_w_in=gdn_w_in, gdn_conv_w=gdn_conv_w, gdn_dt_bias=gdn_dt_bias, gdn_a_log=gdn_a_log,
        gdn_norm=gdn_norm, gdn_w_out=gdn_w_out,
        rw_mu=rw_mu, rw_w_rkv=rw_w_rkv, rw_w0=rw_w0, rw_w1=rw_w1, rw_w2=rw_w2, rw_a0=rw_a0, rw_a1=rw_a1,
        rw_a2=rw_a2, rw_g1=rw_g1, rw_g2=rw_g2, rw_k_k=rw_k_k, rw_k_a=rw_k_a, rw_r_k=rw_r_k, rw_ln_w=rw_ln_w,
        rw_ln_b=rw_ln_b, rw_w_out=rw_w_out, sb_w_qkv=sb_w_qkv, sb_w_out=sb_w_out)
    bsz, seq, d = x_prompt.shape
    n_meta = meta_tokens.shape[0]
    t_valid = n_meta + seq
    t_pad = -(-t_valid // PROMPT_ROW_ALIGN) * PROMPT_ROW_ALIGN
    meta = jnp.broadcast_to(meta_tokens[None], (bsz, n_meta, d))
    x0 = jnp.concatenate([meta, x_prompt, jnp.zeros((bsz, t_pad - t_valid, d), x_prompt.dtype)], axis=1)
    fresh = dict(
        ssm=jnp.zeros((bsz,) + state_ssm.shape[1:], F32), ssm_conv=jnp.zeros((bsz,) + state_ssm_conv.shape[1:], F32),
        gdn=jnp.zeros((bsz,) + state_gdn.shape[1:], F32), gdn_conv=jnp.zeros((bsz,) + state_gdn_conv.shape[1:], F32),
        rwkv=jnp.zeros((bsz,) + state_rwkv.shape[1:], F32),
        rwkv_shift=jnp.zeros((bsz,) + state_rwkv_shift.shape[1:], F32), sb_k=None, sb_v=None)
    y_full, sp = _run_trunk(x0, t_valid, fresh, p)
    y_prompt = y_full[:, n_meta:t_valid]
    past = dict(ssm=state_ssm, ssm_conv=state_ssm_conv, gdn=state_gdn, gdn_conv=state_gdn_conv,
                rwkv=state_rwkv, rwkv_shift=state_rwkv_shift, sb_k=cache_sb_k, sb_v=cache_sb_v)
    y_sample, ss = _run_trunk(x_sample, x_sample.shape[1], past, p)
    return (y_prompt, y_sample,
            sp['ssm'], sp['ssm_conv'], sp['gdn'], sp['gdn_conv'], sp['rwkv'], sp['rwkv_shift'],
            sp['sb_k'], sp['sb_v'],
            ss['ssm'], ss['ssm_conv'], ss['gdn'], ss['gdn_conv'], ss['rwkv'], ss['rwkv_shift'],
            ss['sb_k'], ss['sb_v'])
```

```python
import functools
import math

import jax
import jax.numpy as jnp
from jax import lax
from jax.experimental import pallas as pl
from jax.experimental.pallas import tpu as pltpu

F32 = jnp.float32
BF16 = jnp.bfloat16
EPS = 1e-6
VMEM_LIMIT = 48 * 1024 * 1024
HI = lax.Precision.HIGHEST

PROMPT_ROW_ALIGN = 256
MB_HEAD_DIM = 64
MB_D_STATE = 128
MB_GROUPS = 4
GDN_DK = 128
GDN_DV = 128
RW_HEAD_DIM = 64
RW_GN_EPS = 64e-5
RW_DECAY_SCALE = 0.6065306597126334
SB_HEAD_DIM = 64
SB_LOG_CUT = -100.0
SB_BOUND_SLACK = 1.01


def _cparams(*sem):
    return pltpu.CompilerParams(dimension_semantics=sem, vmem_limit_bytes=VMEM_LIMIT)


def _dot(a, b):
    return jnp.dot(a.astype(BF16), b.astype(BF16), preferred_element_type=F32)


def _dot_nt(a, b):
    return lax.dot_general(a.astype(BF16), b.astype(BF16), (((1,), (1,)), ((), ())),
                           preferred_element_type=F32)


def _dot_tn(a, b):
    return lax.dot_general(a.astype(BF16), b.astype(BF16), (((0,), (0,)), ((), ())),
                           preferred_element_type=F32)


def _dot_hi(a, b):
    return jnp.dot(a, b, preferred_element_type=F32, precision=HI)


def _silu(x):
    return x * jax.nn.sigmoid(x)


def _softplus(x):
    return jnp.maximum(x, 0.0) + jnp.log(1.0 + jnp.exp(-jnp.abs(x)))


def _rmsnorm(x, g):
    return x * lax.rsqrt(jnp.mean(x * x, axis=-1, keepdims=True) + EPS) * g


def _row_tile(n, want):
    return n if n <= want else want


def _resident(shape):
    zeros = (0,) * len(shape)
    return pl.BlockSpec(shape, lambda *_: zeros, pipeline_mode=pl.Buffered(1))


def _norm_matmul_kernel(x_ref, g_ref, w_ref, o_ref):
    h = _rmsnorm(x_ref[...], g_ref[...]).astype(BF16)
    o_ref[...] = jnp.dot(h, w_ref[...], preferred_element_type=F32)


def norm_matmul(x, g, w, *, tm=256):
    n, d = x.shape
    m = w.shape[1]
    tm = _row_tile(n, tm)
    return pl.pallas_call(
        _norm_matmul_kernel,
        out_shape=jax.ShapeDtypeStruct((n, m), F32),
        grid=(pl.cdiv(n, tm),),
        in_specs=[pl.BlockSpec((tm, d), lambda i: (i, 0)), _resident((1, d)), _resident((d, m))],
        out_specs=pl.BlockSpec((tm, m), lambda i: (i, 0)),
        compiler_params=_cparams("parallel"),
        name="norm_matmul",
    )(x, g.reshape(1, d), w)


def _matmul_res_kernel(a_ref, w_ref, r_ref, o_ref):
    o_ref[...] = r_ref[...] + jnp.dot(a_ref[...].astype(BF16), w_ref[...], preferred_element_type=F32)


def matmul_res(a, w, res, *, tm=512):
    n, k = a.shape
    d = w.shape[1]
    tm = _row_tile(n, tm)
    return pl.pallas_call(
        _matmul_res_kernel,
        out_shape=jax.ShapeDtypeStruct((n, d), F32),
        grid=(pl.cdiv(n, tm),),
        in_specs=[pl.BlockSpec((tm, k), lambda i: (i, 0)),
                  pl.BlockSpec((k, d), lambda i: (0, 0)),
                  pl.BlockSpec((tm, d), lambda i: (i, 0))],
        out_specs=pl.BlockSpec((tm, d), lambda i: (i, 0)),
        compiler_params=_cparams("parallel"),
        name="matmul_res",
    )(a, w, res)


def _ffn_kernel(x_ref, g_ref, wgu_ref, wd_ref, o_ref, *, f, tf):
    x = x_ref[...]
    h = _rmsnorm(x, g_ref[...]).astype(BF16)
    acc = x
    for c in range(f // tf):
        gate = jnp.dot(h, wgu_ref[:, c * tf:(c + 1) * tf], preferred_element_type=F32)
        up = jnp.dot(h, wgu_ref[:, f + c * tf:f + (c + 1) * tf], preferred_element_type=F32)
        act = (_silu(gate) * up).astype(BF16)
        acc = acc + jnp.dot(act, wd_ref[c * tf:(c + 1) * tf, :], preferred_element_type=F32)
    o_ref[...] = acc


def ffn(x, g, w_gu, w_down, *, tm=512, tf=1408):
    n, d = x.shape
    f = w_down.shape[0]
    tm = _row_tile(n, tm)
    if f % tf:
        tf = f
    return pl.pallas_call(
        functools.partial(_ffn_kernel, f=f, tf=tf),
        out_shape=jax.ShapeDtypeStruct((n, d), F32),
        grid=(pl.cdiv(n, tm),),
        in_specs=[pl.BlockSpec((tm, d), lambda i: (i, 0)), _resident((1, d)), _resident((d, 2 * f)),
                  _resident((f, d))],
        out_specs=pl.BlockSpec((tm, d), lambda i: (i, 0)),
        compiler_params=_cparams("parallel"),
        name="ffn",
    )(x, g.reshape(1, d), w_gu, w_down)


def _rmsnorm_kernel(x_ref, g_ref, o_ref):
    o_ref[...] = _rmsnorm(x_ref[...], g_ref[...])


def rmsnorm_rows(x, g, *, tm=1024):
    n, d = x.shape
    tm = _row_tile(n, tm)
    return pl.pallas_call(
        _rmsnorm_kernel,
        out_shape=jax.ShapeDtypeStruct((n, d), F32),
        grid=(pl.cdiv(n, tm),),
        in_specs=[pl.BlockSpec((tm, d), lambda i: (i, 0)),
                  pl.BlockSpec((1, d), lambda i: (0, 0))],
        out_specs=pl.BlockSpec((tm, d), lambda i: (i, 0)),
        compiler_params=_cparams("parallel"),
        name="rmsnorm",
    )(x, g.reshape(1, d))


def _causal_conv_silu(raw, buf_ref, w_ref, bias, q):
    buf_ref[pl.ds(8, q), :] = raw
    out = buf_ref[pl.ds(5, q), :] * w_ref[0:1, :]
    out = out + buf_ref[pl.ds(6, q), :] * w_ref[1:2, :]
    out = out + buf_ref[pl.ds(7, q), :] * w_ref[2:3, :]
    out = out + raw * w_ref[3:4, :]
    if bias is not None:
        out = out + bias
    buf_ref[pl.ds(5, 3), :] = buf_ref[pl.ds(q + 5, 3), :]
    return _silu(out)


def _tri(q, strict=False):
    r = lax.broadcasted_iota(jnp.int32, (q, q), 0)
    c = lax.broadcasted_iota(jnp.int32, (q, q), 1)
    return (r > c) if strict else (r >= c)


def _mamba_kernel(z_ref, xs_ref, bc_ref, dt_ref, cx_ref, cbc_ref, h0_ref,
                  cwx_ref, cwbc_ref, cbx_ref, cbbc_ref, dtb_ref, a_ref, dskip_ref, nw_ref,
                  y_ref, h_ref, bufx_ref, bufbc_ref, *, q, t_len, n_heads):
    c = pl.program_id(1)
    n_state = MB_D_STATE
    hpg = n_heads // MB_GROUPS
    gw = hpg * MB_HEAD_DIM

    @pl.when(c == 0)
    def _():
        bufx_ref[pl.ds(5, 3), :] = cx_ref[0]
        bufbc_ref[pl.ds(5, 3), :] = cbc_ref[0]
        h_ref[0] = h0_ref[0]

    rows = c * q + lax.broadcasted_iota(jnp.int32, (q, 1), 0)
    valid = rows < t_len
    xs = _causal_conv_silu(jnp.where(valid, xs_ref[0], 0.0), bufx_ref, cwx_ref, cbx_ref[...], q)
    bc = _causal_conv_silu(jnp.where(valid, bc_ref[0], 0.0), bufbc_ref, cwbc_ref, cbbc_ref[...], q)

    lane = lax.broadcasted_iota(jnp.int32, (q, 128), 1)
    dt = _softplus(jnp.where(valid & (lane < n_heads), dt_ref[0], 0.0) + dtb_ref[...])
    dt = jnp.where(valid & (lane < n_heads), dt, 0.0)
    da = dt * a_ref[...]
    incl = _tri(q)
    acum = _dot_hi(incl.astype(F32), da)
    acum_t = acum.T
    dt_t = dt.T
    last = acum[q - 1:q, :]
    wts = jnp.exp(last - acum) * dt
    e_acum = jnp.exp(acum)
    e_last = jnp.exp(last)

    lo_half = lax.broadcasted_iota(jnp.int32, (q, 128), 1) < MB_HEAD_DIM
    lo_rows = lax.broadcasted_iota(jnp.int32, (128, 128), 0) < MB_HEAD_DIM

    z = z_ref[0]
    for g in range(MB_GROUPS):
        bm = bc[:, g * n_state:(g + 1) * n_state]
        cm = bc[:, (MB_GROUPS + g) * n_state:(MB_GROUPS + g + 1) * n_state]
        cb = _dot_nt(cm, bm)
        hg = h_ref[0, g * gw:(g + 1) * gw, :]
        ch = _dot_nt(cm, hg)
        ys = []
        for p in range(hpg // 2):
            h_e = g * hpg + 2 * p
            x_pair = xs[:, h_e * MB_HEAD_DIM:(h_e + 2) * MB_HEAD_DIM]
            outs = []
            for hh in (h_e, h_e + 1):
                seg = acum[:, hh:hh + 1] - acum_t[hh:hh + 1, :]
                decay = jnp.where(incl, jnp.exp(jnp.where(incl, seg, 0.0)), 0.0)
                m = cb * decay * dt_t[hh:hh + 1, :]
                outs.append(_dot(m, x_pair))
            y_pair = jnp.where(lo_half, outs[0], outs[1])
            e_pair = jnp.where(lo_half, e_acum[:, h_e:h_e + 1], e_acum[:, h_e + 1:h_e + 2])
            y_pair = y_pair + e_pair * ch[:, 2 * p * MB_HEAD_DIM:(2 * p + 2) * MB_HEAD_DIM]
            ys.append(y_pair)
            w_pair = jnp.where(lo_half, wts[:, h_e:h_e + 1], wts[:, h_e + 1:h_e + 2])
            upd = _dot_tn(x_pair * w_pair, bm)
            scale = jnp.where(lo_rows, e_last[:, h_e:h_e + 1], e_last[:, h_e + 1:h_e + 2])
            r0 = h_e * MB_HEAD_DIM
            h_ref[0, r0:r0 + 128, :] = h_ref[0, r0:r0 + 128, :] * scale + upd
        yg = jnp.concatenate(ys, axis=1)
        xg = xs[:, g * gw:(g + 1) * gw]
        yg = yg + xg * dskip_ref[:, g * gw:(g + 1) * gw]
        yg = yg * _silu(z[:, g * gw:(g + 1) * gw])
        yg = yg * lax.rsqrt(jnp.mean(yg * yg, axis=-1, keepdims=True) + EPS)
        y_ref[0, :, g * gw:(g + 1) * gw] = (yg * nw_ref[:, g * gw:(g + 1) * gw]).astype(y_ref.dtype)


def mamba_core(proj, t_valid, conv_buf, h0, conv_w, conv_b, dt_bias, a_log, d_skip, norm_w, *, chunk=128):
    bsz, t_len, _ = proj.shape
    n_heads = dt_bias.shape[0]
    d_inner = n_heads * MB_HEAD_DIM
    gn = MB_GROUPS * MB_D_STATE
    assert d_inner % 1024 == 0 and 2 * gn == 1024 and n_heads <= 128
    q = min(chunk, t_len)
    assert q % 8 == 0 and t_len % q == 0
    n_chunks = t_len // q
    dtb = jnp.zeros((1, 128), F32).at[0, :n_heads].set(dt_bias)
    a_row = jnp.zeros((1, 128), F32).at[0, :n_heads].set(-jnp.exp(a_log))
    dskip = jnp.repeat(d_skip, MB_HEAD_DIM).reshape(1, d_inner)
    kern = functools.partial(_mamba_kernel, q=q, t_len=t_valid, n_heads=n_heads)
    xblk = d_inner // 1024
    y, h = pl.pallas_call(
        kern,
        out_shape=(jax.ShapeDtypeStruct((bsz, t_len, d_inner), BF16),
                   jax.ShapeDtypeStruct((bsz, n_heads * MB_HEAD_DIM, MB_D_STATE), F32)),
        grid=(bsz, n_chunks),
        in_specs=[
            pl.BlockSpec((1, q, d_inner), lambda b, c: (b, c, 0)),
            pl.BlockSpec((1, q, d_inner), lambda b, c: (b, c, 1)),
            pl.BlockSpec((1, q, 2 * gn), lambda b, c: (b, c, 2 * xblk)),
            pl.BlockSpec((1, q, 128), lambda b, c: (b, c, (2 * d_inner + 2 * gn) // 128)),
            pl.BlockSpec((1, 3, d_inner), lambda b, c: (b, 0, 0)),
            pl.BlockSpec((1, 3, 2 * gn), lambda b, c: (b, 0, xblk)),
            pl.BlockSpec((1, n_heads * MB_HEAD_DIM, MB_D_STATE), lambda b, c: (b, 0, 0)),
            pl.BlockSpec((4, d_inner), lambda b, c: (0, 0)),
            pl.BlockSpec((4, 2 * gn), lambda b, c: (0, xblk)),
            pl.BlockSpec((1, d_inner), lambda b, c: (0, 0)),
            pl.BlockSpec((1, 2 * gn), lambda b, c: (0, xblk)),
            _resident((1, 128)), _resident((1, 128)), _resident((1, d_inner)), _resident((1, d_inner)),
        ],
        out_specs=(pl.BlockSpec((1, q, d_inner), lambda b, c: (b, c, 0)),
                   pl.BlockSpec((1, n_heads * MB_HEAD_DIM, MB_D_STATE), lambda b, c: (b, 0, 0))),
        scratch_shapes=[pltpu.VMEM((q + 8, d_inner), F32), pltpu.VMEM((q + 8, 2 * gn), F32)],
        compiler_params=_cparams("parallel", "arbitrary"),
        name="mamba_core",
    )(proj, proj, proj, proj, conv_buf, conv_buf, h0.reshape(bsz, n_heads * MB_HEAD_DIM, MB_D_STATE),
      conv_w, conv_w, conv_b.reshape(1, -1), conv_b.reshape(1, -1), dtb, a_row, dskip,
      norm_w.reshape(1, d_inner))
    return y, h.reshape(bsz, n_heads, MB_HEAD_DIM, MB_D_STATE)


def mamba2_layer_padded(x, t_valid, conv_buf, ssm_state, norm_g, w_in, conv_w, conv_b, dt_bias, a_log,
                        d_skip, norm_w, w_out):
    bsz, t_len, d = x.shape
    d_inner = dt_bias.shape[0] * MB_HEAD_DIM
    x2 = x.reshape(bsz * t_len, d)
    proj = norm_matmul(x2, norm_g, w_in).reshape(bsz, t_len, -1)
    y, h_new = mamba_core(proj, t_valid, conv_buf, ssm_state, conv_w, conv_b, dt_bias, a_log, d_skip, norm_w)
    conv_new = proj[:, t_valid - 3:t_valid, d_inner:d_inner + conv_w.shape[1]]
    out = matmul_res(y.reshape(bsz * t_len, d_inner), w_out, x2)
    return out.reshape(bsz, t_len, d), conv_new, h_new


def _split(a):
    hi = a.astype(BF16)
    lo = (a - hi.astype(F32)).astype(BF16)
    return hi, lo


def _mm(a, b):
    return jnp.dot(a, b, preferred_element_type=F32)


def _dot3(a, b):
    ah, al = _split(a)
    bh, bl = _split(b)
    return _mm(ah, bh) + (_mm(ah, bl) + _mm(al, bh))


INV_BASE = 8


def _unit_lower_inverse(mats):
    n = mats[0].shape[0]
    r = lax.broadcasted_iota(jnp.int32, (n, n), 0)
    c = lax.broadcasted_iota(jnp.int32, (n, n), 1)
    eye = jnp.where(r == c, 1.0, 0.0)
    same = (r // INV_BASE) == (c // INV_BASE)
    diag = [jnp.where(same, a, 0.0) for a in mats]
    xs = [eye - dg for dg in diag]
    ps = diag
    k = 1
    while 2 * k < INV_BASE:
        ps = [_dot(p, p) for p in ps]
        xs = [x + _dot(x, p) for x, p in zip(xs, ps)]
        k *= 2
    w = INV_BASE
    while w < n:
        off = ((r // (2 * w)) == (c // (2 * w))) != ((r // w) == (c // w))
        ts = [_dot(jnp.where(off, a, 0.0), x) for a, x in zip(mats, xs)]
        xs = [x - _dot(x, t) for x, t in zip(xs, ts)]
        w *= 2
    return xs


def _solve_unit_lower(mats, rhs, apply):
    x_inv = _unit_lower_inverse(mats)
    d0 = apply(x_inv, rhs, _dot)
    t0 = apply(mats, d0, _dot3)
    res = [b - d - t for b, d, t in zip(rhs, d0, t0)]
    corr = apply(x_inv, res, _dot)
    return [d + e for d, e in zip(d0, corr)]


def _gdn_kernel(q_ref, k_ref, v_ref, z_ref, ba_ref, cq_ref, ck_ref, cv_ref, s0_ref,
                cwq_ref, cwk_ref, cwv_ref, dtb_ref, al_ref, nw_ref,
                o_ref, s_ref, bufq_ref, bufk_ref, bufv_ref, *, q, t_len, n_hv, n_hk):
    c = pl.program_id(1)
    rep = n_hv // n_hk

    @pl.when(c == 0)
    def _():
        bufq_ref[pl.ds(5, 3), :] = cq_ref[0]
        bufk_ref[pl.ds(5, 3), :] = ck_ref[0]
        bufv_ref[pl.ds(5, 3), :] = cv_ref[0]
        s_ref[0] = s0_ref[0]

    rows = c * q + lax.broadcasted_iota(jnp.int32, (q, 1), 0)
    valid = rows < t_len
    qc = _causal_conv_silu(jnp.where(valid, q_ref[0], 0.0), bufq_ref, cwq_ref, None, q)
    kc = _causal_conv_silu(jnp.where(valid, k_ref[0], 0.0), bufk_ref, cwk_ref, None, q)
    vc = _causal_conv_silu(jnp.where(valid, v_ref[0], 0.0), bufv_ref, cwv_ref, None, q)

    lane = lax.broadcasted_iota(jnp.int32, (q, 128), 1)
    ba = jnp.where(valid & (lane < 2 * n_hv), ba_ref[0], 0.0)
    beta = jnp.where(valid & (lane < n_hv), jax.nn.sigmoid(ba), 0.0)
    g = jnp.where(valid & (lane >= n_hv) & (lane < 2 * n_hv),
                  -jnp.exp(al_ref[...]) * _softplus(ba + dtb_ref[...]), 0.0)
    incl = _tri(q)
    strict = _tri(q, strict=True)
    gcum = _dot_hi(incl.astype(F32), g)
    gcum_t = gcum.T
    eg = jnp.exp(gcum)
    glast = gcum[q - 1:q, :]
    w_last = jnp.exp(glast - gcum)
    e_last = jnp.exp(glast)

    z = z_ref[0]
    qn, kn, kk, qk0 = [], [], [], []
    for kh in range(n_hk):
        qh = qc[:, kh * GDN_DK:(kh + 1) * GDN_DK]
        kh_ = kc[:, kh * GDN_DK:(kh + 1) * GDN_DK]
        qh = qh * (lax.rsqrt(jnp.sum(qh * qh, axis=-1, keepdims=True) + 1e-6) * (GDN_DK ** -0.5))
        kh_ = kh_ * lax.rsqrt(jnp.sum(kh_ * kh_, axis=-1, keepdims=True) + 1e-6)
        qn.append(qh)
        kn.append(kh_)
        kk.append(_dot_nt(kh_, kh_))
        qk0.append(_dot_nt(qh, kh_))
    heads = range(n_hv)
    s_old = [s_ref[0, h * GDN_DK:(h + 1) * GDN_DK, :] for h in heads]
    ks = [_dot(kn[h // rep], s_old[h]) for h in heads]
    qs = [_dot(qn[h // rep], s_old[h]) for h in heads]
    dec, a_mat, rhs = [], [], []
    for h in heads:
        hl = n_hv + h
        seg = gcum[:, hl:hl + 1] - gcum_t[hl:hl + 1, :]
        dec.append(jnp.where(incl, jnp.exp(jnp.where(incl, seg, 0.0)), 0.0))
        bt = beta[:, h:h + 1]
        a_mat.append(jnp.where(strict, kk[h // rep] * dec[h], 0.0) * bt)
        rhs.append(bt * (vc[:, h * GDN_DV:(h + 1) * GDN_DV] - eg[:, hl:hl + 1] * ks[h]))
    delta = _solve_unit_lower(a_mat, rhs, lambda ms, vs, dot: [dot(m, v) for m, v in zip(ms, vs)])
    o_mm = [_dot(qk0[h // rep] * dec[h], delta[h]) for h in heads]
    upd = [_dot_tn(kn[h // rep] * w_last[:, n_hv + h:n_hv + h + 1], delta[h]) for h in heads]
    for h in heads:
        hl = n_hv + h
        s_ref[0, h * GDN_DK:(h + 1) * GDN_DK, :] = s_old[h] * e_last[:, hl:hl + 1] + upd[h]
        o = eg[:, hl:hl + 1] * qs[h] + o_mm[h]
        o = o * lax.rsqrt(jnp.mean(o * o, axis=-1, keepdims=True) + EPS) * nw_ref[...]
        o = o * _silu(z[:, h * GDN_DV:(h + 1) * GDN_DV])
        o_ref[0, :, h * GDN_DV:(h + 1) * GDN_DV] = o.astype(o_ref.dtype)


def gdn_core(proj, t_valid, conv_buf, s0, conv_w, dt_bias, a_log, norm_w, *, chunk=64):
    bsz, t_len, width = proj.shape
    n_hv = dt_bias.shape[0]
    val_dim = n_hv * GDN_DV
    key_dim = (width - 2 * n_hv - 2 * val_dim) // 2
    n_hk = key_dim // GDN_DK
    assert val_dim == 2 * key_dim and key_dim % 128 == 0 and 2 * n_hv <= 128
    q = min(chunk, t_len)
    assert q % 8 == 0 and t_len % q == 0
    n_chunks = t_len // q
    dtb = jnp.zeros((1, 128), F32).at[0, n_hv:2 * n_hv].set(dt_bias)
    alog = jnp.zeros((1, 128), F32).at[0, n_hv:2 * n_hv].set(a_log)
    kern = functools.partial(_gdn_kernel, q=q, t_len=t_valid, n_hv=n_hv, n_hk=n_hk)
    ba_blk = (2 * key_dim + 2 * val_dim) // 128
    s0 = s0.reshape(bsz, n_hv * GDN_DK, GDN_DV)
    o, s = pl.pallas_call(
        kern,
        out_shape=(jax.ShapeDtypeStruct((bsz, t_len, val_dim), BF16),
                   jax.ShapeDtypeStruct(s0.shape, F32)),
        grid=(bsz, n_chunks),
        in_specs=[
            pl.BlockSpec((1, q, key_dim), lambda b, c: (b, c, 0)),
            pl.BlockSpec((1, q, key_dim), lambda b, c: (b, c, 1)),
            pl.BlockSpec((1, q, val_dim), lambda b, c: (b, c, 1)),
            pl.BlockSpec((1, q, val_dim), lambda b, c: (b, c, 2)),
            pl.BlockSpec((1, q, 128), lambda b, c: (b, c, ba_blk)),
            pl.BlockSpec((1, 3, key_dim), lambda b, c: (b, 0, 0)),
            pl.BlockSpec((1, 3, key_dim), lambda b, c: (b, 0, 1)),
            pl.BlockSpec((1, 3, val_dim), lambda b, c: (b, 0, 1)),
            pl.BlockSpec((1,) + s0.shape[1:], lambda b, c: (b, 0, 0)),
            pl.BlockSpec((4, key_dim), lambda b, c: (0, 0)),
            pl.BlockSpec((4, key_dim), lambda b, c: (0, 1)),
            pl.BlockSpec((4, val_dim), lambda b, c: (0, 1)),
            _resident((1, 128)), _resident((1, 128)), _resident((1, GDN_DV)),
        ],
        out_specs=(pl.BlockSpec((1, q, val_dim), lambda b, c: (b, c, 0)),
                   pl.BlockSpec((1,) + s0.shape[1:], lambda b, c: (b, 0, 0))),
        scratch_shapes=[pltpu.VMEM((q + 8, key_dim), F32), pltpu.VMEM((q + 8, key_dim), F32),
                        pltpu.VMEM((q + 8, val_dim), F32)],
        compiler_params=_cparams("parallel", "arbitrary"),
        name="gdn_core",
    )(proj, proj, proj, proj, proj, conv_buf, conv_buf, conv_buf, s0,
      conv_w, conv_w, conv_w, dtb, alog, norm_w.reshape(1, GDN_DV))
    return o, s.reshape(bsz, n_hv, GDN_DK, GDN_DV)


def gdn_layer_padded(x, t_valid, conv_buf, s0, norm_g, w_in, conv_w, dt_bias, a_log, norm_w, w_out):
    bsz, t_len, d = x.shape
    x2 = x.reshape(bsz * t_len, d)
    proj = norm_matmul(x2, norm_g, w_in).reshape(bsz, t_len, -1)
    o, s_new = gdn_core(proj, t_valid, conv_buf, s0, conv_w, dt_bias, a_log, norm_w)
    conv_new = proj[:, t_valid - 3:t_valid, :conv_w.shape[1]]
    out = matmul_res(o.reshape(bsz * t_len, -1), w_out, x2)
    return out.reshape(bsz, t_len, d), conv_new, s_new


def _head_ones(width, head):
    r = lax.broadcasted_iota(jnp.int32, (width, width), 0) // head
    c = lax.broadcasted_iota(jnp.int32, (width, width), 1) // head
    return jnp.where(r == c, 1.0, 0.0).astype(BF16)


def _head_sum(x, ones):
    hi, lo = _split(x)
    return _mm(hi, ones) + _mm(lo, ones)


def _rw_proj_kernel(hn_ref, pv_ref, mu_ref, wr_ref, wk_ref, wv_ref, w1_ref, w2_ref, a1_ref, a2_ref,
                    g1_ref, g2_ref, w0_ref, a0_ref, kk_ref, ka_ref,
                    r_ref, lw_ref, k_ref, v_ref, kn_ref, b_ref, g_ref):
    hn = hn_ref[...]
    dlt = pv_ref[...] - hn

    def mix(c):
        return (hn + dlt * mu_ref[c:c + 1, :]).astype(BF16)

    r_ref[...] = _mm(mix(0), wr_ref[...])
    k = _mm(mix(1), wk_ref[...])
    v_ref[...] = _mm(mix(2), wv_ref[...])
    dec = w0_ref[...] + _mm(jnp.tanh(_mm(mix(3), w1_ref[...])).astype(BF16), w2_ref[...])
    lw_ref[...] = -RW_DECAY_SCALE * jax.nn.sigmoid(dec)
    a = jax.nn.sigmoid(a0_ref[...] + _mm(_mm(mix(4), a1_ref[...]).astype(BF16), a2_ref[...]))
    g_ref[...] = _mm(jax.nn.sigmoid(_mm(mix(5), g1_ref[...])).astype(BF16), g2_ref[...])
    ones = _head_ones(128, RW_HEAD_DIM)
    kn = k * kk_ref[...]
    d = kn.shape[1]
    for j in range(d // 128):
        sl = slice(j * 128, (j + 1) * 128)
        knj = kn[:, sl]
        knj = knj * lax.rsqrt(_head_sum(knj * knj, ones) + 1e-6)
        kn_ref[:, sl] = knj
        b_ref[:, sl] = knj * a[:, sl]
    k_ref[...] = k * (1.0 + (a - 1.0) * ka_ref[...])


def rw_proj(hn, prev, p, *, tm=256):
    n, d = hn.shape
    tm = _row_tile(n, tm)
    row = lambda i: (i, 0)
    fix = lambda i: (0, 0)
    big = pl.BlockSpec((tm, d), row)
    vec = pl.BlockSpec((1, d), fix)

    def full(a):
        return pl.BlockSpec(a.shape, fix)

    ws = [p['rw_w_rkv'][0].astype(BF16), p['rw_w_rkv'][1].astype(BF16), p['rw_w_rkv'][2].astype(BF16),
          p['rw_w1'].astype(BF16), p['rw_w2'].astype(BF16), p['rw_a1'].astype(BF16), p['rw_a2'].astype(BF16),
          p['rw_g1'].astype(BF16), p['rw_g2'].astype(BF16)]
    vecs = [p['rw_w0'].reshape(1, d), p['rw_a0'].reshape(1, d), p['rw_k_k'].reshape(1, d),
            p['rw_k_a'].reshape(1, d)]
    return pl.pallas_call(
        _rw_proj_kernel,
        out_shape=tuple(jax.ShapeDtypeStruct((n, d), F32) for _ in range(7)),
        grid=(pl.cdiv(n, tm),),
        in_specs=[big, big, full(p['rw_mu'])] + [full(w) for w in ws] + [vec] * 4,
        out_specs=tuple(big for _ in range(7)),
        compiler_params=_cparams("parallel"),
        name="rw_proj",
    )(hn, prev, p['rw_mu'], *ws, *vecs)


def _rw_scan_kernel(r_ref, lw_ref, k_ref, v_ref, kn_ref, b_ref, g_ref, s0_ref, rk_ref, lnw_ref, lnb_ref,
                    o_ref, s_ref, *, q, t_len):
    c = pl.program_id(1)

    @pl.when(c == 0)
    def _():
        s_ref[0] = s0_ref[0]

    rows = c * q + lax.broadcasted_iota(jnp.int32, (q, 1), 0)
    valid = rows < t_len
    incl = _tri(q)
    strict = _tri(q, strict=True)
    lw_all = jnp.where(valid, lw_ref[0], 0.0)
    cum_all = _dot_hi(incl.astype(F32), lw_all)
    ones = _head_ones(128, RW_HEAD_DIM)
    lo = lax.broadcasted_iota(jnp.int32, (q, 128), 1) < RW_HEAD_DIM
    blockdiag = ((lax.broadcasted_iota(jnp.int32, (128, 128), 0) < RW_HEAD_DIM)
                 == (lax.broadcasted_iota(jnp.int32, (128, 128), 1) < RW_HEAD_DIM))
    d = lw_all.shape[1]
    pairs = range(d // 128)
    sls = [slice(p * 128, (p + 1) * 128) for p in pairs]
    rs, ks, vs, bends, kends, s_old, cum_ends = [], [], [], [], [], [], []
    pk, pb, u0, y0 = [], [], [], []
    for p in pairs:
        sl = sls[p]
        lw = lw_all[:, sl]
        cum = cum_all[:, sl]
        cum_end = cum[q - 1:q, :]
        r = jnp.where(valid, r_ref[0, :, sl], 0.0)
        k = jnp.where(valid, k_ref[0, :, sl], 0.0)
        v = jnp.where(valid, v_ref[0, :, sl], 0.0)
        kn = jnp.where(valid, kn_ref[0, :, sl], 0.0)
        b = jnp.where(valid, b_ref[0, :, sl], 0.0)
        kq = kn * jnp.exp(cum - lw)
        rq = r * jnp.exp(cum)
        einv = jnp.exp(-cum)
        kd = k * einv
        bd = b * einv
        eend = jnp.exp(cum_end - cum)
        s2 = s_ref[0, sl, :]
        lhs = jnp.concatenate([jnp.where(lo, kq, 0.0), jnp.where(lo, 0.0, kq),
                               jnp.where(lo, rq, 0.0), jnp.where(lo, 0.0, rq)], axis=0)
        pk.append(_dot_nt(lhs, kd))
        pb.append(_dot_nt(lhs, bd))
        u0.append(_dot_nt(kq, s2))
        y0.append(_dot_nt(rq, s2))
        rs.append(r)
        ks.append(k)
        vs.append(v)
        kends.append(k * eend)
        bends.append(b * eend)
        s_old.append(s2)
        cum_ends.append(cum_end)
    ab = [jnp.where(strict, pb[p][e * q:(e + 1) * q], 0.0) for p in pairs for e in range(2)]
    rhs = [u0[p] + jnp.where(lo, _dot(jnp.where(strict, pk[p][0:q], 0.0), vs[p]),
                             _dot(jnp.where(strict, pk[p][q:2 * q], 0.0), vs[p])) for p in pairs]

    def per_head(ms, vecs, dot):
        return [jnp.where(lo, dot(ms[2 * p], vecs[p]), dot(ms[2 * p + 1], vecs[p])) for p in pairs]

    us = _solve_unit_lower(ab, rhs, per_head)
    ys = []
    for p in pairs:
        rk = [jnp.where(incl, pk[p][(2 + e) * q:(3 + e) * q], 0.0) for e in range(2)]
        rb = [jnp.where(incl, pb[p][(2 + e) * q:(3 + e) * q], 0.0) for e in range(2)]
        ys.append(y0[p] + jnp.where(lo, _dot(rk[0], vs[p]) - _dot(rb[0], us[p]),
                                    _dot(rk[1], vs[p]) - _dot(rb[1], us[p])))
    s_new = [s_old[p] * jnp.exp(cum_ends[p]) + _dot_tn(vs[p], kends[p]) - _dot_tn(us[p], bends[p])
             for p in pairs]
    for p in pairs:
        sl = sls[p]
        s_ref[0, sl, :] = jnp.where(blockdiag, s_new[p], 0.0)
        y = ys[p]
        mean = _head_sum(y, ones) * (1.0 / RW_HEAD_DIM)
        yc = y - mean
        var = _head_sum(yc * yc, ones) * (1.0 / RW_HEAD_DIM)
        yn = yc * lax.rsqrt(var + RW_GN_EPS) * lnw_ref[:, sl] + lnb_ref[:, sl]
        bonus = _head_sum(rs[p] * ks[p] * rk_ref[:, sl], ones) * vs[p]
        o_ref[0, :, sl] = ((yn + bonus) * g_ref[0, :, sl]).astype(o_ref.dtype)


def rw_scan(r, lw, k, v, kn, b, g, t_valid, s0, r_k, ln_w, ln_b, *, chunk=64):
    bsz, t_len, d = r.shape
    q = min(chunk, t_len)
    assert q % 8 == 0 and t_len % q == 0 and d % 128 == 0
    n_heads = d // RW_HEAD_DIM
    s5 = s0.reshape(bsz, n_heads // 2, 2, RW_HEAD_DIM, RW_HEAD_DIM)
    s2 = jnp.einsum('bpevk,ef->bpevfk', s5, jnp.eye(2, dtype=F32)).reshape(bsz, d, 128)
    blk = pl.BlockSpec((1, q, d), lambda bb, c: (bb, c, 0))
    vec = pl.BlockSpec((1, d), lambda bb, c: (0, 0))
    st = pl.BlockSpec((1, d, 128), lambda bb, c: (bb, 0, 0))
    o, s_new = pl.pallas_call(
        functools.partial(_rw_scan_kernel, q=q, t_len=t_valid),
        out_shape=(jax.ShapeDtypeStruct((bsz, t_len, d), BF16), jax.ShapeDtypeStruct((bsz, d, 128), F32)),
        grid=(bsz, t_len // q),
        in_specs=[blk] * 7 + [st, vec, vec, vec],
        out_specs=(blk, st),
        compiler_params=_cparams("parallel", "arbitrary"),
        name="rw_scan",
    )(r, lw, k, v, kn, b, g, s2, r_k.reshape(1, d), ln_w.reshape(1, d), ln_b.reshape(1, d))
    s6 = s_new.reshape(bsz, n_heads // 2, 2, RW_HEAD_DIM, 2, RW_HEAD_DIM)
    s_out = jnp.stack([s6[:, :, 0, :, 0, :], s6[:, :, 1, :, 1, :]], axis=2)
    return o, s_out.reshape(bsz, n_heads, RW_HEAD_DIM, RW_HEAD_DIM)


def rwkv_layer_padded(x, t_valid, shift_buf, s0, norm_g, p):
    bsz, t_len, d = x.shape
    x2 = x.reshape(bsz * t_len, d)
    hn = rmsnorm_rows(x2, norm_g).reshape(bsz, t_len, d)
    prev = jnp.concatenate([shift_buf, hn[:, :-1]], axis=1)
    outs = rw_proj(hn.reshape(-1, d), prev.reshape(-1, d), p)
    r, lw, k, v, kn, b, g = (a.reshape(bsz, t_len, d) for a in outs)
    yg, s_new = rw_scan(r, lw, k, v, kn, b, g, t_valid, s0, p['rw_r_k'], p['rw_ln_w'], p['rw_ln_b'])
    out = matmul_res(yg.reshape(bsz * t_len, d), p['rw_w_out'].astype(BF16), x2)
    return out.reshape(bsz, t_len, d), hn[:, t_valid - 1:t_valid], s_new


def _dot3_nt(a, b):
    ah, al = _split(a)
    bh, bl = _split(b)
    dn = (((1,), (1,)), ((), ()))
    f = lambda u, w: lax.dot_general(u, w, dn, preferred_element_type=F32)
    return f(ah, bh) + (f(ah, bl) + f(al, bh))


def _suffix_ones(n):
    r = lax.broadcasted_iota(jnp.int32, (n, n), 0)
    c = lax.broadcasted_iota(jnp.int32, (n, n), 1)
    return jnp.where(r >= c, 1.0, 0.0).astype(BF16)


def _sb_block(q_e, kblk, vblk, carry, acc, lo, suffix, mask):
    outs = []
    new_carry = []
    for e in range(2):
        z = _dot3_nt(q_e[e], kblk)
        lnb = -_softplus(z)
        if mask is not None:
            lnb = jnp.where(mask, lnb, 0.0)
        hi, lw = _split(lnb)
        rsum = _mm(hi, suffix) + _mm(lw, suffix) + carry[e]
        att = jnp.exp(z + rsum)
        if mask is not None:
            att = jnp.where(mask, att, 0.0)
        outs.append(_dot(att, vblk))
        new_carry.append(rsum[:, 0:1])
    return new_carry, acc + jnp.where(lo, outs[0], outs[1])


def _sb_kernel(q_ref, kn_ref, vn_ref, o_ref, kmax_ref, *, bq, n_pairs, scale):
    i = pl.program_id(2)
    t_new = kn_ref.shape[1]
    width = 128 * n_pairs
    pairs = range(n_pairs)

    @pl.when(i == 0)
    def _():
        def body(j, m):
            rows = kn_ref[0, pl.ds(pl.multiple_of(j * bq, bq), bq), :]
            return jnp.maximum(m, jnp.max(jnp.abs(rows).reshape(bq // 8, 8, width), axis=0))

        m8 = lax.fori_loop(0, t_new // bq, body, jnp.zeros((8, width), F32))
        kmax_ref[...] = jnp.max(m8, axis=0, keepdims=True)

    lo = lax.broadcasted_iota(jnp.int32, (bq, 128), 1) < SB_HEAD_DIM
    ones = _head_ones(128, SB_HEAD_DIM)
    q_e, zbs = [], []
    for p in pairs:
        q = q_ref[0, :, p * 128:(p + 1) * 128] * scale
        q_e.append([jnp.where(lo, q, 0.0), jnp.where(lo, 0.0, q)])
        zb = _head_sum(jnp.abs(q) * kmax_ref[:, p * 128:(p + 1) * 128], ones)
        zbs += [zb[:, 0:1], zb[:, SB_HEAD_DIM:SB_HEAD_DIM + 1]]
    suffix = _suffix_ones(bq)

    def block(off, carries, accs, mask):
        new_c, new_a = [], []
        for p in pairs:
            c, a = _sb_block(q_e[p], kn_ref[0, pl.ds(off, bq), p * 128:(p + 1) * 128],
                             vn_ref[0, pl.ds(off, bq), p * 128:(p + 1) * 128],
                             carries[2 * p:2 * p + 2], accs[p], lo, suffix, mask)
            new_c += c
            new_a.append(a)
        return new_c, new_a

    def live(carries):
        worst = carries[0] + zbs[0]
        for c, zb in zip(carries[1:], zbs[1:]):
            worst = jnp.maximum(worst, c + zb)
        return (jnp.max(worst) > SB_LOG_CUT).astype(jnp.int32)

    zero_c = jnp.zeros((bq, 1), F32)
    carries, accs = block(pl.multiple_of(i * bq, bq), [zero_c] * (2 * n_pairs),
                          [jnp.zeros((bq, 128), F32)] * n_pairs, _tri(bq, strict=True))

    def cond(s):
        return (s[0] < i) & (s[1] > 0)

    def body(s):
        jj, _, cs, acs = s
        cs, acs = block(pl.multiple_of((i - 1 - jj) * bq, bq), list(cs), list(acs), None)
        return jj + 1, live(cs), tuple(cs), tuple(acs)

    st = lax.while_loop(cond, body, (jnp.int32(0), live(carries), tuple(carries), tuple(accs)))
    for p in pairs:
        o_ref[0, :, p * 128:(p + 1) * 128] = st[3][p].astype(o_ref.dtype)


def sb_attention(qkv, *, block=256, pairs_per_step=2):
    bsz, t_len, d3 = qkv.shape
    d = d3 // 3
    width = 128 * pairs_per_step
    assert d % width == 0
    n_groups = d // width
    bq = min(block, t_len)
    assert t_len % bq == 0 and bq % 8 == 0

    def kv_spec(which):
        return pl.BlockSpec((1, t_len, width), lambda b, p, i: (b, 0, which * n_groups + p),
                            pipeline_mode=pl.Buffered(1))

    return pl.pallas_call(
        functools.partial(_sb_kernel, bq=bq, n_pairs=pairs_per_step, scale=SB_HEAD_DIM ** -0.5),
        out_shape=jax.ShapeDtypeStruct((bsz, t_len, d), BF16),
        grid=(bsz, n_groups, t_len // bq),
        in_specs=[pl.BlockSpec((1, bq, width), lambda b, p, i: (b, i, p)), kv_spec(1), kv_spec(2)],
        out_specs=pl.BlockSpec((1, bq, width), lambda b, p, i: (b, i, p)),
        compiler_params=_cparams("parallel", "parallel", "arbitrary"),
        scratch_shapes=[pltpu.VMEM((1, width), F32)],
        name="sb_attention",
    )(qkv, qkv, qkv)


def _sb_decode_kernel(qkv_ref, kc_ref, vc_ref, o_ref, q_s, acc_s, carry_s, *, t, n_heads, bp, scale):
    j = pl.program_id(1)
    hd = SB_HEAD_DIM
    d = n_heads * hd
    rows = n_heads * t
    heads = range(n_heads)

    def scores(keys):
        return jnp.concatenate([_dot3_nt(q_s[h * t:(h + 1) * t, :], keys(h)) for h in heads], axis=0)

    def weights(z, carry, suffix, mask):
        lnb = -_softplus(z)
        if mask is not None:
            lnb = jnp.where(mask, lnb, 0.0)
        hi, lw = _split(lnb)
        rsum = _mm(hi, suffix) + _mm(lw, suffix) + carry
        att = jnp.exp(z + rsum)
        if mask is not None:
            att = jnp.where(mask, att, 0.0)
        return att, rsum[:, 0:1]

    @pl.when(j == 0)
    def _():
        for h in heads:
            q_s[h * t:(h + 1) * t, :] = qkv_ref[0, :, h * hd:(h + 1) * hd] * scale
        z = scores(lambda h: qkv_ref[0, :, d + h * hd:d + (h + 1) * hd])
        qi = lax.rem(lax.broadcasted_iota(jnp.int32, (rows, t), 0), t)
        mask = lax.broadcasted_iota(jnp.int32, (rows, t), 1) < qi
        att, carry = weights(z, 0.0, _suffix_ones(t), mask)
        for h in heads:
            acc_s[h * t:(h + 1) * t, :] = _dot(att[h * t:(h + 1) * t],
                                               qkv_ref[0, :, 2 * d + h * hd:2 * d + (h + 1) * hd])
        carry_s[...] = carry

    ub = jnp.concatenate([_dot(jnp.abs(q_s[h * t:(h + 1) * t, :]), jnp.abs(kc_ref[0, h])) for h in heads], axis=0)
    zb = SB_BOUND_SLACK * jnp.max(ub, axis=-1, keepdims=True)

    @pl.when(jnp.max(carry_s[...] + zb) > SB_LOG_CUT)
    def _():
        z = jnp.concatenate([_dot3(q_s[h * t:(h + 1) * t, :], kc_ref[0, h]) for h in heads], axis=0)
        att, carry = weights(z, carry_s[...], _suffix_ones(bp), None)
        for h in heads:
            acc_s[h * t:(h + 1) * t, :] += _dot_nt(att[h * t:(h + 1) * t], vc_ref[0, h])
        carry_s[...] = carry

    @pl.when(j == pl.num_programs(1) - 1)
    def _():
        o_ref[0] = jnp.concatenate([acc_s[h * t:(h + 1) * t, :] for h in heads], axis=1).astype(o_ref.dtype)


def sb_decode(qkv, k_cache, v_cache, *, block=512):
    bsz, t_len, d3 = qkv.shape
    _, past_len, n_heads, hd = k_cache.shape
    assert hd == SB_HEAD_DIM and n_heads * hd * 3 == d3 and t_len % 8 == 0
    bp = min(block, past_len)
    assert past_len % bp == 0
    n_blk = past_len // bp
    k_cache = jnp.transpose(k_cache, (0, 2, 3, 1))
    v_cache = jnp.transpose(v_cache, (0, 2, 3, 1))
    cache_spec = pl.BlockSpec((1, n_heads, hd, bp), lambda b, j: (b, 0, 0, n_blk - 1 - j))
    return pl.pallas_call(
        functools.partial(_sb_decode_kernel, t=t_len, n_heads=n_heads, bp=bp, scale=hd ** -0.5),
        out_shape=jax.ShapeDtypeStruct((bsz, t_len, n_heads * hd), BF16),
        grid=(bsz, n_blk),
        in_specs=[pl.BlockSpec((1, t_len, d3), lambda b, j: (b, 0, 0)), cache_spec, cache_spec],
        out_specs=pl.BlockSpec((1, t_len, n_heads * hd), lambda b, j: (b, 0, 0)),
        scratch_shapes=[pltpu.VMEM((n_heads * t_len, hd), F32), pltpu.VMEM((n_heads * t_len, hd), F32),
                        pltpu.VMEM((n_heads * t_len, 1), F32)],
        compiler_params=_cparams("parallel", "arbitrary"),
        name="sb_decode",
    )(qkv, k_cache, v_cache)


def sb_layer_padded(x, t_valid, k_past, v_past, norm_g, w_qkv, w_out):
    bsz, t_len, d = x.shape
    x2 = x.reshape(bsz * t_len, d)
    qkv = norm_matmul(x2, norm_g, w_qkv).reshape(bsz, t_len, 3 * d)
    o = sb_attention(qkv) if k_past is None else sb_decode(qkv, k_past, v_past)
    out = matmul_res(o.reshape(bsz * t_len, d), w_out, x2)
    n_heads = d // SB_HEAD_DIM
    k_new = qkv[:, :t_valid, d:2 * d].reshape(bsz, t_valid, n_heads, SB_HEAD_DIM)
    v_new = qkv[:, :t_valid, 2 * d:].reshape(bsz, t_valid, n_heads, SB_HEAD_DIM)
    return out.reshape(bsz, t_len, d), k_new, v_new


def _run_trunk(x, t_valid, st, p):
    bsz, t_len, d = x.shape
    new = {}
    bf = lambda a: a.astype(BF16)

    def ffn_layer(x, i):
        return ffn(x.reshape(bsz * t_len, d), p['norm_ffn'][i], bf(p['ffn_w_gu'][i]),
                   bf(p['ffn_w_down'][i])).reshape(bsz, t_len, d)

    x, new['ssm_conv'], new['ssm'] = mamba2_layer_padded(
        x, t_valid, st['ssm_conv'], st['ssm'], p['norm_mix'][0], bf(p['mb_w_in']), p['mb_conv_w'],
        p['mb_conv_b'], p['mb_dt_bias'], p['mb_a_log'], p['mb_d'], p['mb_norm'], bf(p['mb_w_out']))
    x = ffn_layer(x, 0)
    x, new['gdn_conv'], new['gdn'] = gdn_layer_padded(
        x, t_valid, st['gdn_conv'], st['gdn'], p['norm_mix'][1], bf(p['gdn_w_in']), p['gdn_conv_w'],
        p['gdn_dt_bias'], p['gdn_a_log'], p['gdn_norm'], bf(p['gdn_w_out']))
    x = ffn_layer(x, 1)
    x, new['rwkv_shift'], new['rwkv'] = rwkv_layer_padded(
        x, t_valid, st['rwkv_shift'], st['rwkv'], p['norm_mix'][2], p)
    x = ffn_layer(x, 2)
    x, new['sb_k'], new['sb_v'] = sb_layer_padded(
        x, t_valid, st['sb_k'], st['sb_v'], p['norm_mix'][3], bf(p['sb_w_qkv']), bf(p['sb_w_out']))
    x = ffn_layer(x, 3)
    y = rmsnorm_rows(x.reshape(bsz * t_len, d), p['norm_final']).reshape(bsz, t_len, d)
    return y, new


def kernel(x_prompt, x_sample, state_ssm, state_ssm_conv, state_gdn, state_gdn_conv, state_rwkv, state_rwkv_shift, cache_sb_k, cache_sb_v, meta_tokens, norm_mix, norm_ffn, norm_final, ffn_w_gu, ffn_w_down, mb_w_in, mb_conv_w, mb_conv_b, mb_dt_bias, mb_a_log, mb_d, mb_norm, mb_w_out, gdn_w_in, gdn_conv_w, gdn_dt_bias, gdn_a_log, gdn_norm, gdn_w_out, rw_mu, rw_w_rkv, rw_w0, rw_w1, rw_w2, rw_a0, rw_a1, rw_a2, rw_g1, rw_g2, rw_k_k, rw_k_a, rw_r_k, rw_ln_w, rw_ln_b, rw_w_out, sb_w_qkv, sb_w_out):
    p = dict(
        norm_mix=norm_mix, norm_ffn=norm_ffn, norm_final=norm_final, ffn_w_gu=ffn_w_gu, ffn_w_down=ffn_w_down,
        mb_w_in=mb_w_in, mb_conv_w=mb_conv_w, mb_conv_b=mb_conv_b, mb_dt_bias=mb_dt_bias, mb_a_log=mb_a_log,
        mb_d=mb_d, mb_norm=mb_norm, mb_w_out=mb_w_out,
        gdn_w_in=gdn_w_in, gdn_conv_w=gdn_conv_w, gdn_dt_bias=gdn_dt_bias, gdn_a_log=gdn_a_log,
        gdn_norm=gdn_norm, gdn_w_out=gdn_w_out,
        rw_mu=rw_mu, rw_w_rkv=rw_w_rkv, rw_w0=rw_w0, rw_w1=rw_w1, rw_w2=rw_w2, rw_a0=rw_a0, rw_a1=rw_a1,
        rw_a2=rw_a2, rw_g1=rw_g1, rw_g2=rw_g2, rw_k_k=rw_k_k, rw_k_a=rw_k_a, rw_r_k=rw_r_k, rw_ln_w=rw_ln_w,
        rw_ln_b=rw_ln_b, rw_w_out=rw_w_out, sb_w_qkv=sb_w_qkv, sb_w_out=sb_w_out)
    bsz, seq, d = x_prompt.shape
    n_meta = meta_tokens.shape[0]
    t_valid = n_meta + seq
    t_pad = -(-t_valid // PROMPT_ROW_ALIGN) * PROMPT_ROW_ALIGN
    meta = jnp.broadcast_to(meta_tokens[None], (bsz, n_meta, d))
    x0 = jnp.concatenate([meta, x_prompt, jnp.zeros((bsz, t_pad - t_valid, d), x_prompt.dtype)], axis=1)
    fresh = dict(
        ssm=jnp.zeros((bsz,) + state_ssm.shape[1:], F32), ssm_conv=jnp.zeros((bsz,) + state_ssm_conv.shape[1:], F32),
        gdn=jnp.zeros((bsz,) + state_gdn.shape[1:], F32), gdn_conv=jnp.zeros((bsz,) + state_gdn_conv.shape[1:], F32),
        rwkv=jnp.zeros((bsz,) + state_rwkv.shape[1:], F32),
        rwkv_shift=jnp.zeros((bsz,) + state_rwkv_shift.shape[1:], F32), sb_k=None, sb_v=None)
    y_full, sp = _run_trunk(x0, t_valid, fresh, p)
    y_prompt = y_full[:, n_meta:t_valid]
    past = dict(ssm=state_ssm, ssm_conv=state_ssm_conv, gdn=state_gdn, gdn_conv=state_gdn_conv,
                rwkv=state_rwkv, rwkv_shift=state_rwkv_shift, sb_k=cache_sb_k, sb_v=cache_sb_v)
    y_sample, ss = _run_trunk(x_sample, x_sample.shape[1], past, p)
    return (y_prompt, y_sample,
            sp['ssm'], sp['ssm_conv'], sp['gdn'], sp['gdn_conv'], sp['rwkv'], sp['rwkv_shift'],
            sp['sb_k'], sp['sb_v'],
            ss['ssm'], ss['ssm_conv'], ss['gdn'], ss['gdn_conv'], ss['rwkv'], ss['rwkv_shift'],
            ss['sb_k'], ss['sb_v'])
```

```python
import functools
import math

import jax
import jax.numpy as jnp
from jax import lax
from jax.experimental import pallas as pl
from jax.experimental.pallas import tpu as pltpu

F32 = jnp.float32
BF16 = jnp.bfloat16
EPS = 1e-6
VMEM_LIMIT = 48 * 1024 * 1024
HI = lax.Precision.HIGHEST

PROMPT_ROW_ALIGN = 256
MB_HEAD_DIM = 64
MB_D_STATE = 128
MB_GROUPS = 4
GDN_DK = 128
GDN_DV = 128
RW_HEAD_DIM = 64
RW_GN_EPS = 64e-5
RW_DECAY_SCALE = 0.6065306597126334
SB_HEAD_DIM = 64
SB_LOG_CUT = -100.0
SB_BOUND_SLACK = 1.01


def _cparams(*sem):
    return pltpu.CompilerParams(dimension_semantics=sem, vmem_limit_bytes=VMEM_LIMIT)


def _dot(a, b):
    return jnp.dot(a.astype(BF16), b.astype(BF16), preferred_element_type=F32)


def _dot_nt(a, b):
    return lax.dot_general(a.astype(BF16), b.astype(BF16), (((1,), (1,)), ((), ())),
                           preferred_element_type=F32)


def _dot_tn(a, b):
    return lax.dot_general(a.astype(BF16), b.astype(BF16), (((0,), (0,)), ((), ())),
                           preferred_element_type=F32)


def _dot_hi(a, b):
    return jnp.dot(a, b, preferred_element_type=F32, precision=HI)


def _silu(x):
    return x * jax.nn.sigmoid(x)


def _softplus(x):
    return jnp.maximum(x, 0.0) + jnp.log(1.0 + jnp.exp(-jnp.abs(x)))


def _rmsnorm(x, g):
    return x * lax.rsqrt(jnp.mean(x * x, axis=-1, keepdims=True) + EPS) * g


def _row_tile(n, want):
    return n if n <= want else want


def _resident(shape):
    zeros = (0,) * len(shape)
    return pl.BlockSpec(shape, lambda *_: zeros, pipeline_mode=pl.Buffered(1))


def _norm_matmul_kernel(x_ref, g_ref, w_ref, o_ref):
    h = _rmsnorm(x_ref[...], g_ref[...]).astype(BF16)
    o_ref[...] = jnp.dot(h, w_ref[...], preferred_element_type=F32)


def norm_matmul(x, g, w, *, tm=256):
    n, d = x.shape
    m = w.shape[1]
    tm = _row_tile(n, tm)
    return pl.pallas_call(
        _norm_matmul_kernel,
        out_shape=jax.ShapeDtypeStruct((n, m), F32),
        grid=(pl.cdiv(n, tm),),
        in_specs=[pl.BlockSpec((tm, d), lambda i: (i, 0)), _resident((1, d)), _resident((d, m))],
        out_specs=pl.BlockSpec((tm, m), lambda i: (i, 0)),
        compiler_params=_cparams("parallel"),
        name="norm_matmul",
    )(x, g.reshape(1, d), w)


def _mix_ffn_kernel(a_ref, wo_ref, x_ref, g_ref, wgu_ref, wd_ref, o_ref, *, f, tf):
    x = x_ref[...] + jnp.dot(a_ref[...], wo_ref[...], preferred_element_type=F32)
    h = _rmsnorm(x, g_ref[...]).astype(BF16)
    acc = x
    for c in range(f // tf):
        gate = jnp.dot(h, wgu_ref[:, c * tf:(c + 1) * tf], preferred_element_type=F32)
        up = jnp.dot(h, wgu_ref[:, f + c * tf:f + (c + 1) * tf], preferred_element_type=F32)
        act = (_silu(gate) * up).astype(BF16)
        acc = acc + jnp.dot(act, wd_ref[c * tf:(c + 1) * tf, :], preferred_element_type=F32)
    o_ref[...] = acc


def mix_ffn(a, w_out, x, g, w_gu, w_down, *, tm=512, tf=1408):
    n, d = x.shape
    k = a.shape[1]
    f = w_down.shape[0]
    tm = _row_tile(n, tm)
    if f % tf:
        tf = f
    return pl.pallas_call(
        functools.partial(_mix_ffn_kernel, f=f, tf=tf),
        out_shape=jax.ShapeDtypeStruct((n, d), F32),
        grid=(pl.cdiv(n, tm),),
        in_specs=[pl.BlockSpec((tm, k), lambda i: (i, 0)), _resident((k, d)),
                  pl.BlockSpec((tm, d), lambda i: (i, 0)), _resident((1, d)), _resident((d, 2 * f)),
                  _resident((f, d))],
        out_specs=pl.BlockSpec((tm, d), lambda i: (i, 0)),
        compiler_params=_cparams("parallel"),
        name="mix_ffn",
    )(a, w_out, x, g.reshape(1, d), w_gu, w_down)


def _rmsnorm_kernel(x_ref, g_ref, o_ref):
    o_ref[...] = _rmsnorm(x_ref[...], g_ref[...])


def rmsnorm_rows(x, g, *, tm=1024):
    n, d = x.shape
    tm = _row_tile(n, tm)
    return pl.pallas_call(
        _rmsnorm_kernel,
        out_shape=jax.ShapeDtypeStruct((n, d), F32),
        grid=(pl.cdiv(n, tm),),
        in_specs=[pl.BlockSpec((tm, d), lambda i: (i, 0)),
                  pl.BlockSpec((1, d), lambda i: (0, 0))],
        out_specs=pl.BlockSpec((tm, d), lambda i: (i, 0)),
        compiler_params=_cparams("parallel"),
        name="rmsnorm",
    )(x, g.reshape(1, d))


def _causal_conv_silu(raw, buf_ref, w_ref, bias, q):
    buf_ref[pl.ds(8, q), :] = raw
    out = buf_ref[pl.ds(5, q), :] * w_ref[0:1, :]
    out = out + buf_ref[pl.ds(6, q), :] * w_ref[1:2, :]
    out = out + buf_ref[pl.ds(7, q), :] * w_ref[2:3, :]
    out = out + raw * w_ref[3:4, :]
    if bias is not None:
        out = out + bias
    buf_ref[pl.ds(5, 3), :] = buf_ref[pl.ds(q + 5, 3), :]
    return _silu(out)


def _tri(q, strict=False):
    r = lax.broadcasted_iota(jnp.int32, (q, q), 0)
    c = lax.broadcasted_iota(jnp.int32, (q, q), 1)
    return (r > c) if strict else (r >= c)


def _mamba_kernel(z_ref, xs_ref, bc_ref, dt_ref, cx_ref, cbc_ref, h0_ref,
                  cwx_ref, cwbc_ref, cbx_ref, cbbc_ref, dtb_ref, a_ref, dskip_ref, nw_ref,
                  y_ref, h_ref, bufx_ref, bufbc_ref, *, q, t_len, n_heads):
    c = pl.program_id(1)
    n_state = MB_D_STATE
    hpg = n_heads // MB_GROUPS
    gw = hpg * MB_HEAD_DIM

    @pl.when(c == 0)
    def _():
        bufx_ref[pl.ds(5, 3), :] = cx_ref[0]
        bufbc_ref[pl.ds(5, 3), :] = cbc_ref[0]
        h_ref[0] = h0_ref[0]

    rows = c * q + lax.broadcasted_iota(jnp.int32, (q, 1), 0)
    valid = rows < t_len
    xs = _causal_conv_silu(jnp.where(valid, xs_ref[0], 0.0), bufx_ref, cwx_ref, cbx_ref[...], q)
    bc = _causal_conv_silu(jnp.where(valid, bc_ref[0], 0.0), bufbc_ref, cwbc_ref, cbbc_ref[...], q)

    lane = lax.broadcasted_iota(jnp.int32, (q, 128), 1)
    dt = _softplus(jnp.where(valid & (lane < n_heads), dt_ref[0], 0.0) + dtb_ref[...])
    dt = jnp.where(valid & (lane < n_heads), dt, 0.0)
    da = dt * a_ref[...]
    incl = _tri(q)
    acum = _dot_hi(incl.astype(F32), da)
    acum_t = acum.T
    dt_t = dt.T
    last = acum[q - 1:q, :]
    wts = jnp.exp(last - acum) * dt
    e_acum = jnp.exp(acum)
    e_last = jnp.exp(last)

    lo_half = lax.broadcasted_iota(jnp.int32, (q, 128), 1) < MB_HEAD_DIM
    lo_rows = lax.broadcasted_iota(jnp.int32, (128, 128), 0) < MB_HEAD_DIM

    z = z_ref[0]
    for g in range(MB_GROUPS):
        bm = bc[:, g * n_state:(g + 1) * n_state]
        cm = bc[:, (MB_GROUPS + g) * n_state:(MB_GROUPS + g + 1) * n_state]
        cb = _dot_nt(cm, bm)
        hg = h_ref[0, g * gw:(g + 1) * gw, :]
        ch = _dot_nt(cm, hg)
        ys = []
        for p in range(hpg // 2):
            h_e = g * hpg + 2 * p
            x_pair = xs[:, h_e * MB_HEAD_DIM:(h_e + 2) * MB_HEAD_DIM]
            outs = []
            for hh in (h_e, h_e + 1):
                seg = acum[:, hh:hh + 1] - acum_t[hh:hh + 1, :]
                decay = jnp.where(incl, jnp.exp(jnp.where(incl, seg, 0.0)), 0.0)
                m = cb * decay * dt_t[hh:hh + 1, :]
                outs.append(_dot(m, x_pair))
            y_pair = jnp.where(lo_half, outs[0], outs[1])
            e_pair = jnp.where(lo_half, e_acum[:, h_e:h_e + 1], e_acum[:, h_e + 1:h_e + 2])
            y_pair = y_pair + e_pair * ch[:, 2 * p * MB_HEAD_DIM:(2 * p + 2) * MB_HEAD_DIM]
            ys.append(y_pair)
            w_pair = jnp.where(lo_half, wts[:, h_e:h_e + 1], wts[:, h_e + 1:h_e + 2])
            upd = _dot_tn(x_pair * w_pair, bm)
            scale = jnp.where(lo_rows, e_last[:, h_e:h_e + 1], e_last[:, h_e + 1:h_e + 2])
            r0 = h_e * MB_HEAD_DIM
            h_ref[0, r0:r0 + 128, :] = h_ref[0, r0:r0 + 128, :] * scale + upd
        yg = jnp.concatenate(ys, axis=1)
        xg = xs[:, g * gw:(g + 1) * gw]
        yg = yg + xg * dskip_ref[:, g * gw:(g + 1) * gw]
        yg = yg * _silu(z[:, g * gw:(g + 1) * gw])
        yg = yg * lax.rsqrt(jnp.mean(yg * yg, axis=-1, keepdims=True) + EPS)
        y_ref[0, :, g * gw:(g + 1) * gw] = (yg * nw_ref[:, g * gw:(g + 1) * gw]).astype(y_ref.dtype)


def mamba_core(proj, t_valid, conv_buf, h0, conv_w, conv_b, dt_bias, a_log, d_skip, norm_w, *, chunk=128):
    bsz, t_len, _ = proj.shape
    n_heads = dt_bias.shape[0]
    d_inner = n_heads * MB_HEAD_DIM
    gn = MB_GROUPS * MB_D_STATE
    assert d_inner % 1024 == 0 and 2 * gn == 1024 and n_heads <= 128
    q = min(chunk, t_len)
    assert q % 8 == 0 and t_len % q == 0
    n_chunks = t_len // q
    dtb = jnp.zeros((1, 128), F32).at[0, :n_heads].set(dt_bias)
    a_row = jnp.zeros((1, 128), F32).at[0, :n_heads].set(-jnp.exp(a_log))
    dskip = jnp.repeat(d_skip, MB_HEAD_DIM).reshape(1, d_inner)
    kern = functools.partial(_mamba_kernel, q=q, t_len=t_valid, n_heads=n_heads)
    xblk = d_inner // 1024
    y, h = pl.pallas_call(
        kern,
        out_shape=(jax.ShapeDtypeStruct((bsz, t_len, d_inner), BF16),
                   jax.ShapeDtypeStruct((bsz, n_heads * MB_HEAD_DIM, MB_D_STATE), F32)),
        grid=(bsz, n_chunks),
        in_specs=[
            pl.BlockSpec((1, q, d_inner), lambda b, c: (b, c, 0)),
            pl.BlockSpec((1, q, d_inner), lambda b, c: (b, c, 1)),
            pl.BlockSpec((1, q, 2 * gn), lambda b, c: (b, c, 2 * xblk)),
            pl.BlockSpec((1, q, 128), lambda b, c: (b, c, (2 * d_inner + 2 * gn) // 128)),
            pl.BlockSpec((1, 3, d_inner), lambda b, c: (b, 0, 0)),
            pl.BlockSpec((1, 3, 2 * gn), lambda b, c: (b, 0, xblk)),
            pl.BlockSpec((1, n_heads * MB_HEAD_DIM, MB_D_STATE), lambda b, c: (b, 0, 0)),
            pl.BlockSpec((4, d_inner), lambda b, c: (0, 0)),
            pl.BlockSpec((4, 2 * gn), lambda b, c: (0, xblk)),
            pl.BlockSpec((1, d_inner), lambda b, c: (0, 0)),
            pl.BlockSpec((1, 2 * gn), lambda b, c: (0, xblk)),
            _resident((1, 128)), _resident((1, 128)), _resident((1, d_inner)), _resident((1, d_inner)),
        ],
        out_specs=(pl.BlockSpec((1, q, d_inner), lambda b, c: (b, c, 0)),
                   pl.BlockSpec((1, n_heads * MB_HEAD_DIM, MB_D_STATE), lambda b, c: (b, 0, 0))),
        scratch_shapes=[pltpu.VMEM((q + 8, d_inner), F32), pltpu.VMEM((q + 8, 2 * gn), F32)],
        compiler_params=_cparams("parallel", "arbitrary"),
        name="mamba_core",
    )(proj, proj, proj, proj, conv_buf, conv_buf, h0.reshape(bsz, n_heads * MB_HEAD_DIM, MB_D_STATE),
      conv_w, conv_w, conv_b.reshape(1, -1), conv_b.reshape(1, -1), dtb, a_row, dskip,
      norm_w.reshape(1, d_inner))
    return y, h.reshape(bsz, n_heads, MB_HEAD_DIM, MB_D_STATE)


def mamba2_mixer(x, t_valid, conv_buf, ssm_state, norm_g, w_in, conv_w, conv_b, dt_bias, a_log, d_skip, norm_w):
    bsz, t_len, d = x.shape
    d_inner = dt_bias.shape[0] * MB_HEAD_DIM
    proj = norm_matmul(x.reshape(bsz * t_len, d), norm_g, w_in).reshape(bsz, t_len, -1)
    y, h_new = mamba_core(proj, t_valid, conv_buf, ssm_state, conv_w, conv_b, dt_bias, a_log, d_skip, norm_w)
    conv_new = proj[:, t_valid - 3:t_valid, d_inner:d_inner + conv_w.shape[1]]
    return y.reshape(bsz * t_len, d_inner), conv_new, h_new


def _split(a):
    hi = a.astype(BF16)
    lo = (a - hi.astype(F32)).astype(BF16)
    return hi, lo


def _mm(a, b):
    return jnp.dot(a, b, preferred_element_type=F32)


def _dot3(a, b):
    ah, al = _split(a)
    bh, bl = _split(b)
    return _mm(ah, bh) + (_mm(ah, bl) + _mm(al, bh))


INV_BASE = 8


def _unit_lower_inverse(mats):
    n = mats[0].shape[0]
    r = lax.broadcasted_iota(jnp.int32, (n, n), 0)
    c = lax.broadcasted_iota(jnp.int32, (n, n), 1)
    eye = jnp.where(r == c, 1.0, 0.0)
    same = (r // INV_BASE) == (c // INV_BASE)
    diag = [jnp.where(same, a, 0.0) for a in mats]
    xs = [eye - dg for dg in diag]
    ps = diag
    k = 1
    while 2 * k < INV_BASE:
        ps = [_dot(p, p) for p in ps]
        xs = [x + _dot(x, p) for x, p in zip(xs, ps)]
        k *= 2
    w = INV_BASE
    while w < n:
        off = ((r // (2 * w)) == (c // (2 * w))) != ((r // w) == (c // w))
        ts = [_dot(jnp.where(off, a, 0.0), x) for a, x in zip(mats, xs)]
        xs = [x - _dot(x, t) for x, t in zip(xs, ts)]
        w *= 2
    return xs


def _solve_unit_lower(mats, rhs, apply):
    x_inv = _unit_lower_inverse(mats)
    d0 = apply(x_inv, rhs, _dot)
    t0 = apply(mats, d0, _dot3)
    res = [b - d - t for b, d, t in zip(rhs, d0, t0)]
    corr = apply(x_inv, res, _dot)
    return [d + e for d, e in zip(d0, corr)]


def _gdn_kernel(q_ref, k_ref, v_ref, z_ref, ba_ref, cq_ref, ck_ref, cv_ref, s0_ref,
                cwq_ref, cwk_ref, cwv_ref, dtb_ref, al_ref, nw_ref,
                o_ref, s_ref, bufq_ref, bufk_ref, bufv_ref, *, q, t_len, n_hv, n_hk):
    c = pl.program_id(1)
    rep = n_hv // n_hk

    @pl.when(c == 0)
    def _():
        bufq_ref[pl.ds(5, 3), :] = cq_ref[0]
        bufk_ref[pl.ds(5, 3), :] = ck_ref[0]
        bufv_ref[pl.ds(5, 3), :] = cv_ref[0]
        s_ref[0] = s0_ref[0]

    rows = c * q + lax.broadcasted_iota(jnp.int32, (q, 1), 0)
    valid = rows < t_len
    qc = _causal_conv_silu(jnp.where(valid, q_ref[0], 0.0), bufq_ref, cwq_ref, None, q)
    kc = _causal_conv_silu(jnp.where(valid, k_ref[0], 0.0), bufk_ref, cwk_ref, None, q)
    vc = _causal_conv_silu(jnp.where(valid, v_ref[0], 0.0), bufv_ref, cwv_ref, None, q)

    lane = lax.broadcasted_iota(jnp.int32, (q, 128), 1)
    ba = jnp.where(valid & (lane < 2 * n_hv), ba_ref[0], 0.0)
    beta = jnp.where(valid & (lane < n_hv), jax.nn.sigmoid(ba), 0.0)
    g = jnp.where(valid & (lane >= n_hv) & (lane < 2 * n_hv),
                  -jnp.exp(al_ref[...]) * _softplus(ba + dtb_ref[...]), 0.0)
    incl = _tri(q)
    strict = _tri(q, strict=True)
    gcum = _dot_hi(incl.astype(F32), g)
    gcum_t = gcum.T
    eg = jnp.exp(gcum)
    glast = gcum[q - 1:q, :]
    w_last = jnp.exp(glast - gcum)
    e_last = jnp.exp(glast)

    z = z_ref[0]
    qn, kn, kk, qk0 = [], [], [], []
    for kh in range(n_hk):
        qh = qc[:, kh * GDN_DK:(kh + 1) * GDN_DK]
        kh_ = kc[:, kh * GDN_DK:(kh + 1) * GDN_DK]
        qh = qh * (lax.rsqrt(jnp.sum(qh * qh, axis=-1, keepdims=True) + 1e-6) * (GDN_DK ** -0.5))
        kh_ = kh_ * lax.rsqrt(jnp.sum(kh_ * kh_, axis=-1, keepdims=True) + 1e-6)
        qn.append(qh)
        kn.append(kh_)
        kk.append(_dot_nt(kh_, kh_))
        qk0.append(_dot_nt(qh, kh_))
    heads = range(n_hv)
    s_old = [s_ref[0, h * GDN_DK:(h + 1) * GDN_DK, :] for h in heads]
    ks = [_dot(kn[h // rep], s_old[h]) for h in heads]
    qs = [_dot(qn[h // rep], s_old[h]) for h in heads]
    dec, a_mat, rhs = [], [], []
    for h in heads:
        hl = n_hv + h
        seg = gcum[:, hl:hl + 1] - gcum_t[hl:hl + 1, :]
        dec.append(jnp.where(incl, jnp.exp(jnp.where(incl, seg, 0.0)), 0.0))
        bt = beta[:, h:h + 1]
        a_mat.append(jnp.where(strict, kk[h // rep] * dec[h], 0.0) * bt)
        rhs.append(bt * (vc[:, h * GDN_DV:(h + 1) * GDN_DV] - eg[:, hl:hl + 1] * ks[h]))
    delta = _solve_unit_lower(a_mat, rhs, lambda ms, vs, dot: [dot(m, v) for m, v in zip(ms, vs)])
    o_mm = [_dot(qk0[h // rep] * dec[h], delta[h]) for h in heads]
    upd = [_dot_tn(kn[h // rep] * w_last[:, n_hv + h:n_hv + h + 1], delta[h]) for h in heads]
    for h in heads:
        hl = n_hv + h
        s_ref[0, h * GDN_DK:(h + 1) * GDN_DK, :] = s_old[h] * e_last[:, hl:hl + 1] + upd[h]
        o = eg[:, hl:hl + 1] * qs[h] + o_mm[h]
        o = o * lax.rsqrt(jnp.mean(o * o, axis=-1, keepdims=True) + EPS) * nw_ref[...]
        o = o * _silu(z[:, h * GDN_DV:(h + 1) * GDN_DV])
        o_ref[0, :, h * GDN_DV:(h + 1) * GDN_DV] = o.astype(o_ref.dtype)


def gdn_core(proj, t_valid, conv_buf, s0, conv_w, dt_bias, a_log, norm_w, *, chunk=64):
    bsz, t_len, width = proj.shape
    n_hv = dt_bias.shape[0]
    val_dim = n_hv * GDN_DV
    key_dim = (width - 2 * n_hv - 2 * val_dim) // 2
    n_hk = key_dim // GDN_DK
    assert val_dim == 2 * key_dim and key_dim % 128 == 0 and 2 * n_hv <= 128
    q = min(chunk, t_len)
    assert q % 8 == 0 and t_len % q == 0
    n_chunks = t_len // q
    dtb = jnp.zeros((1, 128), F32).at[0, n_hv:2 * n_hv].set(dt_bias)
    alog = jnp.zeros((1, 128), F32).at[0, n_hv:2 * n_hv].set(a_log)
    kern = functools.partial(_gdn_kernel, q=q, t_len=t_valid, n_hv=n_hv, n_hk=n_hk)
    ba_blk = (2 * key_dim + 2 * val_dim) // 128
    s0 = s0.reshape(bsz, n_hv * GDN_DK, GDN_DV)
    o, s = pl.pallas_call(
        kern,
        out_shape=(jax.ShapeDtypeStruct((bsz, t_len, val_dim), BF16),
                   jax.ShapeDtypeStruct(s0.shape, F32)),
        grid=(bsz, n_chunks),
        in_specs=[
            pl.BlockSpec((1, q, key_dim), lambda b, c: (b, c, 0)),
            pl.BlockSpec((1, q, key_dim), lambda b, c: (b, c, 1)),
            pl.BlockSpec((1, q, val_dim), lambda b, c: (b, c, 1)),
            pl.BlockSpec((1, q, val_dim), lambda b, c: (b, c, 2)),
            pl.BlockSpec((1, q, 128), lambda b, c: (b, c, ba_blk)),
            pl.BlockSpec((1, 3, key_dim), lambda b, c: (b, 0, 0)),
            pl.BlockSpec((1, 3, key_dim), lambda b, c: (b, 0, 1)),
            pl.BlockSpec((1, 3, val_dim), lambda b, c: (b, 0, 1)),
            pl.BlockSpec((1,) + s0.shape[1:], lambda b, c: (b, 0, 0)),
            pl.BlockSpec((4, key_dim), lambda b, c: (0, 0)),
            pl.BlockSpec((4, key_dim), lambda b, c: (0, 1)),
            pl.BlockSpec((4, val_dim), lambda b, c: (0, 1)),
            _resident((1, 128)), _resident((1, 128)), _resident((1, GDN_DV)),
        ],
        out_specs=(pl.BlockSpec((1, q, val_dim), lambda b, c: (b, c, 0)),
                   pl.BlockSpec((1,) + s0.shape[1:], lambda b, c: (b, 0, 0))),
        scratch_shapes=[pltpu.VMEM((q + 8, key_dim), F32), pltpu.VMEM((q + 8, key_dim), F32),
                        pltpu.VMEM((q + 8, val_dim), F32)],
        compiler_params=_cparams("parallel", "arbitrary"),
        name="gdn_core",
    )(proj, proj, proj, proj, proj, conv_buf, conv_buf, conv_buf, s0,
      conv_w, conv_w, conv_w, dtb, alog, norm_w.reshape(1, GDN_DV))
    return o, s.reshape(bsz, n_hv, GDN_DK, GDN_DV)


def gdn_mixer(x, t_valid, conv_buf, s0, norm_g, w_in, conv_w, dt_bias, a_log, norm_w):
    bsz, t_len, d = x.shape
    proj = norm_matmul(x.reshape(bsz * t_len, d), norm_g, w_in).reshape(bsz, t_len, -1)
    o, s_new = gdn_core(proj, t_valid, conv_buf, s0, conv_w, dt_bias, a_log, norm_w)
    conv_new = proj[:, t_valid - 3:t_valid, :conv_w.shape[1]]
    return o.reshape(bsz * t_len, -1), conv_new, s_new


def _head_ones(width, head):
    r = lax.broadcasted_iota(jnp.int32, (width, width), 0) // head
    c = lax.broadcasted_iota(jnp.int32, (width, width), 1) // head
    return jnp.where(r == c, 1.0, 0.0).astype(BF16)


def _head_sum(x, ones):
    hi, lo = _split(x)
    return _mm(hi, ones) + _mm(lo, ones)


def _rw_proj_kernel(hn_ref, pv_ref, mu_ref, wr_ref, wk_ref, wv_ref, w1_ref, w2_ref, a1_ref, a2_ref,
                    g1_ref, g2_ref, w0_ref, a0_ref, kk_ref, ka_ref,
                    r_ref, lw_ref, k_ref, v_ref, kn_ref, b_ref, g_ref):
    hn = hn_ref[...]
    dlt = pv_ref[...] - hn

    def mix(c):
        return (hn + dlt * mu_ref[c:c + 1, :]).astype(BF16)

    r_ref[...] = _mm(mix(0), wr_ref[...])
    k = _mm(mix(1), wk_ref[...])
    v_ref[...] = _mm(mix(2), wv_ref[...])
    dec = w0_ref[...] + _mm(jnp.tanh(_mm(mix(3), w1_ref[...])).astype(BF16), w2_ref[...])
    lw_ref[...] = -RW_DECAY_SCALE * jax.nn.sigmoid(dec)
    a = jax.nn.sigmoid(a0_ref[...] + _mm(_mm(mix(4), a1_ref[...]).astype(BF16), a2_ref[...]))
    g_ref[...] = _mm(jax.nn.sigmoid(_mm(mix(5), g1_ref[...])).astype(BF16), g2_ref[...])
    ones = _head_ones(128, RW_HEAD_DIM)
    kn = k * kk_ref[...]
    d = kn.shape[1]
    for j in range(d // 128):
        sl = slice(j * 128, (j + 1) * 128)
        knj = kn[:, sl]
        knj = knj * lax.rsqrt(_head_sum(knj * knj, ones) + 1e-6)
        kn_ref[:, sl] = knj
        b_ref[:, sl] = knj * a[:, sl]
    k_ref[...] = k * (1.0 + (a - 1.0) * ka_ref[...])


def rw_proj(hn, prev, p, *, tm=256):
    n, d = hn.shape
    tm = _row_tile(n, tm)
    row = lambda i: (i, 0)
    fix = lambda i: (0, 0)
    big = pl.BlockSpec((tm, d), row)
    vec = pl.BlockSpec((1, d), fix)

    def full(a):
        return pl.BlockSpec(a.shape, fix)

    ws = [p['rw_w_rkv'][0].astype(BF16), p['rw_w_rkv'][1].astype(BF16), p['rw_w_rkv'][2].astype(BF16),
          p['rw_w1'].astype(BF16), p['rw_w2'].astype(BF16), p['rw_a1'].astype(BF16), p['rw_a2'].astype(BF16),
          p['rw_g1'].astype(BF16), p['rw_g2'].astype(BF16)]
    vecs = [p['rw_w0'].reshape(1, d), p['rw_a0'].reshape(1, d), p['rw_k_k'].reshape(1, d),
            p['rw_k_a'].reshape(1, d)]
    return pl.pallas_call(
        _rw_proj_kernel,
        out_shape=tuple(jax.ShapeDtypeStruct((n, d), F32) for _ in range(7)),
        grid=(pl.cdiv(n, tm),),
        in_specs=[big, big, full(p['rw_mu'])] + [full(w) for w in ws] + [vec] * 4,
        out_specs=tuple(big for _ in range(7)),
        compiler_params=_cparams("parallel"),
        name="rw_proj",
    )(hn, prev, p['rw_mu'], *ws, *vecs)


def _rw_scan_kernel(r_ref, lw_ref, k_ref, v_ref, kn_ref, b_ref, g_ref, s0_ref, rk_ref, lnw_ref, lnb_ref,
                    o_ref, s_ref, *, q, t_len):
    c = pl.program_id(1)

    @pl.when(c == 0)
    def _():
        s_ref[0] = s0_ref[0]

    rows = c * q + lax.broadcasted_iota(jnp.int32, (q, 1), 0)
    valid = rows < t_len
    incl = _tri(q)
    strict = _tri(q, strict=True)
    lw_all = jnp.where(valid, lw_ref[0], 0.0)
    cum_all = _dot_hi(incl.astype(F32), lw_all)
    ones = _head_ones(128, RW_HEAD_DIM)
    lo = lax.broadcasted_iota(jnp.int32, (q, 128), 1) < RW_HEAD_DIM
    blockdiag = ((lax.broadcasted_iota(jnp.int32, (128, 128), 0) < RW_HEAD_DIM)
                 == (lax.broadcasted_iota(jnp.int32, (128, 128), 1) < RW_HEAD_DIM))
    d = lw_all.shape[1]
    pairs = range(d // 128)
    sls = [slice(p * 128, (p + 1) * 128) for p in pairs]
    rs, ks, vs, bends, kends, s_old, cum_ends = [], [], [], [], [], [], []
    pk, pb, u0, y0 = [], [], [], []
    for p in pairs:
        sl = sls[p]
        lw = lw_all[:, sl]
        cum = cum_all[:, sl]
        cum_end = cum[q - 1:q, :]
        r = jnp.where(valid, r_ref[0, :, sl], 0.0)
        k = jnp.where(valid, k_ref[0, :, sl], 0.0)
        v = jnp.where(valid, v_ref[0, :, sl], 0.0)
        kn = jnp.where(valid, kn_ref[0, :, sl], 0.0)
        b = jnp.where(valid, b_ref[0, :, sl], 0.0)
        kq = kn * jnp.exp(cum - lw)
        rq = r * jnp.exp(cum)
        einv = jnp.exp(-cum)
        kd = k * einv
        bd = b * einv
        eend = jnp.exp(cum_end - cum)
        s2 = s_ref[0, sl, :]
        lhs = jnp.concatenate([jnp.where(lo, kq, 0.0), jnp.where(lo, 0.0, kq),
                               jnp.where(lo, rq, 0.0), jnp.where(lo, 0.0, rq)], axis=0)
        pk.append(_dot_nt(lhs, kd))
        pb.append(_dot_nt(lhs, bd))
        u0.append(_dot_nt(kq, s2))
        y0.append(_dot_nt(rq, s2))
        rs.append(r)
        ks.append(k)
        vs.append(v)
        kends.append(k * eend)
        bends.append(b * eend)
        s_old.append(s2)
        cum_ends.append(cum_end)
    ab = [jnp.where(strict, pb[p][e * q:(e + 1) * q], 0.0) for p in pairs for e in range(2)]
    rhs = [u0[p] + jnp.where(lo, _dot(jnp.where(strict, pk[p][0:q], 0.0), vs[p]),
                             _dot(jnp.where(strict, pk[p][q:2 * q], 0.0), vs[p])) for p in pairs]

    def per_head(ms, vecs, dot):
        return [jnp.where(lo, dot(ms[2 * p], vecs[p]), dot(ms[2 * p + 1], vecs[p])) for p in pairs]

    us = _solve_unit_lower(ab, rhs, per_head)
    ys = []
    for p in pairs:
        rk = [jnp.where(incl, pk[p][(2 + e) * q:(3 + e) * q], 0.0) for e in range(2)]
        rb = [jnp.where(incl, pb[p][(2 + e) * q:(3 + e) * q], 0.0) for e in range(2)]
        ys.append(y0[p] + jnp.where(lo, _dot(rk[0], vs[p]) - _dot(rb[0], us[p]),
                                    _dot(rk[1], vs[p]) - _dot(rb[1], us[p])))
    s_new = [s_old[p] * jnp.exp(cum_ends[p]) + _dot_tn(vs[p], kends[p]) - _dot_tn(us[p], bends[p])
             for p in pairs]
    for p in pairs:
        sl = sls[p]
        s_ref[0, sl, :] = jnp.where(blockdiag, s_new[p], 0.0)
        y = ys[p]
        mean = _head_sum(y, ones) * (1.0 / RW_HEAD_DIM)
        yc = y - mean
        var = _head_sum(yc * yc, ones) * (1.0 / RW_HEAD_DIM)
        yn = yc * lax.rsqrt(var + RW_GN_EPS) * lnw_ref[:, sl] + lnb_ref[:, sl]
        bonus = _head_sum(rs[p] * ks[p] * rk_ref[:, sl], ones) * vs[p]
        o_ref[0, :, sl] = ((yn + bonus) * g_ref[0, :, sl]).astype(o_ref.dtype)


def rw_scan(r, lw, k, v, kn, b, g, t_valid, s0, r_k, ln_w, ln_b, *, chunk=64):
    bsz, t_len, d = r.shape
    q = min(chunk, t_len)
    assert q % 8 == 0 and t_len % q == 0 and d % 128 == 0
    n_heads = d // RW_HEAD_DIM
    s5 = s0.reshape(bsz, n_heads // 2, 2, RW_HEAD_DIM, RW_HEAD_DIM)
    s2 = jnp.einsum('bpevk,ef->bpevfk', s5, jnp.eye(2, dtype=F32)).reshape(bsz, d, 128)
    blk = pl.BlockSpec((1, q, d), lambda bb, c: (bb, c, 0))
    vec = pl.BlockSpec((1, d), lambda bb, c: (0, 0))
    st = pl.BlockSpec((1, d, 128), lambda bb, c: (bb, 0, 0))
    o, s_new = pl.pallas_call(
        functools.partial(_rw_scan_kernel, q=q, t_len=t_valid),
        out_shape=(jax.ShapeDtypeStruct((bsz, t_len, d), BF16), jax.ShapeDtypeStruct((bsz, d, 128), F32)),
        grid=(bsz, t_len // q),
        in_specs=[blk] * 7 + [st, vec, vec, vec],
        out_specs=(blk, st),
        compiler_params=_cparams("parallel", "arbitrary"),
        name="rw_scan",
    )(r, lw, k, v, kn, b, g, s2, r_k.reshape(1, d), ln_w.reshape(1, d), ln_b.reshape(1, d))
    s6 = s_new.reshape(bsz, n_heads // 2, 2, RW_HEAD_DIM, 2, RW_HEAD_DIM)
    s_out = jnp.stack([s6[:, :, 0, :, 0, :], s6[:, :, 1, :, 1, :]], axis=2)
    return o, s_out.reshape(bsz, n_heads, RW_HEAD_DIM, RW_HEAD_DIM)


def rwkv_mixer(x, t_valid, shift_buf, s0, norm_g, p):
    bsz, t_len, d = x.shape
    hn = rmsnorm_rows(x.reshape(bsz * t_len, d), norm_g).reshape(bsz, t_len, d)
    prev = jnp.concatenate([shift_buf, hn[:, :-1]], axis=1)
    outs = rw_proj(hn.reshape(-1, d), prev.reshape(-1, d), p)
    r, lw, k, v, kn, b, g = (a.reshape(bsz, t_len, d) for a in outs)
    yg, s_new = rw_scan(r, lw, k, v, kn, b, g, t_valid, s0, p['rw_r_k'], p['rw_ln_w'], p['rw_ln_b'])
    return yg.reshape(bsz * t_len, d), hn[:, t_valid - 1:t_valid], s_new


def _dot3_nt(a, b):
    ah, al = _split(a)
    bh, bl = _split(b)
    dn = (((1,), (1,)), ((), ()))
    f = lambda u, w: lax.dot_general(u, w, dn, preferred_element_type=F32)
    return f(ah, bh) + (f(ah, bl) + f(al, bh))


def _suffix_ones(n):
    r = lax.broadcasted_iota(jnp.int32, (n, n), 0)
    c = lax.broadcasted_iota(jnp.int32, (n, n), 1)
    return jnp.where(r >= c, 1.0, 0.0).astype(BF16)


def _sb_block(q_e, kblk, vblk, carry, acc, lo, suffix, mask):
    outs = []
    new_carry = []
    for e in range(2):
        z = _dot_nt(q_e[e], kblk)
        lnb = -_softplus(z)
        if mask is not None:
            lnb = jnp.where(mask, lnb, 0.0)
        hi, lw = _split(lnb)
        rsum = _mm(hi, suffix) + _mm(lw, suffix) + carry[e]
        att = jnp.exp(z + rsum)
        if mask is not None:
            att = jnp.where(mask, att, 0.0)
        outs.append(_dot(att, vblk))
        new_carry.append(rsum[:, 0:1])
    return new_carry, acc + jnp.where(lo, outs[0], outs[1])


def _sb_kernel(q_ref, kn_ref, vn_ref, o_ref, kmax_ref, *, bq, n_pairs, scale):
    i = pl.program_id(2)
    t_new = kn_ref.shape[1]
    width = 128 * n_pairs
    pairs = range(n_pairs)

    @pl.when(i == 0)
    def _():
        def body(j, m):
            rows = kn_ref[0, pl.ds(pl.multiple_of(j * bq, bq), bq), :]
            return jnp.maximum(m, jnp.max(jnp.abs(rows).reshape(bq // 8, 8, width), axis=0))

        m8 = lax.fori_loop(0, t_new // bq, body, jnp.zeros((8, width), F32))
        kmax_ref[...] = jnp.max(m8, axis=0, keepdims=True)

    lo = lax.broadcasted_iota(jnp.int32, (bq, 128), 1) < SB_HEAD_DIM
    ones = _head_ones(128, SB_HEAD_DIM)
    q_e, zbs = [], []
    for p in pairs:
        q = q_ref[0, :, p * 128:(p + 1) * 128] * scale
        q_e.append([jnp.where(lo, q, 0.0), jnp.where(lo, 0.0, q)])
        zb = _head_sum(jnp.abs(q) * kmax_ref[:, p * 128:(p + 1) * 128], ones)
        zbs += [zb[:, 0:1], zb[:, SB_HEAD_DIM:SB_HEAD_DIM + 1]]
    suffix = _suffix_ones(bq)

    def block(off, carries, accs, mask):
        new_c, new_a = [], []
        for p in pairs:
            c, a = _sb_block(q_e[p], kn_ref[0, pl.ds(off, bq), p * 128:(p + 1) * 128],
                             vn_ref[0, pl.ds(off, bq), p * 128:(p + 1) * 128],
                             carries[2 * p:2 * p + 2], accs[p], lo, suffix, mask)
            new_c += c
            new_a.append(a)
        return new_c, new_a

    def live(carries):
        worst = carries[0] + zbs[0]
        for c, zb in zip(carries[1:], zbs[1:]):
            worst = jnp.maximum(worst, c + zb)
        return (jnp.max(worst) > SB_LOG_CUT).astype(jnp.int32)

    zero_c = jnp.zeros((bq, 1), F32)
    carries, accs = block(pl.multiple_of(i * bq, bq), [zero_c] * (2 * n_pairs),
                          [jnp.zeros((bq, 128), F32)] * n_pairs, _tri(bq, strict=True))

    def cond(s):
        return (s[0] < i) & (s[1] > 0)

    def body(s):
        jj, _, cs, acs = s
        cs, acs = block(pl.multiple_of((i - 1 - jj) * bq, bq), list(cs), list(acs), None)
        return jj + 1, live(cs), tuple(cs), tuple(acs)

    st = lax.while_loop(cond, body, (jnp.int32(0), live(carries), tuple(carries), tuple(accs)))
    for p in pairs:
        o_ref[0, :, p * 128:(p + 1) * 128] = st[3][p].astype(o_ref.dtype)


def sb_attention(qkv, *, block=256, pairs_per_step=2):
    bsz, t_len, d3 = qkv.shape
    d = d3 // 3
    width = 128 * pairs_per_step
    assert d % width == 0
    n_groups = d // width
    bq = min(block, t_len)
    assert t_len % bq == 0 and bq % 8 == 0

    def kv_spec(which):
        return pl.BlockSpec((1, t_len, width), lambda b, p, i: (b, 0, which * n_groups + p),
                            pipeline_mode=pl.Buffered(1))

    return pl.pallas_call(
        functools.partial(_sb_kernel, bq=bq, n_pairs=pairs_per_step, scale=SB_HEAD_DIM ** -0.5),
        out_shape=jax.ShapeDtypeStruct((bsz, t_len, d), BF16),
        grid=(bsz, n_groups, t_len // bq),
        in_specs=[pl.BlockSpec((1, bq, width), lambda b, p, i: (b, i, p)), kv_spec(1), kv_spec(2)],
        out_specs=pl.BlockSpec((1, bq, width), lambda b, p, i: (b, i, p)),
        compiler_params=_cparams("parallel", "parallel", "arbitrary"),
        scratch_shapes=[pltpu.VMEM((1, width), F32)],
        name="sb_attention",
    )(qkv, qkv, qkv)


def _sb_decode_kernel(qkv_ref, kc_ref, vc_hbm, o_ref, q_s, acc_s, carry_s, v_buf, v_sem, *, t, n_heads, bp, scale):
    j = pl.program_id(1)
    hd = SB_HEAD_DIM
    d = n_heads * hd
    rows = n_heads * t
    heads = range(n_heads)

    def scores(keys):
        return jnp.concatenate([_dot3_nt(q_s[h * t:(h + 1) * t, :], keys(h)) for h in heads], axis=0)

    def weights(z, carry, suffix, mask):
        lnb = -_softplus(z)
        if mask is not None:
            lnb = jnp.where(mask, lnb, 0.0)
        hi, lw = _split(lnb)
        rsum = _mm(hi, suffix) + _mm(lw, suffix) + carry
        att = jnp.exp(z + rsum)
        if mask is not None:
            att = jnp.where(mask, att, 0.0)
        return att, rsum[:, 0:1]

    @pl.when(j == 0)
    def _():
        for h in heads:
            q_s[h * t:(h + 1) * t, :] = qkv_ref[0, :, h * hd:(h + 1) * hd] * scale
        z = scores(lambda h: qkv_ref[0, :, d + h * hd:d + (h + 1) * hd])
        qi = lax.rem(lax.broadcasted_iota(jnp.int32, (rows, t), 0), t)
        mask = lax.broadcasted_iota(jnp.int32, (rows, t), 1) < qi
        att, carry = weights(z, 0.0, _suffix_ones(t), mask)
        for h in heads:
            acc_s[h * t:(h + 1) * t, :] = _dot(att[h * t:(h + 1) * t],
                                               qkv_ref[0, :, 2 * d + h * hd:2 * d + (h + 1) * hd])
        carry_s[...] = carry

    ub = jnp.concatenate([_dot(jnp.abs(q_s[h * t:(h + 1) * t, :]), jnp.abs(kc_ref[0, h])) for h in heads], axis=0)
    zb = SB_BOUND_SLACK * jnp.max(ub, axis=-1, keepdims=True)

    @pl.when(jnp.max(carry_s[...] + zb) > SB_LOG_CUT)
    def _():
        off = pl.multiple_of((pl.num_programs(1) - 1 - j) * bp, bp)
        v_copy = pltpu.make_async_copy(vc_hbm.at[pl.program_id(0), :, :, pl.ds(off, bp)], v_buf, v_sem)
        v_copy.start()
        z = jnp.concatenate([_dot3(q_s[h * t:(h + 1) * t, :], kc_ref[0, h]) for h in heads], axis=0)
        att, carry = weights(z, carry_s[...], _suffix_ones(bp), None)
        v_copy.wait()
        for h in heads:
            acc_s[h * t:(h + 1) * t, :] += _dot_nt(att[h * t:(h + 1) * t], v_buf[h])
        carry_s[...] = carry

    @pl.when(j == pl.num_programs(1) - 1)
    def _():
        o_ref[0] = jnp.concatenate([acc_s[h * t:(h + 1) * t, :] for h in heads], axis=1).astype(o_ref.dtype)


def sb_decode(qkv, k_cache, v_cache, *, block=512):
    bsz, t_len, d3 = qkv.shape
    _, past_len, n_heads, hd = k_cache.shape
    assert hd == SB_HEAD_DIM and n_heads * hd * 3 == d3 and t_len % 8 == 0
    bp = min(block, past_len)
    assert past_len % bp == 0
    n_blk = past_len // bp
    k_cache = jnp.transpose(k_cache, (0, 2, 3, 1))
    v_cache = jnp.transpose(v_cache, (0, 2, 3, 1))
    cache_spec = pl.BlockSpec((1, n_heads, hd, bp), lambda b, j: (b, 0, 0, n_blk - 1 - j))
    return pl.pallas_call(
        functools.partial(_sb_decode_kernel, t=t_len, n_heads=n_heads, bp=bp, scale=hd ** -0.5),
        out_shape=jax.ShapeDtypeStruct((bsz, t_len, n_heads * hd), BF16),
        grid=(bsz, n_blk),
        in_specs=[pl.BlockSpec((1, t_len, d3), lambda b, j: (b, 0, 0)), cache_spec,
                  pl.BlockSpec(memory_space=pl.ANY)],
        out_specs=pl.BlockSpec((1, t_len, n_heads * hd), lambda b, j: (b, 0, 0)),
        scratch_shapes=[pltpu.VMEM((n_heads * t_len, hd), F32), pltpu.VMEM((n_heads * t_len, hd), F32),
                        pltpu.VMEM((n_heads * t_len, 1), F32), pltpu.VMEM((n_heads, hd, bp), F32),
                        pltpu.SemaphoreType.DMA(())],
        compiler_params=_cparams("parallel", "arbitrary"),
        name="sb_decode",
    )(qkv, k_cache, v_cache)


def sb_mixer(x, t_valid, k_past, v_past, norm_g, w_qkv):
    bsz, t_len, d = x.shape
    qkv = norm_matmul(x.reshape(bsz * t_len, d), norm_g, w_qkv).reshape(bsz, t_len, 3 * d)
    o = sb_attention(qkv) if k_past is None else sb_decode(qkv, k_past, v_past)
    n_heads = d // SB_HEAD_DIM
    k_new = qkv[:, :t_valid, d:2 * d].reshape(bsz, t_valid, n_heads, SB_HEAD_DIM)
    v_new = qkv[:, :t_valid, 2 * d:].reshape(bsz, t_valid, n_heads, SB_HEAD_DIM)
    return o.reshape(bsz * t_len, d), k_new, v_new


def _run_trunk(x, t_valid, st, p):
    bsz, t_len, d = x.shape
    new = {}
    bf = lambda a: a.astype(BF16)

    def finish_layer(x, a, w_out, i):
        return mix_ffn(a, bf(w_out), x.reshape(bsz * t_len, d), p['norm_ffn'][i], bf(p['ffn_w_gu'][i]),
                       bf(p['ffn_w_down'][i])).reshape(bsz, t_len, d)

    a, new['ssm_conv'], new['ssm'] = mamba2_mixer(
        x, t_valid, st['ssm_conv'], st['ssm'], p['norm_mix'][0], bf(p['mb_w_in']), p['mb_conv_w'],
        p['mb_conv_b'], p['mb_dt_bias'], p['mb_a_log'], p['mb_d'], p['mb_norm'])
    x = finish_layer(x, a, p['mb_w_out'], 0)
    a, new['gdn_conv'], new['gdn'] = gdn_mixer(
        x, t_valid, st['gdn_conv'], st['gdn'], p['norm_mix'][1], bf(p['gdn_w_in']), p['gdn_conv_w'],
        p['gdn_dt_bias'], p['gdn_a_log'], p['gdn_norm'])
    x = finish_layer(x, a, p['gdn_w_out'], 1)
    a, new['rwkv_shift'], new['rwkv'] = rwkv_mixer(x, t_valid, st['rwkv_shift'], st['rwkv'], p['norm_mix'][2], p)
    x = finish_layer(x, a, p['rw_w_out'], 2)
    a, new['sb_k'], new['sb_v'] = sb_mixer(x, t_valid, st['sb_k'], st['sb_v'], p['norm_mix'][3], bf(p['sb_w_qkv']))
    x = finish_layer(x, a, p['sb_w_out'], 3)
    y = rmsnorm_rows(x.reshape(bsz * t_len, d), p['norm_final']).reshape(bsz, t_len, d)
    return y, new


def kernel(x_prompt, x_sample, state_ssm, state_ssm_conv, state_gdn, state_gdn_conv, state_rwkv, state_rwkv_shift, cache_sb_k, cache_sb_v, meta_tokens, norm_mix, norm_ffn, norm_final, ffn_w_gu, ffn_w_down, mb_w_in, mb_conv_w, mb_conv_b, mb_dt_bias, mb_a_log, mb_d, mb_norm, mb_w_out, gdn_w_in, gdn_conv_w, gdn_dt_bias, gdn_a_log, gdn_norm, gdn_w_out, rw_mu, rw_w_rkv, rw_w0, rw_w1, rw_w2, rw_a0, rw_a1, rw_a2, rw_g1, rw_g2, rw_k_k, rw_k_a, rw_r_k, rw_ln_w, rw_ln_b, rw_w_out, sb_w_qkv, sb_w_out):
    p = dict(
        norm_mix=norm_mix, norm_ffn=norm_ffn, norm_final=norm_final, ffn_w_gu=ffn_w_gu, ffn_w_down=ffn_w_down,
        mb_w_in=mb_w_in, mb_conv_w=mb_conv_w, mb_conv_b=mb_conv_b, mb_dt_bias=mb_dt_bias, mb_a_log=mb_a_log,
        mb_d=mb_d, mb_norm=mb_norm, mb_w_out=mb_w_out,
        gdn_w_in=gdn_w_in, gdn_conv_w=gdn_conv_w, gdn_dt_bias=gdn_dt_bias, gdn_a_log=gdn_a_log,
        gdn_norm=gdn_norm, gdn_w_out=gdn_w_out,
        rw_mu=rw_mu, rw_w_rkv=rw_w_rkv, rw_w0=rw_w0, rw_w1=rw_w1, rw_w2=rw_w2, rw_a0=rw_a0, rw_a1=rw_a1,
        rw_a2=rw_a2, rw_g1=rw_g1, rw_g2=rw_g2, rw_k_k=rw_k_k, rw_k_a=rw_k_a, rw_r_k=rw_r_k, rw_ln_w=rw_ln_w,
        rw_ln_b=rw_ln_b, rw_w_out=rw_w_out, sb_w_qkv=sb_w_qkv, sb_w_out=sb_w_out)
    bsz, seq, d = x_prompt.shape
    n_meta = meta_tokens.shape[0]
    t_valid = n_meta + seq
    t_pad = -(-t_valid // PROMPT_ROW_ALIGN) * PROMPT_ROW_ALIGN
    meta = jnp.broadcast_to(meta_tokens[None], (bsz, n_meta, d))
    x0 = jnp.concatenate([meta, x_prompt, jnp.zeros((bsz, t_pad - t_valid, d), x_prompt.dtype)], axis=1)
    fresh = dict(
        ssm=jnp.zeros((bsz,) + state_ssm.shape[1:], F32), ssm_conv=jnp.zeros((bsz,) + state_ssm_conv.shape[1:], F32),
        gdn=jnp.zeros((bsz,) + state_gdn.shape[1:], F32), gdn_conv=jnp.zeros((bsz,) + state_gdn_conv.shape[1:], F32),
        rwkv=jnp.zeros((bsz,) + state_rwkv.shape[1:], F32),
        rwkv_shift=jnp.zeros((bsz,) + state_rwkv_shift.shape[1:], F32), sb_k=None, sb_v=None)
    y_full, sp = _run_trunk(x0, t_valid, fresh, p)
    y_prompt = y_full[:, n_meta:t_valid]
    past = dict(ssm=state_ssm, ssm_conv=state_ssm_conv, gdn=state_gdn, gdn_conv=state_gdn_conv,
                rwkv=state_rwkv, rwkv_shift=state_rwkv_shift, sb_k=cache_sb_k, sb_v=cache_sb_v)
    y_sample, ss = _run_trunk(x_sample, x_sample.shape[1], past, p)
    return (y_prompt, y_sample,
            sp['ssm'], sp['ssm_conv'], sp['gdn'], sp['gdn_conv'], sp['rwkv'], sp['rwkv_shift'],
            sp['sb_k'], sp['sb_v'],
            ss['ssm'], ss['ssm_conv'], ss['gdn'], ss['gdn_conv'], ss['rwkv'], ss['rwkv_shift'],
            ss['sb_k'], ss['sb_v'])
```

```python
import functools
import math

import jax
import jax.numpy as jnp
from jax import lax
from jax.experimental import pallas as pl
from jax.experimental.pallas import tpu as pltpu

F32 = jnp.float32
BF16 = jnp.bfloat16
EPS = 1e-6
VMEM_LIMIT = 48 * 1024 * 1024
HI = lax.Precision.HIGHEST

PROMPT_ROW_ALIGN = 256
MB_HEAD_DIM = 64
MB_D_STATE = 128
MB_GROUPS = 4
GDN_DK = 128
GDN_DV = 128
RW_HEAD_DIM = 64
RW_GN_EPS = 64e-5
RW_DECAY_SCALE = 0.6065306597126334
SB_HEAD_DIM = 64
SB_LOG_CUT = -100.0
SB_BOUND_SLACK = 1.01


def _cparams(*sem):
    return pltpu.CompilerParams(dimension_semantics=sem, vmem_limit_bytes=VMEM_LIMIT)


def _dot(a, b):
    return jnp.dot(a.astype(BF16), b.astype(BF16), preferred_element_type=F32)


def _dot_nt(a, b):
    return lax.dot_general(a.astype(BF16), b.astype(BF16), (((1,), (1,)), ((), ())),
                           preferred_element_type=F32)


def _dot_tn(a, b):
    return lax.dot_general(a.astype(BF16), b.astype(BF16), (((0,), (0,)), ((), ())),
                           preferred_element_type=F32)


def _dot_hi(a, b):
    return jnp.dot(a, b, preferred_element_type=F32, precision=HI)


def _silu(x):
    return x * jax.nn.sigmoid(x)


def _softplus(x):
    return jnp.maximum(x, 0.0) + jnp.log(1.0 + jnp.exp(-jnp.abs(x)))


def _rmsnorm(x, g):
    return x * lax.rsqrt(jnp.mean(x * x, axis=-1, keepdims=True) + EPS) * g


def _row_tile(n, want):
    return n if n <= want else want


def _resident(shape):
    zeros = (0,) * len(shape)
    return pl.BlockSpec(shape, lambda *_: zeros, pipeline_mode=pl.Buffered(1))


def _norm_matmul_kernel(x_ref, g_ref, w_ref, o_ref):
    h = _rmsnorm(x_ref[...], g_ref[...]).astype(BF16)
    o_ref[...] = jnp.dot(h, w_ref[...], preferred_element_type=F32)


def norm_matmul(x, g, w, *, tm=256):
    n, d = x.shape
    m = w.shape[1]
    tm = _row_tile(n, tm)
    return pl.pallas_call(
        _norm_matmul_kernel,
        out_shape=jax.ShapeDtypeStruct((n, m), F32),
        grid=(pl.cdiv(n, tm),),
        in_specs=[pl.BlockSpec((tm, d), lambda i: (i, 0)), _resident((1, d)), _resident((d, m))],
        out_specs=pl.BlockSpec((tm, m), lambda i: (i, 0)),
        compiler_params=_cparams("parallel"),
        name="norm_matmul",
    )(x, g.reshape(1, d), w)


def _mix_ffn_kernel(a_ref, wo_ref, x_ref, g_ref, wgu_ref, wd_ref, *rest, f, tf):
    o_ref = rest[-1]
    x = x_ref[...] + jnp.dot(a_ref[...], wo_ref[...], preferred_element_type=F32)
    h = _rmsnorm(x, g_ref[...]).astype(BF16)
    acc = x
    for c in range(f // tf):
        gate = jnp.dot(h, wgu_ref[:, c * tf:(c + 1) * tf], preferred_element_type=F32)
        up = jnp.dot(h, wgu_ref[:, f + c * tf:f + (c + 1) * tf], preferred_element_type=F32)
        act = (_silu(gate) * up).astype(BF16)
        acc = acc + jnp.dot(act, wd_ref[c * tf:(c + 1) * tf, :], preferred_element_type=F32)
    if len(rest) == 2:
        acc = _rmsnorm(acc, rest[0][...])
    o_ref[...] = acc


def mix_ffn(a, w_out, x, g, w_gu, w_down, g_out=None, *, tm=512, tf=1408):
    n, d = x.shape
    k = a.shape[1]
    f = w_down.shape[0]
    tm = _row_tile(n, tm)
    if f % tf:
        tf = f
    args = [a, w_out, x, g.reshape(1, d), w_gu, w_down]
    in_specs = [pl.BlockSpec((tm, k), lambda i: (i, 0)), _resident((k, d)),
                pl.BlockSpec((tm, d), lambda i: (i, 0)), _resident((1, d)), _resident((d, 2 * f)),
                _resident((f, d))]
    if g_out is not None:
        args.append(g_out.reshape(1, d))
        in_specs.append(_resident((1, d)))
    return pl.pallas_call(
        functools.partial(_mix_ffn_kernel, f=f, tf=tf),
        out_shape=jax.ShapeDtypeStruct((n, d), F32),
        grid=(pl.cdiv(n, tm),),
        in_specs=in_specs,
        out_specs=pl.BlockSpec((tm, d), lambda i: (i, 0)),
        compiler_params=_cparams("parallel"),
        name="mix_ffn",
    )(*args)


def _causal_conv_silu(raw, buf_ref, w_ref, bias, q):
    buf_ref[pl.ds(8, q), :] = raw
    out = buf_ref[pl.ds(5, q), :] * w_ref[0:1, :]
    out = out + buf_ref[pl.ds(6, q), :] * w_ref[1:2, :]
    out = out + buf_ref[pl.ds(7, q), :] * w_ref[2:3, :]
    out = out + raw * w_ref[3:4, :]
    if bias is not None:
        out = out + bias
    buf_ref[pl.ds(5, 3), :] = buf_ref[pl.ds(q + 5, 3), :]
    return _silu(out)


def _tri(q, strict=False):
    r = lax.broadcasted_iota(jnp.int32, (q, q), 0)
    c = lax.broadcasted_iota(jnp.int32, (q, q), 1)
    return (r > c) if strict else (r >= c)


def _mamba_kernel(z_ref, xs_ref, bc_ref, dt_ref, cx_ref, cbc_ref, h0_ref,
                  cwx_ref, cwbc_ref, cbx_ref, cbbc_ref, dtb_ref, a_ref, dskip_ref, nw_ref,
                  y_ref, h_ref, bufx_ref, bufbc_ref, *, q, t_len, n_heads):
    c = pl.program_id(1)
    n_state = MB_D_STATE
    hpg = n_heads // MB_GROUPS
    gw = hpg * MB_HEAD_DIM

    @pl.when(c == 0)
    def _():
        bufx_ref[pl.ds(5, 3), :] = cx_ref[0]
        bufbc_ref[pl.ds(5, 3), :] = cbc_ref[0]
        h_ref[0] = h0_ref[0]

    rows = c * q + lax.broadcasted_iota(jnp.int32, (q, 1), 0)
    valid = rows < t_len
    xs = _causal_conv_silu(jnp.where(valid, xs_ref[0], 0.0), bufx_ref, cwx_ref, cbx_ref[...], q)
    bc = _causal_conv_silu(jnp.where(valid, bc_ref[0], 0.0), bufbc_ref, cwbc_ref, cbbc_ref[...], q)

    lane = lax.broadcasted_iota(jnp.int32, (q, 128), 1)
    dt = _softplus(jnp.where(valid & (lane < n_heads), dt_ref[0], 0.0) + dtb_ref[...])
    dt = jnp.where(valid & (lane < n_heads), dt, 0.0)
    da = dt * a_ref[...]
    incl = _tri(q)
    acum = _dot_hi(incl.astype(F32), da)
    acum_t = acum.T
    dt_t = dt.T
    last = acum[q - 1:q, :]
    wts = jnp.exp(last - acum) * dt
    e_acum = jnp.exp(acum)
    e_last = jnp.exp(last)

    lo_half = lax.broadcasted_iota(jnp.int32, (q, 128), 1) < MB_HEAD_DIM
    lo_rows = lax.broadcasted_iota(jnp.int32, (128, 128), 0) < MB_HEAD_DIM

    z = z_ref[0]
    for g in range(MB_GROUPS):
        bm = bc[:, g * n_state:(g + 1) * n_state]
        cm = bc[:, (MB_GROUPS + g) * n_state:(MB_GROUPS + g + 1) * n_state]
        cb = _dot_nt(cm, bm)
        hg = h_ref[0, g * gw:(g + 1) * gw, :]
        ch = _dot_nt(cm, hg)
        ys = []
        for p in range(hpg // 2):
            h_e = g * hpg + 2 * p
            x_pair = xs[:, h_e * MB_HEAD_DIM:(h_e + 2) * MB_HEAD_DIM]
            outs = []
            for hh in (h_e, h_e + 1):
                seg = acum[:, hh:hh + 1] - acum_t[hh:hh + 1, :]
                decay = jnp.where(incl, jnp.exp(jnp.where(incl, seg, 0.0)), 0.0)
                m = cb * decay * dt_t[hh:hh + 1, :]
                outs.append(_dot(m, x_pair))
            y_pair = jnp.where(lo_half, outs[0], outs[1])
            e_pair = jnp.where(lo_half, e_acum[:, h_e:h_e + 1], e_acum[:, h_e + 1:h_e + 2])
            y_pair = y_pair + e_pair * ch[:, 2 * p * MB_HEAD_DIM:(2 * p + 2) * MB_HEAD_DIM]
            ys.append(y_pair)
            w_pair = jnp.where(lo_half, wts[:, h_e:h_e + 1], wts[:, h_e + 1:h_e + 2])
            upd = _dot_tn(x_pair * w_pair, bm)
            scale = jnp.where(lo_rows, e_last[:, h_e:h_e + 1], e_last[:, h_e + 1:h_e + 2])
            r0 = h_e * MB_HEAD_DIM
            h_ref[0, r0:r0 + 128, :] = h_ref[0, r0:r0 + 128, :] * scale + upd
        yg = jnp.concatenate(ys, axis=1)
        xg = xs[:, g * gw:(g + 1) * gw]
        yg = yg + xg * dskip_ref[:, g * gw:(g + 1) * gw]
        yg = yg * _silu(z[:, g * gw:(g + 1) * gw])
        yg = yg * lax.rsqrt(jnp.mean(yg * yg, axis=-1, keepdims=True) + EPS)
        y_ref[0, :, g * gw:(g + 1) * gw] = (yg * nw_ref[:, g * gw:(g + 1) * gw]).astype(y_ref.dtype)


def mamba_core(proj, t_valid, conv_buf, h0, conv_w, conv_b, dt_bias, a_log, d_skip, norm_w, *, chunk=128):
    bsz, t_len, _ = proj.shape
    n_heads = dt_bias.shape[0]
    d_inner = n_heads * MB_HEAD_DIM
    gn = MB_GROUPS * MB_D_STATE
    assert d_inner % 1024 == 0 and 2 * gn == 1024 and n_heads <= 128
    q = min(chunk, t_len)
    assert q % 8 == 0 and t_len % q == 0
    n_chunks = t_len // q
    dtb = jnp.zeros((1, 128), F32).at[0, :n_heads].set(dt_bias)
    a_row = jnp.zeros((1, 128), F32).at[0, :n_heads].set(-jnp.exp(a_log))
    dskip = jnp.repeat(d_skip, MB_HEAD_DIM).reshape(1, d_inner)
    kern = functools.partial(_mamba_kernel, q=q, t_len=t_valid, n_heads=n_heads)
    xblk = d_inner // 1024
    y, h = pl.pallas_call(
        kern,
        out_shape=(jax.ShapeDtypeStruct((bsz, t_len, d_inner), BF16),
                   jax.ShapeDtypeStruct((bsz, n_heads * MB_HEAD_DIM, MB_D_STATE), F32)),
        grid=(bsz, n_chunks),
        in_specs=[
            pl.BlockSpec((1, q, d_inner), lambda b, c: (b, c, 0)),
            pl.BlockSpec((1, q, d_inner), lambda b, c: (b, c, 1)),
            pl.BlockSpec((1, q, 2 * gn), lambda b, c: (b, c, 2 * xblk)),
            pl.BlockSpec((1, q, 128), lambda b, c: (b, c, (2 * d_inner + 2 * gn) // 128)),
            pl.BlockSpec((1, 3, d_inner), lambda b, c: (b, 0, 0)),
            pl.BlockSpec((1, 3, 2 * gn), lambda b, c: (b, 0, xblk)),
            pl.BlockSpec((1, n_heads * MB_HEAD_DIM, MB_D_STATE), lambda b, c: (b, 0, 0)),
            pl.BlockSpec((4, d_inner), lambda b, c: (0, 0)),
            pl.BlockSpec((4, 2 * gn), lambda b, c: (0, xblk)),
            pl.BlockSpec((1, d_inner), lambda b, c: (0, 0)),
            pl.BlockSpec((1, 2 * gn), lambda b, c: (0, xblk)),
            _resident((1, 128)), _resident((1, 128)), _resident((1, d_inner)), _resident((1, d_inner)),
        ],
        out_specs=(pl.BlockSpec((1, q, d_inner), lambda b, c: (b, c, 0)),
                   pl.BlockSpec((1, n_heads * MB_HEAD_DIM, MB_D_STATE), lambda b, c: (b, 0, 0))),
        scratch_shapes=[pltpu.VMEM((q + 8, d_inner), F32), pltpu.VMEM((q + 8, 2 * gn), F32)],
        compiler_params=_cparams("parallel", "arbitrary"),
        name="mamba_core",
    )(proj, proj, proj, proj, conv_buf, conv_buf, h0.reshape(bsz, n_heads * MB_HEAD_DIM, MB_D_STATE),
      conv_w, conv_w, conv_b.reshape(1, -1), conv_b.reshape(1, -1), dtb, a_row, dskip,
      norm_w.reshape(1, d_inner))
    return y, h.reshape(bsz, n_heads, MB_HEAD_DIM, MB_D_STATE)


def mamba2_mixer(x, t_valid, conv_buf, ssm_state, norm_g, w_in, conv_w, conv_b, dt_bias, a_log, d_skip, norm_w):
    bsz, t_len, d = x.shape
    d_inner = dt_bias.shape[0] * MB_HEAD_DIM
    proj = norm_matmul(x.reshape(bsz * t_len, d), norm_g, w_in).reshape(bsz, t_len, -1)
    y, h_new = mamba_core(proj, t_valid, conv_buf, ssm_state, conv_w, conv_b, dt_bias, a_log, d_skip, norm_w)
    conv_new = proj[:, t_valid - 3:t_valid, d_inner:d_inner + conv_w.shape[1]]
    return y.reshape(bsz * t_len, d_inner), conv_new, h_new


def _split(a):
    hi = a.astype(BF16)
    lo = (a - hi.astype(F32)).astype(BF16)
    return hi, lo


def _mm(a, b):
    return jnp.dot(a, b, preferred_element_type=F32)


def _dot3(a, b):
    ah, al = _split(a)
    bh, bl = _split(b)
    return _mm(ah, bh) + (_mm(ah, bl) + _mm(al, bh))


INV_BASE = 8


def _unit_lower_inverse(mats):
    n = mats[0].shape[0]
    r = lax.broadcasted_iota(jnp.int32, (n, n), 0)
    c = lax.broadcasted_iota(jnp.int32, (n, n), 1)
    eye = jnp.where(r == c, 1.0, 0.0)
    same = (r // INV_BASE) == (c // INV_BASE)
    diag = [jnp.where(same, a, 0.0) for a in mats]
    xs = [eye - dg for dg in diag]
    ps = diag
    k = 1
    while 2 * k < INV_BASE:
        ps = [_dot(p, p) for p in ps]
        xs = [x + _dot(x, p) for x, p in zip(xs, ps)]
        k *= 2
    w = INV_BASE
    while w < n:
        off = ((r // (2 * w)) == (c // (2 * w))) != ((r // w) == (c // w))
        ts = [_dot(jnp.where(off, a, 0.0), x) for a, x in zip(mats, xs)]
        xs = [x - _dot(x, t) for x, t in zip(xs, ts)]
        w *= 2
    return xs


def _solve_unit_lower(mats, rhs, apply):
    x_inv = _unit_lower_inverse(mats)
    d0 = apply(x_inv, rhs, _dot)
    t0 = apply(mats, d0, _dot3)
    res = [b - d - t for b, d, t in zip(rhs, d0, t0)]
    corr = apply(x_inv, res, _dot)
    return [d + e for d, e in zip(d0, corr)]


def _gdn_kernel(q_ref, k_ref, v_ref, z_ref, ba_ref, cq_ref, ck_ref, cv_ref, s0_ref,
                cwq_ref, cwk_ref, cwv_ref, dtb_ref, al_ref, nw_ref,
                o_ref, s_ref, bufq_ref, bufk_ref, bufv_ref, *, q, t_len, n_hv, n_hk):
    c = pl.program_id(1)
    rep = n_hv // n_hk

    @pl.when(c == 0)
    def _():
        bufq_ref[pl.ds(5, 3), :] = cq_ref[0]
        bufk_ref[pl.ds(5, 3), :] = ck_ref[0]
        bufv_ref[pl.ds(5, 3), :] = cv_ref[0]
        s_ref[0] = s0_ref[0]

    rows = c * q + lax.broadcasted_iota(jnp.int32, (q, 1), 0)
    valid = rows < t_len
    qc = _causal_conv_silu(jnp.where(valid, q_ref[0], 0.0), bufq_ref, cwq_ref, None, q)
    kc = _causal_conv_silu(jnp.where(valid, k_ref[0], 0.0), bufk_ref, cwk_ref, None, q)
    vc = _causal_conv_silu(jnp.where(valid, v_ref[0], 0.0), bufv_ref, cwv_ref, None, q)

    lane = lax.broadcasted_iota(jnp.int32, (q, 128), 1)
    ba = jnp.where(valid & (lane < 2 * n_hv), ba_ref[0], 0.0)
    beta = jnp.where(valid & (lane < n_hv), jax.nn.sigmoid(ba), 0.0)
    g = jnp.where(valid & (lane >= n_hv) & (lane < 2 * n_hv),
                  -jnp.exp(al_ref[...]) * _softplus(ba + dtb_ref[...]), 0.0)
    incl = _tri(q)
    strict = _tri(q, strict=True)
    gcum = _dot_hi(incl.astype(F32), g)
    gcum_t = gcum.T
    eg = jnp.exp(gcum)
    glast = gcum[q - 1:q, :]
    w_last = jnp.exp(glast - gcum)
    e_last = jnp.exp(glast)

    z = z_ref[0]
    qn, kn, kk, qk0 = [], [], [], []
    for kh in range(n_hk):
        qh = qc[:, kh * GDN_DK:(kh + 1) * GDN_DK]
        kh_ = kc[:, kh * GDN_DK:(kh + 1) * GDN_DK]
        qh = qh * (lax.rsqrt(jnp.sum(qh * qh, axis=-1, keepdims=True) + 1e-6) * (GDN_DK ** -0.5))
        kh_ = kh_ * lax.rsqrt(jnp.sum(kh_ * kh_, axis=-1, keepdims=True) + 1e-6)
        qn.append(qh)
        kn.append(kh_)
        kk.append(_dot_nt(kh_, kh_))
        qk0.append(_dot_nt(qh, kh_))
    heads = range(n_hv)
    s_old = [s_ref[0, h * GDN_DK:(h + 1) * GDN_DK, :] for h in heads]
    ks = [_dot(kn[h // rep], s_old[h]) for h in heads]
    qs = [_dot(qn[h // rep], s_old[h]) for h in heads]
    dec, a_mat, rhs = [], [], []
    for h in heads:
        hl = n_hv + h
        seg = gcum[:, hl:hl + 1] - gcum_t[hl:hl + 1, :]
        dec.append(jnp.where(incl, jnp.exp(jnp.where(incl, seg, 0.0)), 0.0))
        bt = beta[:, h:h + 1]
        a_mat.append(jnp.where(strict, kk[h // rep] * dec[h], 0.0) * bt)
        rhs.append(bt * (vc[:, h * GDN_DV:(h + 1) * GDN_DV] - eg[:, hl:hl + 1] * ks[h]))
    delta = _solve_unit_lower(a_mat, rhs, lambda ms, vs, dot: [dot(m, v) for m, v in zip(ms, vs)])
    o_mm = [_dot(qk0[h // rep] * dec[h], delta[h]) for h in heads]
    upd = [_dot_tn(kn[h // rep] * w_last[:, n_hv + h:n_hv + h + 1], delta[h]) for h in heads]
    for h in heads:
        hl = n_hv + h
        s_ref[0, h * GDN_DK:(h + 1) * GDN_DK, :] = s_old[h] * e_last[:, hl:hl + 1] + upd[h]
        o = eg[:, hl:hl + 1] * qs[h] + o_mm[h]
        o = o * lax.rsqrt(jnp.mean(o * o, axis=-1, keepdims=True) + EPS) * nw_ref[...]
        o = o * _silu(z[:, h * GDN_DV:(h + 1) * GDN_DV])
        o_ref[0, :, h * GDN_DV:(h + 1) * GDN_DV] = o.astype(o_ref.dtype)


def gdn_core(proj, t_valid, conv_buf, s0, conv_w, dt_bias, a_log, norm_w, *, chunk=64):
    bsz, t_len, width = proj.shape
    n_hv = dt_bias.shape[0]
    val_dim = n_hv * GDN_DV
    key_dim = (width - 2 * n_hv - 2 * val_dim) // 2
    n_hk = key_dim // GDN_DK
    assert val_dim == 2 * key_dim and key_dim % 128 == 0 and 2 * n_hv <= 128
    q = min(chunk, t_len)
    assert q % 8 == 0 and t_len % q == 0
    n_chunks = t_len // q
    dtb = jnp.zeros((1, 128), F32).at[0, n_hv:2 * n_hv].set(dt_bias)
    alog = jnp.zeros((1, 128), F32).at[0, n_hv:2 * n_hv].set(a_log)
    kern = functools.partial(_gdn_kernel, q=q, t_len=t_valid, n_hv=n_hv, n_hk=n_hk)
    ba_blk = (2 * key_dim + 2 * val_dim) // 128
    s0 = s0.reshape(bsz, n_hv * GDN_DK, GDN_DV)
    o, s = pl.pallas_call(
        kern,
        out_shape=(jax.ShapeDtypeStruct((bsz, t_len, val_dim), BF16),
                   jax.ShapeDtypeStruct(s0.shape, F32)),
        grid=(bsz, n_chunks),
        in_specs=[
            pl.BlockSpec((1, q, key_dim), lambda b, c: (b, c, 0)),
            pl.BlockSpec((1, q, key_dim), lambda b, c: (b, c, 1)),
            pl.BlockSpec((1, q, val_dim), lambda b, c: (b, c, 1)),
            pl.BlockSpec((1, q, val_dim), lambda b, c: (b, c, 2)),
            pl.BlockSpec((1, q, 128), lambda b, c: (b, c, ba_blk)),
            pl.BlockSpec((1, 3, key_dim), lambda b, c: (b, 0, 0)),
            pl.BlockSpec((1, 3, key_dim), lambda b, c: (b, 0, 1)),
            pl.BlockSpec((1, 3, val_dim), lambda b, c: (b, 0, 1)),
            pl.BlockSpec((1,) + s0.shape[1:], lambda b, c: (b, 0, 0)),
            pl.BlockSpec((4, key_dim), lambda b, c: (0, 0)),
            pl.BlockSpec((4, key_dim), lambda b, c: (0, 1)),
            pl.BlockSpec((4, val_dim), lambda b, c: (0, 1)),
            _resident((1, 128)), _resident((1, 128)), _resident((1, GDN_DV)),
        ],
        out_specs=(pl.BlockSpec((1, q, val_dim), lambda b, c: (b, c, 0)),
                   pl.BlockSpec((1,) + s0.shape[1:], lambda b, c: (b, 0, 0))),
        scratch_shapes=[pltpu.VMEM((q + 8, key_dim), F32), pltpu.VMEM((q + 8, key_dim), F32),
                        pltpu.VMEM((q + 8, val_dim), F32)],
        compiler_params=_cparams("parallel", "arbitrary"),
        name="gdn_core",
    )(proj, proj, proj, proj, proj, conv_buf, conv_buf, conv_buf, s0,
      conv_w, conv_w, conv_w, dtb, alog, norm_w.reshape(1, GDN_DV))
    return o, s.reshape(bsz, n_hv, GDN_DK, GDN_DV)


def gdn_mixer(x, t_valid, conv_buf, s0, norm_g, w_in, conv_w, dt_bias, a_log, norm_w):
    bsz, t_len, d = x.shape
    proj = norm_matmul(x.reshape(bsz * t_len, d), norm_g, w_in).reshape(bsz, t_len, -1)
    o, s_new = gdn_core(proj, t_valid, conv_buf, s0, conv_w, dt_bias, a_log, norm_w)
    conv_new = proj[:, t_valid - 3:t_valid, :conv_w.shape[1]]
    return o.reshape(bsz * t_len, -1), conv_new, s_new


def _head_ones(width, head):
    r = lax.broadcasted_iota(jnp.int32, (width, width), 0) // head
    c = lax.broadcasted_iota(jnp.int32, (width, width), 1) // head
    return jnp.where(r == c, 1.0, 0.0).astype(BF16)


def _head_sum(x, ones):
    hi, lo = _split(x)
    return _mm(hi, ones) + _mm(lo, ones)


def _rw_proj_kernel(x_ref, sh_ref, gn_ref, mu_ref, wr_ref, wk_ref, wv_ref, w1_ref, w2_ref, a1_ref, a2_ref,
                    g1_ref, g2_ref, w0_ref, a0_ref, kk_ref, ka_ref,
                    r_ref, lw_ref, k_ref, v_ref, kn_ref, b_ref, g_ref, shn_ref, buf_ref,
                    *, tm, state_tile, state_row):
    ti = pl.program_id(1)
    hn = _rmsnorm(x_ref[0], gn_ref[...])

    @pl.when(ti == 0)
    def _():
        buf_ref[pl.ds(7, 1), :] = sh_ref[0]

    buf_ref[pl.ds(8, tm), :] = hn
    dlt = buf_ref[pl.ds(7, tm), :] - hn

    @pl.when(ti == state_tile)
    def _():
        shn_ref[0] = buf_ref[pl.ds(8 + state_row, 1), :]

    buf_ref[pl.ds(7, 1), :] = buf_ref[pl.ds(tm + 7, 1), :]
    r_ref, lw_ref, k_ref, v_ref, kn_ref, b_ref, g_ref = (
        o.at[0] for o in (r_ref, lw_ref, k_ref, v_ref, kn_ref, b_ref, g_ref))

    def mix(c):
        return (hn + dlt * mu_ref[c:c + 1, :]).astype(BF16)

    r_ref[...] = _mm(mix(0), wr_ref[...])
    k = _mm(mix(1), wk_ref[...])
    v_ref[...] = _mm(mix(2), wv_ref[...])
    dec = w0_ref[...] + _mm(jnp.tanh(_mm(mix(3), w1_ref[...])).astype(BF16), w2_ref[...])
    lw_ref[...] = -RW_DECAY_SCALE * jax.nn.sigmoid(dec)
    a = jax.nn.sigmoid(a0_ref[...] + _mm(_mm(mix(4), a1_ref[...]).astype(BF16), a2_ref[...]))
    g_ref[...] = _mm(jax.nn.sigmoid(_mm(mix(5), g1_ref[...])).astype(BF16), g2_ref[...])
    ones = _head_ones(128, RW_HEAD_DIM)
    kn = k * kk_ref[...]
    d = kn.shape[1]
    for j in range(d // 128):
        sl = slice(j * 128, (j + 1) * 128)
        knj = kn[:, sl]
        knj = knj * lax.rsqrt(_head_sum(knj * knj, ones) + 1e-6)
        kn_ref[:, sl] = knj
        b_ref[:, sl] = knj * a[:, sl]
    k_ref[...] = k * (1.0 + (a - 1.0) * ka_ref[...])


def rw_proj(x, t_valid, shift_buf, norm_g, p, *, tm=256):
    bsz, t_len, d = x.shape
    tm = _row_tile(t_len, tm)
    assert t_len % tm == 0
    state_tile, state_row = divmod(t_valid - 1, tm)
    big = pl.BlockSpec((1, tm, d), lambda b, i: (b, i, 0))
    one_row = pl.BlockSpec((1, 1, d), lambda b, i: (b, 0, 0))
    ws = [p['rw_w_rkv'][0].astype(BF16), p['rw_w_rkv'][1].astype(BF16), p['rw_w_rkv'][2].astype(BF16),
          p['rw_w1'].astype(BF16), p['rw_w2'].astype(BF16), p['rw_a1'].astype(BF16), p['rw_a2'].astype(BF16),
          p['rw_g1'].astype(BF16), p['rw_g2'].astype(BF16)]
    vecs = [norm_g.reshape(1, d), p['rw_mu'], *ws, p['rw_w0'].reshape(1, d), p['rw_a0'].reshape(1, d),
            p['rw_k_k'].reshape(1, d), p['rw_k_a'].reshape(1, d)]
    big_out = jax.ShapeDtypeStruct((bsz, t_len, d), F32)
    return pl.pallas_call(
        functools.partial(_rw_proj_kernel, tm=tm, state_tile=state_tile, state_row=state_row),
        out_shape=(big_out,) * 7 + (jax.ShapeDtypeStruct((bsz, 1, d), F32),),
        grid=(bsz, t_len // tm),
        in_specs=[big, one_row] + [_resident(v.shape) for v in vecs],
        out_specs=(big,) * 7 + (one_row,),
        scratch_shapes=[pltpu.VMEM((tm + 8, d), F32)],
        compiler_params=_cparams("parallel", "arbitrary"),
        name="rw_proj",
    )(x, shift_buf, *vecs)


def _rw_scan_kernel(r_ref, lw_ref, k_ref, v_ref, kn_ref, b_ref, g_ref, s0_ref, rk_ref, lnw_ref, lnb_ref,
                    o_ref, s_ref, *, q, t_len):
    c = pl.program_id(1)

    @pl.when(c == 0)
    def _():
        s_ref[0] = s0_ref[0]

    rows = c * q + lax.broadcasted_iota(jnp.int32, (q, 1), 0)
    valid = rows < t_len
    incl = _tri(q)
    strict = _tri(q, strict=True)
    lw_all = jnp.where(valid, lw_ref[0], 0.0)
    cum_all = _dot_hi(incl.astype(F32), lw_all)
    ones = _head_ones(128, RW_HEAD_DIM)
    lo = lax.broadcasted_iota(jnp.int32, (q, 128), 1) < RW_HEAD_DIM
    blockdiag = ((lax.broadcasted_iota(jnp.int32, (128, 128), 0) < RW_HEAD_DIM)
                 == (lax.broadcasted_iota(jnp.int32, (128, 128), 1) < RW_HEAD_DIM))
    d = lw_all.shape[1]
    pairs = range(d // 128)
    sls = [slice(p * 128, (p + 1) * 128) for p in pairs]
    rs, ks, vs, bends, kends, s_old, cum_ends = [], [], [], [], [], [], []
    pk, pb, u0, y0 = [], [], [], []
    for p in pairs:
        sl = sls[p]
        lw = lw_all[:, sl]
        cum = cum_all[:, sl]
        cum_end = cum[q - 1:q, :]
        r = jnp.where(valid, r_ref[0, :, sl], 0.0)
        k = jnp.where(valid, k_ref[0, :, sl], 0.0)
        v = jnp.where(valid, v_ref[0, :, sl], 0.0)
        kn = jnp.where(valid, kn_ref[0, :, sl], 0.0)
        b = jnp.where(valid, b_ref[0, :, sl], 0.0)
        kq = kn * jnp.exp(cum - lw)
        rq = r * jnp.exp(cum)
        einv = jnp.exp(-cum)
        kd = k * einv
        bd = b * einv
        eend = jnp.exp(cum_end - cum)
        s2 = s_ref[0, sl, :]
        lhs = jnp.concatenate([jnp.where(lo, kq, 0.0), jnp.where(lo, 0.0, kq),
                               jnp.where(lo, rq, 0.0), jnp.where(lo, 0.0, rq)], axis=0)
        pk.append(_dot_nt(lhs, kd))
        pb.append(_dot_nt(lhs, bd))
        u0.append(_dot_nt(kq, s2))
        y0.append(_dot_nt(rq, s2))
        rs.append(r)
        ks.append(k)
        vs.append(v)
        kends.append(k * eend)
        bends.append(b * eend)
        s_old.append(s2)
        cum_ends.append(cum_end)
    ab = [jnp.where(strict, pb[p][e * q:(e + 1) * q], 0.0) for p in pairs for e in range(2)]
    rhs = [u0[p] + jnp.where(lo, _dot(jnp.where(strict, pk[p][0:q], 0.0), vs[p]),
                             _dot(jnp.where(strict, pk[p][q:2 * q], 0.0), vs[p])) for p in pairs]

    def per_head(ms, vecs, dot):
        return [jnp.where(lo, dot(ms[2 * p], vecs[p]), dot(ms[2 * p + 1], vecs[p])) for p in pairs]

    us = _solve_unit_lower(ab, rhs, per_head)
    ys = []
    for p in pairs:
        rk = [jnp.where(incl, pk[p][(2 + e) * q:(3 + e) * q], 0.0) for e in range(2)]
        rb = [jnp.where(incl, pb[p][(2 + e) * q:(3 + e) * q], 0.0) for e in range(2)]
        ys.append(y0[p] + jnp.where(lo, _dot(rk[0], vs[p]) - _dot(rb[0], us[p]),
                                    _dot(rk[1], vs[p]) - _dot(rb[1], us[p])))
    s_new = [s_old[p] * jnp.exp(cum_ends[p]) + _dot_tn(vs[p], kends[p]) - _dot_tn(us[p], bends[p])
             for p in pairs]
    for p in pairs:
        sl = sls[p]
        s_ref[0, sl, :] = jnp.where(blockdiag, s_new[p], 0.0)
        y = ys[p]
        mean = _head_sum(y, ones) * (1.0 / RW_HEAD_DIM)
        yc = y - mean
        var = _head_sum(yc * yc, ones) * (1.0 / RW_HEAD_DIM)
        yn = yc * lax.rsqrt(var + RW_GN_EPS) * lnw_ref[:, sl] + lnb_ref[:, sl]
        bonus = _head_sum(rs[p] * ks[p] * rk_ref[:, sl], ones) * vs[p]
        o_ref[0, :, sl] = ((yn + bonus) * g_ref[0, :, sl]).astype(o_ref.dtype)


def rw_scan(r, lw, k, v, kn, b, g, t_valid, s0, r_k, ln_w, ln_b, *, chunk=64):
    bsz, t_len, d = r.shape
    q = min(chunk, t_len)
    assert q % 8 == 0 and t_len % q == 0 and d % 128 == 0
    n_heads = d // RW_HEAD_DIM
    s5 = s0.reshape(bsz, n_heads // 2, 2, RW_HEAD_DIM, RW_HEAD_DIM)
    s2 = jnp.einsum('bpevk,ef->bpevfk', s5, jnp.eye(2, dtype=F32)).reshape(bsz, d, 128)
    blk = pl.BlockSpec((1, q, d), lambda bb, c: (bb, c, 0))
    vec = pl.BlockSpec((1, d), lambda bb, c: (0, 0))
    st = pl.BlockSpec((1, d, 128), lambda bb, c: (bb, 0, 0))
    o, s_new = pl.pallas_call(
        functools.partial(_rw_scan_kernel, q=q, t_len=t_valid),
        out_shape=(jax.ShapeDtypeStruct((bsz, t_len, d), BF16), jax.ShapeDtypeStruct((bsz, d, 128), F32)),
        grid=(bsz, t_len // q),
        in_specs=[blk] * 7 + [st, vec, vec, vec],
        out_specs=(blk, st),
        compiler_params=_cparams("parallel", "arbitrary"),
        name="rw_scan",
    )(r, lw, k, v, kn, b, g, s2, r_k.reshape(1, d), ln_w.reshape(1, d), ln_b.reshape(1, d))
    s6 = s_new.reshape(bsz, n_heads // 2, 2, RW_HEAD_DIM, 2, RW_HEAD_DIM)
    s_out = jnp.stack([s6[:, :, 0, :, 0, :], s6[:, :, 1, :, 1, :]], axis=2)
    return o, s_out.reshape(bsz, n_heads, RW_HEAD_DIM, RW_HEAD_DIM)


def rwkv_mixer(x, t_valid, shift_buf, s0, norm_g, p):
    bsz, t_len, d = x.shape
    r, lw, k, v, kn, b, g, shift_new = rw_proj(x, t_valid, shift_buf, norm_g, p)
    yg, s_new = rw_scan(r, lw, k, v, kn, b, g, t_valid, s0, p['rw_r_k'], p['rw_ln_w'], p['rw_ln_b'])
    return yg.reshape(bsz * t_len, d), shift_new, s_new


def _dot3_nt(a, b):
    ah, al = _split(a)
    bh, bl = _split(b)
    dn = (((1,), (1,)), ((), ()))
    f = lambda u, w: lax.dot_general(u, w, dn, preferred_element_type=F32)
    return f(ah, bh) + (f(ah, bl) + f(al, bh))


def _suffix_ones(n):
    r = lax.broadcasted_iota(jnp.int32, (n, n), 0)
    c = lax.broadcasted_iota(jnp.int32, (n, n), 1)
    return jnp.where(r >= c, 1.0, 0.0).astype(BF16)


def _sb_block(q_e, kblk, vblk, carry, acc, lo, suffix, mask):
    outs = []
    new_carry = []
    for e in range(2):
        z = _dot3_nt(q_e[e], kblk)
        lnb = -_softplus(z)
        if mask is not None:
            lnb = jnp.where(mask, lnb, 0.0)
        hi, lw = _split(lnb)
        rsum = _mm(hi, suffix) + _mm(lw, suffix) + carry[e]
        att = jnp.exp(z + rsum)
        if mask is not None:
            att = jnp.where(mask, att, 0.0)
        outs.append(_dot(att, vblk))
        new_carry.append(rsum[:, 0:1])
    return new_carry, acc + jnp.where(lo, outs[0], outs[1])


def _sb_kernel(q_ref, kn_ref, vn_ref, o_ref, kmax_ref, *, bq, n_pairs, scale):
    i = pl.program_id(2)
    t_new = kn_ref.shape[1]
    width = 128 * n_pairs
    pairs = range(n_pairs)

    @pl.when(i == 0)
    def _():
        def body(j, m):
            rows = kn_ref[0, pl.ds(pl.multiple_of(j * bq, bq), bq), :]
            return jnp.maximum(m, jnp.max(jnp.abs(rows).reshape(bq // 8, 8, width), axis=0))

        m8 = lax.fori_loop(0, t_new // bq, body, jnp.zeros((8, width), F32))
        kmax_ref[...] = jnp.max(m8, axis=0, keepdims=True)

    lo = lax.broadcasted_iota(jnp.int32, (bq, 128), 1) < SB_HEAD_DIM
    ones = _head_ones(128, SB_HEAD_DIM)
    q_e, zbs = [], []
    for p in pairs:
        q = q_ref[0, :, p * 128:(p + 1) * 128] * scale
        q_e.append([jnp.where(lo, q, 0.0), jnp.where(lo, 0.0, q)])
        zb = _head_sum(jnp.abs(q) * kmax_ref[:, p * 128:(p + 1) * 128], ones)
        zbs += [zb[:, 0:1], zb[:, SB_HEAD_DIM:SB_HEAD_DIM + 1]]
    suffix = _suffix_ones(bq)

    def block(off, carries, accs, mask):
        new_c, new_a = [], []
        for p in pairs:
            c, a = _sb_block(q_e[p], kn_ref[0, pl.ds(off, bq), p * 128:(p + 1) * 128],
                             vn_ref[0, pl.ds(off, bq), p * 128:(p + 1) * 128],
                             carries[2 * p:2 * p + 2], accs[p], lo, suffix, mask)
            new_c += c
            new_a.append(a)
        return new_c, new_a

    def live(carries):
        worst = carries[0] + zbs[0]
        for c, zb in zip(carries[1:], zbs[1:]):
            worst = jnp.maximum(worst, c + zb)
        return (jnp.max(worst) > SB_LOG_CUT).astype(jnp.int32)

    zero_c = jnp.zeros((bq, 1), F32)
    carries, accs = block(pl.multiple_of(i * bq, bq), [zero_c] * (2 * n_pairs),
                          [jnp.zeros((bq, 128), F32)] * n_pairs, _tri(bq, strict=True))

    def cond(s):
        return (s[0] < i) & (s[1] > 0)

    def body(s):
        jj, _, cs, acs = s
        cs, acs = block(pl.multiple_of((i - 1 - jj) * bq, bq), list(cs), list(acs), None)
        return jj + 1, live(cs), tuple(cs), tuple(acs)

    st = lax.while_loop(cond, body, (jnp.int32(0), live(carries), tuple(carries), tuple(accs)))
    for p in pairs:
        o_ref[0, :, p * 128:(p + 1) * 128] = st[3][p].astype(o_ref.dtype)


def sb_attention(qkv, *, block=256, pairs_per_step=2):
    bsz, t_len, d3 = qkv.shape
    d = d3 // 3
    width = 128 * pairs_per_step
    assert d % width == 0
    n_groups = d // width
    bq = min(block, t_len)
    assert t_len % bq == 0 and bq % 8 == 0

    def kv_spec(which):
        return pl.BlockSpec((1, t_len, width), lambda b, p, i: (b, 0, which * n_groups + p),
                            pipeline_mode=pl.Buffered(1))

    return pl.pallas_call(
        functools.partial(_sb_kernel, bq=bq, n_pairs=pairs_per_step, scale=SB_HEAD_DIM ** -0.5),
        out_shape=jax.ShapeDtypeStruct((bsz, t_len, d), BF16),
        grid=(bsz, n_groups, t_len // bq),
        in_specs=[pl.BlockSpec((1, bq, width), lambda b, p, i: (b, i, p)), kv_spec(1), kv_spec(2)],
        out_specs=pl.BlockSpec((1, bq, width), lambda b, p, i: (b, i, p)),
        compiler_params=_cparams("parallel", "parallel", "arbitrary"),
        scratch_shapes=[pltpu.VMEM((1, width), F32)],
        name="sb_attention",
    )(qkv, qkv, qkv)


def _sb_decode_kernel(qkv_ref, kc_ref, vc_hbm, o_ref, q_s, acc_s, carry_s, v_buf, v_sem, *, t, n_heads, bp, scale):
    j = pl.program_id(1)
    hd = SB_HEAD_DIM
    d = n_heads * hd
    rows = n_heads * t
    heads = range(n_heads)

    def scores(keys):
        return jnp.concatenate([_dot3_nt(q_s[h * t:(h + 1) * t, :], keys(h)) for h in heads], axis=0)

    def weights(z, carry, suffix, mask):
        lnb = -_softplus(z)
        if mask is not None:
            lnb = jnp.where(mask, lnb, 0.0)
        hi, lw = _split(lnb)
        rsum = _mm(hi, suffix) + _mm(lw, suffix) + carry
        att = jnp.exp(z + rsum)
        if mask is not None:
            att = jnp.where(mask, att, 0.0)
        return att, rsum[:, 0:1]

    @pl.when(j == 0)
    def _():
        for h in heads:
            q_s[h * t:(h + 1) * t, :] = qkv_ref[0, :, h * hd:(h + 1) * hd] * scale
        z = scores(lambda h: qkv_ref[0, :, d + h * hd:d + (h + 1) * hd])
        qi = lax.rem(lax.broadcasted_iota(jnp.int32, (rows, t), 0), t)
        mask = lax.broadcasted_iota(jnp.int32, (rows, t), 1) < qi
        att, carry = weights(z, 0.0, _suffix_ones(t), mask)
        for h in heads:
            acc_s[h * t:(h + 1) * t, :] = _dot(att[h * t:(h + 1) * t],
                                               qkv_ref[0, :, 2 * d + h * hd:2 * d + (h + 1) * hd])
        carry_s[...] = carry

    ub = jnp.concatenate([_dot(jnp.abs(q_s[h * t:(h + 1) * t, :]), jnp.abs(kc_ref[0, h])) for h in heads], axis=0)
    zb = SB_BOUND_SLACK * jnp.max(ub, axis=-1, keepdims=True)

    @pl.when(jnp.max(carry_s[...] + zb) > SB_LOG_CUT)
    def _():
        off = pl.multiple_of((pl.num_programs(1) - 1 - j) * bp, bp)
        v_copy = pltpu.make_async_copy(vc_hbm.at[pl.program_id(0), :, :, pl.ds(off, bp)], v_buf, v_sem)
        v_copy.start()
        z = jnp.concatenate([_dot3(q_s[h * t:(h + 1) * t, :], kc_ref[0, h]) for h in heads], axis=0)
        att, carry = weights(z, carry_s[...], _suffix_ones(bp), None)
        v_copy.wait()
        for h in heads:
            acc_s[h * t:(h + 1) * t, :] += _dot_nt(att[h * t:(h + 1) * t], v_buf[h])
        carry_s[...] = carry

    @pl.when(j == pl.num_programs(1) - 1)
    def _():
        o_ref[0] = jnp.concatenate([acc_s[h * t:(h + 1) * t, :] for h in heads], axis=1).astype(o_ref.dtype)


def sb_decode(qkv, k_cache, v_cache, *, block=512):
    bsz, t_len, d3 = qkv.shape
    _, past_len, n_heads, hd = k_cache.shape
    assert hd == SB_HEAD_DIM and n_heads * hd * 3 == d3 and t_len % 8 == 0
    bp = min(block, past_len)
    assert past_len % bp == 0
    n_blk = past_len // bp
    k_cache = jnp.transpose(k_cache, (0, 2, 3, 1))
    v_cache = jnp.transpose(v_cache, (0, 2, 3, 1))
    cache_spec = pl.BlockSpec((1, n_heads, hd, bp), lambda b, j: (b, 0, 0, n_blk - 1 - j))
    return pl.pallas_call(
        functools.partial(_sb_decode_kernel, t=t_len, n_heads=n_heads, bp=bp, scale=hd ** -0.5),
        out_shape=jax.ShapeDtypeStruct((bsz, t_len, n_heads * hd), BF16),
        grid=(bsz, n_blk),
        in_specs=[pl.BlockSpec((1, t_len, d3), lambda b, j: (b, 0, 0)), cache_spec,
                  pl.BlockSpec(memory_space=pl.ANY)],
        out_specs=pl.BlockSpec((1, t_len, n_heads * hd), lambda b, j: (b, 0, 0)),
        scratch_shapes=[pltpu.VMEM((n_heads * t_len, hd), F32), pltpu.VMEM((n_heads * t_len, hd), F32),
                        pltpu.VMEM((n_heads * t_len, 1), F32), pltpu.VMEM((n_heads, hd, bp), F32),
                        pltpu.SemaphoreType.DMA(())],
        compiler_params=_cparams("parallel", "arbitrary"),
        name="sb_decode",
    )(qkv, k_cache, v_cache)


def sb_mixer(x, t_valid, k_past, v_past, norm_g, w_qkv):
    bsz, t_len, d = x.shape
    qkv = norm_matmul(x.reshape(bsz * t_len, d), norm_g, w_qkv).reshape(bsz, t_len, 3 * d)
    o = sb_attention(qkv) if k_past is None else sb_decode(qkv, k_past, v_past)
    n_heads = d // SB_HEAD_DIM
    k_new = qkv[:, :t_valid, d:2 * d].reshape(bsz, t_valid, n_heads, SB_HEAD_DIM)
    v_new = qkv[:, :t_valid, 2 * d:].reshape(bsz, t_valid, n_heads, SB_HEAD_DIM)
    return o.reshape(bsz * t_len, d), k_new, v_new


def _run_trunk(x, t_valid, st, p):
    bsz, t_len, d = x.shape
    new = {}
    bf = lambda a: a.astype(BF16)

    def finish_layer(x, a, w_out, i, g_out=None):
        return mix_ffn(a, bf(w_out), x.reshape(bsz * t_len, d), p['norm_ffn'][i], bf(p['ffn_w_gu'][i]),
                       bf(p['ffn_w_down'][i]), g_out).reshape(bsz, t_len, d)

    a, new['ssm_conv'], new['ssm'] = mamba2_mixer(
        x, t_valid, st['ssm_conv'], st['ssm'], p['norm_mix'][0], bf(p['mb_w_in']), p['mb_conv_w'],
        p['mb_conv_b'], p['mb_dt_bias'], p['mb_a_log'], p['mb_d'], p['mb_norm'])
    x = finish_layer(x, a, p['mb_w_out'], 0)
    a, new['gdn_conv'], new['gdn'] = gdn_mixer(
        x, t_valid, st['gdn_conv'], st['gdn'], p['norm_mix'][1], bf(p['gdn_w_in']), p['gdn_conv_w'],
        p['gdn_dt_bias'], p['gdn_a_log'], p['gdn_norm'])
    x = finish_layer(x, a, p['gdn_w_out'], 1)
    a, new['rwkv_shift'], new['rwkv'] = rwkv_mixer(x, t_valid, st['rwkv_shift'], st['rwkv'], p['norm_mix'][2], p)
    x = finish_layer(x, a, p['rw_w_out'], 2)
    a, new['sb_k'], new['sb_v'] = sb_mixer(x, t_valid, st['sb_k'], st['sb_v'], p['norm_mix'][3], bf(p['sb_w_qkv']))
    return finish_layer(x, a, p['sb_w_out'], 3, p['norm_final']), new


def kernel(x_prompt, x_sample, state_ssm, state_ssm_conv, state_gdn, state_gdn_conv, state_rwkv, state_rwkv_shift, cache_sb_k, cache_sb_v, meta_tokens, norm_mix, norm_ffn, norm_final, ffn_w_gu, ffn_w_down, mb_w_in, mb_conv_w, mb_conv_b, mb_dt_bias, mb_a_log, mb_d, mb_norm, mb_w_out, gdn_w_in, gdn_conv_w, gdn_dt_bias, gdn_a_log, gdn_norm, gdn_w_out, rw_mu, rw_w_rkv, rw_w0, rw_w1, rw_w2, rw_a0, rw_a1, rw_a2, rw_g1, rw_g2, rw_k_k, rw_k_a, rw_r_k, rw_ln_w, rw_ln_b, rw_w_out, sb_w_qkv, sb_w_out):
    p = dict(
        norm_mix=norm_mix, norm_ffn=norm_ffn, norm_final=norm_final, ffn_w_gu=ffn_w_gu, ffn_w_down=ffn_w_down,
        mb_w_in=mb_w_in, mb_conv_w=mb_conv_w, mb_conv_b=mb_conv_b, mb_dt_bias=mb_dt_bias, mb_a_log=mb_a_log,
        mb_d=mb_d, mb_norm=mb_norm, mb_w_out=mb_w_out,
        gdn_w_in=gdn_w_in, gdn_conv_w=gdn_conv_w, gdn_dt_bias=gdn_dt_bias, gdn_a_log=gdn_a_log,
        gdn_norm=gdn_norm, gdn_w_out=gdn_w_out,
        rw_mu=rw_mu, rw_w_rkv=rw_w_rkv, rw_w0=rw_w0, rw_w1=rw_w1, rw_w2=rw_w2, rw_a0=rw_a0, rw_a1=rw_a1,
        rw_a2=rw_a2, rw_g1=rw_g1, rw_g2=rw_g2, rw_k_k=rw_k_k, rw_k_a=rw_k_a, rw_r_k=rw_r_k, rw_ln_w=rw_ln_w,
        rw_ln_b=rw_ln_b, rw_w_out=rw_w_out, sb_w_qkv=sb_w_qkv, sb_w_out=sb_w_out)
    bsz, seq, d = x_prompt.shape
    n_meta = meta_tokens.shape[0]
    t_valid = n_meta + seq
    t_pad = -(-t_valid // PROMPT_ROW_ALIGN) * PROMPT_ROW_ALIGN
    meta = jnp.broadcast_to(meta_tokens[None], (bsz, n_meta, d))
    x0 = jnp.concatenate([meta, x_prompt, jnp.zeros((bsz, t_pad - t_valid, d), x_prompt.dtype)], axis=1)
    fresh = dict(
        ssm=jnp.zeros((bsz,) + state_ssm.shape[1:], F32), ssm_conv=jnp.zeros((bsz,) + state_ssm_conv.shape[1:], F32),
        gdn=jnp.zeros((bsz,) + state_gdn.shape[1:], F32), gdn_conv=jnp.zeros((bsz,) + state_gdn_conv.shape[1:], F32),
        rwkv=jnp.zeros((bsz,) + state_rwkv.shape[1:], F32),
        rwkv_shift=jnp.zeros((bsz,) + state_rwkv_shift.shape[1:], F32), sb_k=None, sb_v=None)
    y_full, sp = _run_trunk(x0, t_valid, fresh, p)
    y_prompt = y_full[:, n_meta:t_valid]
    past = dict(ssm=state_ssm, ssm_conv=state_ssm_conv, gdn=state_gdn, gdn_conv=state_gdn_conv,
                rwkv=state_rwkv, rwkv_shift=state_rwkv_shift, sb_k=cache_sb_k, sb_v=cache_sb_v)
    y_sample, ss = _run_trunk(x_sample, x_sample.shape[1], past, p)
    return (y_prompt, y_sample,
            sp['ssm'], sp['ssm_conv'], sp['gdn'], sp['gdn_conv'], sp['rwkv'], sp['rwkv_shift'],
            sp['sb_k'], sp['sb_v'],
            ss['ssm'], ss['ssm_conv'], ss['gdn'], ss['gdn_conv'], ss['rwkv'], ss['rwkv_shift'],
            ss['sb_k'], ss['sb_v'])
```

```python
import functools
import math

import jax
import jax.numpy as jnp
from jax import lax
from jax.experimental import pallas as pl
from jax.experimental.pallas import tpu as pltpu

F32 = jnp.float32
BF16 = jnp.bfloat16
EPS = 1e-6
VMEM_LIMIT = 48 * 1024 * 1024
HI = lax.Precision.HIGHEST

PROMPT_ROW_ALIGN = 256
MB_HEAD_DIM = 64
MB_D_STATE = 128
MB_GROUPS = 4
GDN_DK = 128
GDN_DV = 128
RW_HEAD_DIM = 64
RW_GN_EPS = 64e-5
RW_DECAY_SCALE = 0.6065306597126334
SB_HEAD_DIM = 64
SB_LOG_CUT = -100.0
SB_BOUND_SLACK = 1.01


def _cparams(*sem):
    return pltpu.CompilerParams(dimension_semantics=sem, vmem_limit_bytes=VMEM_LIMIT)


def _dot(a, b):
    return jnp.dot(a.astype(BF16), b.astype(BF16), preferred_element_type=F32)


def _dot_nt(a, b):
    return lax.dot_general(a.astype(BF16), b.astype(BF16), (((1,), (1,)), ((), ())),
                           preferred_element_type=F32)


def _dot_tn(a, b):
    return lax.dot_general(a.astype(BF16), b.astype(BF16), (((0,), (0,)), ((), ())),
                           preferred_element_type=F32)


def _dot_hi(a, b):
    return jnp.dot(a, b, preferred_element_type=F32, precision=HI)


def _silu(x):
    return x * jax.nn.sigmoid(x)


def _softplus(x):
    return jnp.maximum(x, 0.0) + jnp.log(1.0 + jnp.exp(-jnp.abs(x)))


def _rmsnorm(x, g):
    return x * lax.rsqrt(jnp.mean(x * x, axis=-1, keepdims=True) + EPS) * g


def _row_tile(n, want):
    return n if n <= want else want


def _resident(shape):
    zeros = (0,) * len(shape)
    return pl.BlockSpec(shape, lambda *_: zeros, pipeline_mode=pl.Buffered(1))


def _norm_matmul_kernel(x_ref, g_ref, w_ref, o_ref):
    h = _rmsnorm(x_ref[...], g_ref[...]).astype(BF16)
    o_ref[...] = jnp.dot(h, w_ref[...], preferred_element_type=F32)


def norm_matmul(x, g, w, *, tm=256):
    n, d = x.shape
    m = w.shape[1]
    tm = _row_tile(n, tm)
    return pl.pallas_call(
        _norm_matmul_kernel,
        out_shape=jax.ShapeDtypeStruct((n, m), F32),
        grid=(pl.cdiv(n, tm),),
        in_specs=[pl.BlockSpec((tm, d), lambda i: (i, 0)), _resident((1, d)), _resident((d, m))],
        out_specs=pl.BlockSpec((tm, m), lambda i: (i, 0)),
        compiler_params=_cparams("parallel"),
        name="norm_matmul",
    )(x, g.reshape(1, d), w)


def _mix_ffn_kernel(a_ref, wo_ref, x_ref, g_ref, wgu_ref, wd_ref, *rest, f, tf):
    o_ref = rest[-1]
    x = x_ref[...] + jnp.dot(a_ref[...], wo_ref[...], preferred_element_type=F32)
    h = _rmsnorm(x, g_ref[...]).astype(BF16)
    acc = x
    for c in range(f // tf):
        gate = jnp.dot(h, wgu_ref[:, c * tf:(c + 1) * tf], preferred_element_type=F32)
        up = jnp.dot(h, wgu_ref[:, f + c * tf:f + (c + 1) * tf], preferred_element_type=F32)
        act = (_silu(gate) * up).astype(BF16)
        acc = acc + jnp.dot(act, wd_ref[c * tf:(c + 1) * tf, :], preferred_element_type=F32)
    if len(rest) == 2:
        acc = _rmsnorm(acc, rest[0][...])
    o_ref[...] = acc


def mix_ffn(a, w_out, x, g, w_gu, w_down, g_out=None, *, tm=512, tf=1408):
    n, d = x.shape
    k = a.shape[1]
    f = w_down.shape[0]
    tm = _row_tile(n, tm)
    if f % tf:
        tf = f
    args = [a, w_out, x, g.reshape(1, d), w_gu, w_down]
    in_specs = [pl.BlockSpec((tm, k), lambda i: (i, 0)), _resident((k, d)),
                pl.BlockSpec((tm, d), lambda i: (i, 0)), _resident((1, d)), _resident((d, 2 * f)),
                _resident((f, d))]
    if g_out is not None:
        args.append(g_out.reshape(1, d))
        in_specs.append(_resident((1, d)))
    return pl.pallas_call(
        functools.partial(_mix_ffn_kernel, f=f, tf=tf),
        out_shape=jax.ShapeDtypeStruct((n, d), F32),
        grid=(pl.cdiv(n, tm),),
        in_specs=in_specs,
        out_specs=pl.BlockSpec((tm, d), lambda i: (i, 0)),
        compiler_params=_cparams("parallel"),
        name="mix_ffn",
    )(*args)


def _causal_conv_silu(raw, buf_ref, w_ref, bias, q):
    buf_ref[pl.ds(8, q), :] = raw
    out = buf_ref[pl.ds(5, q), :] * w_ref[0:1, :]
    out = out + buf_ref[pl.ds(6, q), :] * w_ref[1:2, :]
    out = out + buf_ref[pl.ds(7, q), :] * w_ref[2:3, :]
    out = out + raw * w_ref[3:4, :]
    if bias is not None:
        out = out + bias
    buf_ref[pl.ds(5, 3), :] = buf_ref[pl.ds(q + 5, 3), :]
    return _silu(out)


def _tri(q, strict=False):
    r = lax.broadcasted_iota(jnp.int32, (q, q), 0)
    c = lax.broadcasted_iota(jnp.int32, (q, q), 1)
    return (r > c) if strict else (r >= c)


def _mamba_kernel(z_ref, xs_ref, bc_ref, dt_ref, cx_ref, cbc_ref, h0_ref,
                  cwx_ref, cwbc_ref, cbx_ref, cbbc_ref, dtb_ref, a_ref, dskip_ref, nw_ref,
                  y_ref, h_ref, bufx_ref, bufbc_ref, *, q, t_len, n_heads):
    c = pl.program_id(1)
    n_state = MB_D_STATE
    hpg = n_heads // MB_GROUPS
    gw = hpg * MB_HEAD_DIM

    @pl.when(c == 0)
    def _():
        bufx_ref[pl.ds(5, 3), :] = cx_ref[0]
        bufbc_ref[pl.ds(5, 3), :] = cbc_ref[0]
        h_ref[0] = h0_ref[0]

    rows = c * q + lax.broadcasted_iota(jnp.int32, (q, 1), 0)
    valid = rows < t_len
    xs = _causal_conv_silu(jnp.where(valid, xs_ref[0], 0.0), bufx_ref, cwx_ref, cbx_ref[...], q)
    bc = _causal_conv_silu(jnp.where(valid, bc_ref[0], 0.0), bufbc_ref, cwbc_ref, cbbc_ref[...], q)

    lane = lax.broadcasted_iota(jnp.int32, (q, 128), 1)
    dt = _softplus(jnp.where(valid & (lane < n_heads), dt_ref[0], 0.0) + dtb_ref[...])
    dt = jnp.where(valid & (lane < n_heads), dt, 0.0)
    da = dt * a_ref[...]
    incl = _tri(q)
    acum = _dot_hi(incl.astype(F32), da)
    acum_t = acum.T
    dt_t = dt.T
    last = acum[q - 1:q, :]
    wts = jnp.exp(last - acum) * dt
    e_acum = jnp.exp(acum)
    e_last = jnp.exp(last)

    lo_half = lax.broadcasted_iota(jnp.int32, (q, 128), 1) < MB_HEAD_DIM
    lo_rows = lax.broadcasted_iota(jnp.int32, (128, 128), 0) < MB_HEAD_DIM

    z = z_ref[0]
    for g in range(MB_GROUPS):
        bm = bc[:, g * n_state:(g + 1) * n_state]
        cm = bc[:, (MB_GROUPS + g) * n_state:(MB_GROUPS + g + 1) * n_state]
        cb = _dot_nt(cm, bm)
        hg = h_ref[0, g * gw:(g + 1) * gw, :]
        ch = _dot_nt(cm, hg)
        ys = []
        for p in range(hpg // 2):
            h_e = g * hpg + 2 * p
            x_pair = xs[:, h_e * MB_HEAD_DIM:(h_e + 2) * MB_HEAD_DIM]
            outs = []
            for hh in (h_e, h_e + 1):
                seg = acum[:, hh:hh + 1] - acum_t[hh:hh + 1, :]
                decay = jnp.where(incl, jnp.exp(jnp.where(incl, seg, 0.0)), 0.0)
                m = cb * decay * dt_t[hh:hh + 1, :]
                outs.append(_dot(m, x_pair))
            y_pair = jnp.where(lo_half, outs[0], outs[1])
            e_pair = jnp.where(lo_half, e_acum[:, h_e:h_e + 1], e_acum[:, h_e + 1:h_e + 2])
            y_pair = y_pair + e_pair * ch[:, 2 * p * MB_HEAD_DIM:(2 * p + 2) * MB_HEAD_DIM]
            ys.append(y_pair)
            w_pair = jnp.where(lo_half, wts[:, h_e:h_e + 1], wts[:, h_e + 1:h_e + 2])
            upd = _dot_tn(x_pair * w_pair, bm)
            scale = jnp.where(lo_rows, e_last[:, h_e:h_e + 1], e_last[:, h_e + 1:h_e + 2])
            r0 = h_e * MB_HEAD_DIM
            h_ref[0, r0:r0 + 128, :] = h_ref[0, r0:r0 + 128, :] * scale + upd
        yg = jnp.concatenate(ys, axis=1)
        xg = xs[:, g * gw:(g + 1) * gw]
        yg = yg + xg * dskip_ref[:, g * gw:(g + 1) * gw]
        yg = yg * _silu(z[:, g * gw:(g + 1) * gw])
        yg = yg * lax.rsqrt(jnp.mean(yg * yg, axis=-1, keepdims=True) + EPS)
        y_ref[0, :, g * gw:(g + 1) * gw] = (yg * nw_ref[:, g * gw:(g + 1) * gw]).astype(y_ref.dtype)


def mamba_core(proj, t_valid, conv_buf, h0, conv_w, conv_b, dt_bias, a_log, d_skip, norm_w, *, chunk=128):
    bsz, t_len, _ = proj.shape
    n_heads = dt_bias.shape[0]
    d_inner = n_heads * MB_HEAD_DIM
    gn = MB_GROUPS * MB_D_STATE
    assert d_inner % 1024 == 0 and 2 * gn == 1024 and n_heads <= 128
    q = min(chunk, t_len)
    assert q % 8 == 0 and t_len % q == 0
    n_chunks = t_len // q
    dtb = jnp.zeros((1, 128), F32).at[0, :n_heads].set(dt_bias)
    a_row = jnp.zeros((1, 128), F32).at[0, :n_heads].set(-jnp.exp(a_log))
    dskip = jnp.repeat(d_skip, MB_HEAD_DIM).reshape(1, d_inner)
    kern = functools.partial(_mamba_kernel, q=q, t_len=t_valid, n_heads=n_heads)
    xblk = d_inner // 1024
    y, h = pl.pallas_call(
        kern,
        out_shape=(jax.ShapeDtypeStruct((bsz, t_len, d_inner), BF16),
                   jax.ShapeDtypeStruct((bsz, n_heads * MB_HEAD_DIM, MB_D_STATE), F32)),
        grid=(bsz, n_chunks),
        in_specs=[
            pl.BlockSpec((1, q, d_inner), lambda b, c: (b, c, 0)),
            pl.BlockSpec((1, q, d_inner), lambda b, c: (b, c, 1)),
            pl.BlockSpec((1, q, 2 * gn), lambda b, c: (b, c, 2 * xblk)),
            pl.BlockSpec((1, q, 128), lambda b, c: (b, c, (2 * d_inner + 2 * gn) // 128)),
            pl.BlockSpec((1, 3, d_inner), lambda b, c: (b, 0, 0)),
            pl.BlockSpec((1, 3, 2 * gn), lambda b, c: (b, 0, xblk)),
            pl.BlockSpec((1, n_heads * MB_HEAD_DIM, MB_D_STATE), lambda b, c: (b, 0, 0)),
            pl.BlockSpec((4, d_inner), lambda b, c: (0, 0)),
            pl.BlockSpec((4, 2 * gn), lambda b, c: (0, xblk)),
            pl.BlockSpec((1, d_inner), lambda b, c: (0, 0)),
            pl.BlockSpec((1, 2 * gn), lambda b, c: (0, xblk)),
            _resident((1, 128)), _resident((1, 128)), _resident((1, d_inner)), _resident((1, d_inner)),
        ],
        out_specs=(pl.BlockSpec((1, q, d_inner), lambda b, c: (b, c, 0)),
                   pl.BlockSpec((1, n_heads * MB_HEAD_DIM, MB_D_STATE), lambda b, c: (b, 0, 0))),
        scratch_shapes=[pltpu.VMEM((q + 8, d_inner), F32), pltpu.VMEM((q + 8, 2 * gn), F32)],
        compiler_params=_cparams("parallel", "arbitrary"),
        name="mamba_core",
    )(proj, proj, proj, proj, conv_buf, conv_buf, h0.reshape(bsz, n_heads * MB_HEAD_DIM, MB_D_STATE),
      conv_w, conv_w, conv_b.reshape(1, -1), conv_b.reshape(1, -1), dtb, a_row, dskip,
      norm_w.reshape(1, d_inner))
    return y, h.reshape(bsz, n_heads, MB_HEAD_DIM, MB_D_STATE)


def mamba2_mixer(x, t_valid, conv_buf, ssm_state, norm_g, w_in, conv_w, conv_b, dt_bias, a_log, d_skip, norm_w):
    bsz, t_len, d = x.shape
    d_inner = dt_bias.shape[0] * MB_HEAD_DIM
    proj = norm_matmul(x.reshape(bsz * t_len, d), norm_g, w_in).reshape(bsz, t_len, -1)
    y, h_new = mamba_core(proj, t_valid, conv_buf, ssm_state, conv_w, conv_b, dt_bias, a_log, d_skip, norm_w)
    conv_new = proj[:, t_valid - 3:t_valid, d_inner:d_inner + conv_w.shape[1]]
    return y.reshape(bsz * t_len, d_inner), conv_new, h_new


def _split(a):
    hi = a.astype(BF16)
    lo = (a - hi.astype(F32)).astype(BF16)
    return hi, lo


def _mm(a, b):
    return jnp.dot(a, b, preferred_element_type=F32)


def _dot3(a, b):
    ah, al = _split(a)
    bh, bl = _split(b)
    return _mm(ah, bh) + (_mm(ah, bl) + _mm(al, bh))


INV_BASE = 8


def _unit_lower_inverse(mats):
    n = mats[0].shape[0]
    r = lax.broadcasted_iota(jnp.int32, (n, n), 0)
    c = lax.broadcasted_iota(jnp.int32, (n, n), 1)
    eye = jnp.where(r == c, 1.0, 0.0)
    same = (r // INV_BASE) == (c // INV_BASE)
    diag = [jnp.where(same, a, 0.0) for a in mats]
    xs = [eye - dg for dg in diag]
    ps = diag
    k = 1
    while 2 * k < INV_BASE:
        ps = [_dot(p, p) for p in ps]
        xs = [x + _dot(x, p) for x, p in zip(xs, ps)]
        k *= 2
    w = INV_BASE
    while w < n:
        off = ((r // (2 * w)) == (c // (2 * w))) != ((r // w) == (c // w))
        ts = [_dot(jnp.where(off, a, 0.0), x) for a, x in zip(mats, xs)]
        xs = [x - _dot(x, t) for x, t in zip(xs, ts)]
        w *= 2
    return xs


def _solve_unit_lower(mats, rhs, apply):
    x_inv = _unit_lower_inverse(mats)
    d0 = apply(x_inv, rhs, _dot)
    t0 = apply(mats, d0, _dot3)
    res = [b - d - t for b, d, t in zip(rhs, d0, t0)]
    corr = apply(x_inv, res, _dot)
    return [d + e for d, e in zip(d0, corr)]


def _gdn_kernel(q_ref, k_ref, v_ref, z_ref, ba_ref, cq_ref, ck_ref, cv_ref, s0_ref,
                cwq_ref, cwk_ref, cwv_ref, dtb_ref, al_ref, nw_ref,
                o_ref, s_ref, bufq_ref, bufk_ref, bufv_ref, *, q, t_len, n_hv, n_hk):
    c = pl.program_id(1)
    rep = n_hv // n_hk

    @pl.when(c == 0)
    def _():
        bufq_ref[pl.ds(5, 3), :] = cq_ref[0]
        bufk_ref[pl.ds(5, 3), :] = ck_ref[0]
        bufv_ref[pl.ds(5, 3), :] = cv_ref[0]
        s_ref[0] = s0_ref[0]

    rows = c * q + lax.broadcasted_iota(jnp.int32, (q, 1), 0)
    valid = rows < t_len
    qc = _causal_conv_silu(jnp.where(valid, q_ref[0], 0.0), bufq_ref, cwq_ref, None, q)
    kc = _causal_conv_silu(jnp.where(valid, k_ref[0], 0.0), bufk_ref, cwk_ref, None, q)
    vc = _causal_conv_silu(jnp.where(valid, v_ref[0], 0.0), bufv_ref, cwv_ref, None, q)

    lane = lax.broadcasted_iota(jnp.int32, (q, 128), 1)
    ba = jnp.where(valid & (lane < 2 * n_hv), ba_ref[0], 0.0)
    beta = jnp.where(valid & (lane < n_hv), jax.nn.sigmoid(ba), 0.0)
    g = jnp.where(valid & (lane >= n_hv) & (lane < 2 * n_hv),
                  -jnp.exp(al_ref[...]) * _softplus(ba + dtb_ref[...]), 0.0)
    incl = _tri(q)
    strict = _tri(q, strict=True)
    gcum = _dot_hi(incl.astype(F32), g)
    gcum_t = gcum.T
    eg = jnp.exp(gcum)
    glast = gcum[q - 1:q, :]
    w_last = jnp.exp(glast - gcum)
    e_last = jnp.exp(glast)

    z = z_ref[0]
    qn, kn, kk, qk0 = [], [], [], []
    for kh in range(n_hk):
        qh = qc[:, kh * GDN_DK:(kh + 1) * GDN_DK]
        kh_ = kc[:, kh * GDN_DK:(kh + 1) * GDN_DK]
        qh = qh * (lax.rsqrt(jnp.sum(qh * qh, axis=-1, keepdims=True) + 1e-6) * (GDN_DK ** -0.5))
        kh_ = kh_ * lax.rsqrt(jnp.sum(kh_ * kh_, axis=-1, keepdims=True) + 1e-6)
        qn.append(qh)
        kn.append(kh_)
        kk.append(_dot_nt(kh_, kh_))
        qk0.append(_dot_nt(qh, kh_))
    heads = range(n_hv)
    s_old = [s_ref[0, h * GDN_DK:(h + 1) * GDN_DK, :] for h in heads]
    ks = [_dot(kn[h // rep], s_old[h]) for h in heads]
    qs = [_dot(qn[h // rep], s_old[h]) for h in heads]
    dec, a_mat, rhs = [], [], []
    for h in heads:
        hl = n_hv + h
        seg = gcum[:, hl:hl + 1] - gcum_t[hl:hl + 1, :]
        dec.append(jnp.where(incl, jnp.exp(jnp.where(incl, seg, 0.0)), 0.0))
        bt = beta[:, h:h + 1]
        a_mat.append(jnp.where(strict, kk[h // rep] * dec[h], 0.0) * bt)
        rhs.append(bt * (vc[:, h * GDN_DV:(h + 1) * GDN_DV] - eg[:, hl:hl + 1] * ks[h]))
    delta = _solve_unit_lower(a_mat, rhs, lambda ms, vs, dot: [dot(m, v) for m, v in zip(ms, vs)])
    o_mm = [_dot(qk0[h // rep] * dec[h], delta[h]) for h in heads]
    upd = [_dot_tn(kn[h // rep] * w_last[:, n_hv + h:n_hv + h + 1], delta[h]) for h in heads]
    for h in heads:
        hl = n_hv + h
        s_ref[0, h * GDN_DK:(h + 1) * GDN_DK, :] = s_old[h] * e_last[:, hl:hl + 1] + upd[h]
        o = eg[:, hl:hl + 1] * qs[h] + o_mm[h]
        o = o * lax.rsqrt(jnp.mean(o * o, axis=-1, keepdims=True) + EPS) * nw_ref[...]
        o = o * _silu(z[:, h * GDN_DV:(h + 1) * GDN_DV])
        o_ref[0, :, h * GDN_DV:(h + 1) * GDN_DV] = o.astype(o_ref.dtype)


def gdn_core(proj, t_valid, conv_buf, s0, conv_w, dt_bias, a_log, norm_w, *, chunk=64):
    bsz, t_len, width = proj.shape
    n_hv = dt_bias.shape[0]
    val_dim = n_hv * GDN_DV
    key_dim = (width - 2 * n_hv - 2 * val_dim) // 2
    n_hk = key_dim // GDN_DK
    assert val_dim == 2 * key_dim and key_dim % 128 == 0 and 2 * n_hv <= 128
    q = min(chunk, t_len)
    assert q % 8 == 0 and t_len % q == 0
    n_chunks = t_len // q
    dtb = jnp.zeros((1, 128), F32).at[0, n_hv:2 * n_hv].set(dt_bias)
    alog = jnp.zeros((1, 128), F32).at[0, n_hv:2 * n_hv].set(a_log)
    kern = functools.partial(_gdn_kernel, q=q, t_len=t_valid, n_hv=n_hv, n_hk=n_hk)
    ba_blk = (2 * key_dim + 2 * val_dim) // 128
    s0 = s0.reshape(bsz, n_hv * GDN_DK, GDN_DV)
    o, s = pl.pallas_call(
        kern,
        out_shape=(jax.ShapeDtypeStruct((bsz, t_len, val_dim), BF16),
                   jax.ShapeDtypeStruct(s0.shape, F32)),
        grid=(bsz, n_chunks),
        in_specs=[
            pl.BlockSpec((1, q, key_dim), lambda b, c: (b, c, 0)),
            pl.BlockSpec((1, q, key_dim), lambda b, c: (b, c, 1)),
            pl.BlockSpec((1, q, val_dim), lambda b, c: (b, c, 1)),
            pl.BlockSpec((1, q, val_dim), lambda b, c: (b, c, 2)),
            pl.BlockSpec((1, q, 128), lambda b, c: (b, c, ba_blk)),
            pl.BlockSpec((1, 3, key_dim), lambda b, c: (b, 0, 0)),
            pl.BlockSpec((1, 3, key_dim), lambda b, c: (b, 0, 1)),
            pl.BlockSpec((1, 3, val_dim), lambda b, c: (b, 0, 1)),
            pl.BlockSpec((1,) + s0.shape[1:], lambda b, c: (b, 0, 0)),
            pl.BlockSpec((4, key_dim), lambda b, c: (0, 0)),
            pl.BlockSpec((4, key_dim), lambda b, c: (0, 1)),
            pl.BlockSpec((4, val_dim), lambda b, c: (0, 1)),
            _resident((1, 128)), _resident((1, 128)), _resident((1, GDN_DV)),
        ],
        out_specs=(pl.BlockSpec((1, q, val_dim), lambda b, c: (b, c, 0)),
                   pl.BlockSpec((1,) + s0.shape[1:], lambda b, c: (b, 0, 0))),
        scratch_shapes=[pltpu.VMEM((q + 8, key_dim), F32), pltpu.VMEM((q + 8, key_dim), F32),
                        pltpu.VMEM((q + 8, val_dim), F32)],
        compiler_params=_cparams("parallel", "arbitrary"),
        name="gdn_core",
    )(proj, proj, proj, proj, proj, conv_buf, conv_buf, conv_buf, s0,
      conv_w, conv_w, conv_w, dtb, alog, norm_w.reshape(1, GDN_DV))
    return o, s.reshape(bsz, n_hv, GDN_DK, GDN_DV)


def gdn_mixer(x, t_valid, conv_buf, s0, norm_g, w_in, conv_w, dt_bias, a_log, norm_w):
    bsz, t_len, d = x.shape
    proj = norm_matmul(x.reshape(bsz * t_len, d), norm_g, w_in).reshape(bsz, t_len, -1)
    o, s_new = gdn_core(proj, t_valid, conv_buf, s0, conv_w, dt_bias, a_log, norm_w)
    conv_new = proj[:, t_valid - 3:t_valid, :conv_w.shape[1]]
    return o.reshape(bsz * t_len, -1), conv_new, s_new


def _head_ones(width, head):
    r = lax.broadcasted_iota(jnp.int32, (width, width), 0) // head
    c = lax.broadcasted_iota(jnp.int32, (width, width), 1) // head
    return jnp.where(r == c, 1.0, 0.0).astype(BF16)


def _head_sum(x, ones):
    hi, lo = _split(x)
    return _mm(hi, ones) + _mm(lo, ones)


def _rw_proj_kernel(x_ref, sh_ref, gn_ref, mu_ref, wr_ref, wk_ref, wv_ref, w1_ref, w2_ref, a1_ref, a2_ref,
                    g1_ref, g2_ref, w0_ref, a0_ref, kk_ref, ka_ref,
                    r_ref, lw_ref, k_ref, v_ref, kn_ref, b_ref, g_ref, shn_ref, buf_ref,
                    *, tm, state_tile, state_row):
    ti = pl.program_id(1)
    hn = _rmsnorm(x_ref[0], gn_ref[...])

    @pl.when(ti == 0)
    def _():
        buf_ref[pl.ds(7, 1), :] = sh_ref[0]

    buf_ref[pl.ds(8, tm), :] = hn
    dlt = buf_ref[pl.ds(7, tm), :] - hn

    @pl.when(ti == state_tile)
    def _():
        shn_ref[0] = buf_ref[pl.ds(8 + state_row, 1), :]

    buf_ref[pl.ds(7, 1), :] = buf_ref[pl.ds(tm + 7, 1), :]
    r_ref, lw_ref, k_ref, v_ref, kn_ref, b_ref, g_ref = (
        o.at[0] for o in (r_ref, lw_ref, k_ref, v_ref, kn_ref, b_ref, g_ref))

    def mix(c):
        return (hn + dlt * mu_ref[c:c + 1, :]).astype(BF16)

    r_ref[...] = _mm(mix(0), wr_ref[...])
    k = _mm(mix(1), wk_ref[...])
    v_ref[...] = _mm(mix(2), wv_ref[...])
    dec = w0_ref[...] + _mm(jnp.tanh(_mm(mix(3), w1_ref[...])).astype(BF16), w2_ref[...])
    lw_ref[...] = -RW_DECAY_SCALE * jax.nn.sigmoid(dec)
    a = jax.nn.sigmoid(a0_ref[...] + _mm(_mm(mix(4), a1_ref[...]).astype(BF16), a2_ref[...]))
    g_ref[...] = _mm(jax.nn.sigmoid(_mm(mix(5), g1_ref[...])).astype(BF16), g2_ref[...])
    ones = _head_ones(128, RW_HEAD_DIM)
    kn = k * kk_ref[...]
    d = kn.shape[1]
    for j in range(d // 128):
        sl = slice(j * 128, (j + 1) * 128)
        knj = kn[:, sl]
        knj = knj * lax.rsqrt(_head_sum(knj * knj, ones) + 1e-6)
        kn_ref[:, sl] = knj
        b_ref[:, sl] = knj * a[:, sl]
    k_ref[...] = k * (1.0 + (a - 1.0) * ka_ref[...])


def rw_proj(x, t_valid, shift_buf, norm_g, p, *, tm=256):
    bsz, t_len, d = x.shape
    tm = _row_tile(t_len, tm)
    assert t_len % tm == 0
    state_tile, state_row = divmod(t_valid - 1, tm)
    big = pl.BlockSpec((1, tm, d), lambda b, i: (b, i, 0))
    one_row = pl.BlockSpec((1, 1, d), lambda b, i: (b, 0, 0))
    ws = [p['rw_w_rkv'][0].astype(BF16), p['rw_w_rkv'][1].astype(BF16), p['rw_w_rkv'][2].astype(BF16),
          p['rw_w1'].astype(BF16), p['rw_w2'].astype(BF16), p['rw_a1'].astype(BF16), p['rw_a2'].astype(BF16),
          p['rw_g1'].astype(BF16), p['rw_g2'].astype(BF16)]
    vecs = [norm_g.reshape(1, d), p['rw_mu'], *ws, p['rw_w0'].reshape(1, d), p['rw_a0'].reshape(1, d),
            p['rw_k_k'].reshape(1, d), p['rw_k_a'].reshape(1, d)]
    big_out = jax.ShapeDtypeStruct((bsz, t_len, d), F32)
    return pl.pallas_call(
        functools.partial(_rw_proj_kernel, tm=tm, state_tile=state_tile, state_row=state_row),
        out_shape=(big_out,) * 7 + (jax.ShapeDtypeStruct((bsz, 1, d), F32),),
        grid=(bsz, t_len // tm),
        in_specs=[big, one_row] + [_resident(v.shape) for v in vecs],
        out_specs=(big,) * 7 + (one_row,),
        scratch_shapes=[pltpu.VMEM((tm + 8, d), F32)],
        compiler_params=_cparams("parallel", "arbitrary"),
        name="rw_proj",
    )(x, shift_buf, *vecs)


def _rw_scan_kernel(r_ref, lw_ref, k_ref, v_ref, kn_ref, b_ref, g_ref, s0_ref, rk_ref, lnw_ref, lnb_ref,
                    o_ref, s_ref, *, q, t_len):
    c = pl.program_id(1)

    @pl.when(c == 0)
    def _():
        s_ref[0] = s0_ref[0]

    rows = c * q + lax.broadcasted_iota(jnp.int32, (q, 1), 0)
    valid = rows < t_len
    incl = _tri(q)
    strict = _tri(q, strict=True)
    lw_all = jnp.where(valid, lw_ref[0], 0.0)
    cum_all = _dot_hi(incl.astype(F32), lw_all)
    ones = _head_ones(128, RW_HEAD_DIM)
    lo = lax.broadcasted_iota(jnp.int32, (q, 128), 1) < RW_HEAD_DIM
    blockdiag = ((lax.broadcasted_iota(jnp.int32, (128, 128), 0) < RW_HEAD_DIM)
                 == (lax.broadcasted_iota(jnp.int32, (128, 128), 1) < RW_HEAD_DIM))
    d = lw_all.shape[1]
    pairs = range(d // 128)
    sls = [slice(p * 128, (p + 1) * 128) for p in pairs]
    rs, ks, vs, bends, kends, s_old, cum_ends = [], [], [], [], [], [], []
    pk, pb, u0, y0 = [], [], [], []
    for p in pairs:
        sl = sls[p]
        lw = lw_all[:, sl]
        cum = cum_all[:, sl]
        cum_end = cum[q - 1:q, :]
        r = jnp.where(valid, r_ref[0, :, sl], 0.0)
        k = jnp.where(valid, k_ref[0, :, sl], 0.0)
        v = jnp.where(valid, v_ref[0, :, sl], 0.0)
        kn = jnp.where(valid, kn_ref[0, :, sl], 0.0)
        b = jnp.where(valid, b_ref[0, :, sl], 0.0)
        kq = kn * jnp.exp(cum - lw)
        rq = r * jnp.exp(cum)
        einv = jnp.exp(-cum)
        kd = k * einv
        bd = b * einv
        eend = jnp.exp(cum_end - cum)
        s2 = s_ref[0, sl, :]
        lhs = jnp.concatenate([jnp.where(lo, kq, 0.0), jnp.where(lo, 0.0, kq),
                               jnp.where(lo, rq, 0.0), jnp.where(lo, 0.0, rq)], axis=0)
        pk.append(_dot_nt(lhs, kd))
        pb.append(_dot_nt(lhs, bd))
        u0.append(_dot_nt(kq, s2))
        y0.append(_dot_nt(rq, s2))
        rs.append(r)
        ks.append(k)
        vs.append(v)
        kends.append(k * eend)
        bends.append(b * eend)
        s_old.append(s2)
        cum_ends.append(cum_end)
    ab = [jnp.where(strict, pb[p][e * q:(e + 1) * q], 0.0) for p in pairs for e in range(2)]
    rhs = [u0[p] + jnp.where(lo, _dot(jnp.where(strict, pk[p][0:q], 0.0), vs[p]),
                             _dot(jnp.where(strict, pk[p][q:2 * q], 0.0), vs[p])) for p in pairs]

    def per_head(ms, vecs, dot):
        return [jnp.where(lo, dot(ms[2 * p], vecs[p]), dot(ms[2 * p + 1], vecs[p])) for p in pairs]

    us = _solve_unit_lower(ab, rhs, per_head)
    ys = []
    for p in pairs:
        rk = [jnp.where(incl, pk[p][(2 + e) * q:(3 + e) * q], 0.0) for e in range(2)]
        rb = [jnp.where(incl, pb[p][(2 + e) * q:(3 + e) * q], 0.0) for e in range(2)]
        ys.append(y0[p] + jnp.where(lo, _dot(rk[0], vs[p]) - _dot(rb[0], us[p]),
                                    _dot(rk[1], vs[p]) - _dot(rb[1], us[p])))
    s_new = [s_old[p] * jnp.exp(cum_ends[p]) + _dot_tn(vs[p], kends[p]) - _dot_tn(us[p], bends[p])
             for p in pairs]
    for p in pairs:
        sl = sls[p]
        s_ref[0, sl, :] = jnp.where(blockdiag, s_new[p], 0.0)
        y = ys[p]
        mean = _head_sum(y, ones) * (1.0 / RW_HEAD_DIM)
        yc = y - mean
        var = _head_sum(yc * yc, ones) * (1.0 / RW_HEAD_DIM)
        yn = yc * lax.rsqrt(var + RW_GN_EPS) * lnw_ref[:, sl] + lnb_ref[:, sl]
        bonus = _head_sum(rs[p] * ks[p] * rk_ref[:, sl], ones) * vs[p]
        o_ref[0, :, sl] = ((yn + bonus) * g_ref[0, :, sl]).astype(o_ref.dtype)


def rw_scan(r, lw, k, v, kn, b, g, t_valid, s0, r_k, ln_w, ln_b, *, chunk=64):
    bsz, t_len, d = r.shape
    q = min(chunk, t_len)
    assert q % 8 == 0 and t_len % q == 0 and d % 128 == 0
    n_heads = d // RW_HEAD_DIM
    s5 = s0.reshape(bsz, n_heads // 2, 2, RW_HEAD_DIM, RW_HEAD_DIM)
    s2 = jnp.einsum('bpevk,ef->bpevfk', s5, jnp.eye(2, dtype=F32)).reshape(bsz, d, 128)
    blk = pl.BlockSpec((1, q, d), lambda bb, c: (bb, c, 0))
    vec = pl.BlockSpec((1, d), lambda bb, c: (0, 0))
    st = pl.BlockSpec((1, d, 128), lambda bb, c: (bb, 0, 0))
    o, s_new = pl.pallas_call(
        functools.partial(_rw_scan_kernel, q=q, t_len=t_valid),
        out_shape=(jax.ShapeDtypeStruct((bsz, t_len, d), BF16), jax.ShapeDtypeStruct((bsz, d, 128), F32)),
        grid=(bsz, t_len // q),
        in_specs=[blk] * 7 + [st, vec, vec, vec],
        out_specs=(blk, st),
        compiler_params=_cparams("parallel", "arbitrary"),
        name="rw_scan",
    )(r, lw, k, v, kn, b, g, s2, r_k.reshape(1, d), ln_w.reshape(1, d), ln_b.reshape(1, d))
    s6 = s_new.reshape(bsz, n_heads // 2, 2, RW_HEAD_DIM, 2, RW_HEAD_DIM)
    s_out = jnp.stack([s6[:, :, 0, :, 0, :], s6[:, :, 1, :, 1, :]], axis=2)
    return o, s_out.reshape(bsz, n_heads, RW_HEAD_DIM, RW_HEAD_DIM)


def rwkv_mixer(x, t_valid, shift_buf, s0, norm_g, p):
    bsz, t_len, d = x.shape
    r, lw, k, v, kn, b, g, shift_new = rw_proj(x, t_valid, shift_buf, norm_g, p)
    yg, s_new = rw_scan(r, lw, k, v, kn, b, g, t_valid, s0, p['rw_r_k'], p['rw_ln_w'], p['rw_ln_b'])
    return yg.reshape(bsz * t_len, d), shift_new, s_new


def _dot3_nt(a, b):
    ah, al = _split(a)
    bh, bl = _split(b)
    dn = (((1,), (1,)), ((), ()))
    f = lambda u, w: lax.dot_general(u, w, dn, preferred_element_type=F32)
    return f(ah, bh) + (f(ah, bl) + f(al, bh))


def _suffix_ones(n):
    r = lax.broadcasted_iota(jnp.int32, (n, n), 0)
    c = lax.broadcasted_iota(jnp.int32, (n, n), 1)
    return jnp.where(r >= c, 1.0, 0.0).astype(BF16)


def _sb_block(q_e, kblk, vblk, carry, acc, lo, suffix, mask):
    outs = []
    new_carry = []
    for e in range(2):
        z = _dot3_nt(q_e[e], kblk)
        lnb = -_softplus(z)
        if mask is not None:
            lnb = jnp.where(mask, lnb, 0.0)
        hi, lw = _split(lnb)
        rsum = _mm(hi, suffix) + _mm(lw, suffix) + carry[e]
        att = jnp.exp(z + rsum)
        if mask is not None:
            att = jnp.where(mask, att, 0.0)
        outs.append(_dot(att, vblk))
        new_carry.append(rsum[:, 0:1])
    return new_carry, acc + jnp.where(lo, outs[0], outs[1])


def _sb_kernel(q_ref, kn_ref, vn_ref, o_ref, kmax_ref, *, bq, n_pairs, scale):
    i = pl.program_id(2)
    t_new = kn_ref.shape[1]
    width = 128 * n_pairs
    pairs = range(n_pairs)

    @pl.when(i == 0)
    def _():
        def body(j, m):
            rows = kn_ref[0, pl.ds(pl.multiple_of(j * bq, bq), bq), :]
            return jnp.maximum(m, jnp.max(jnp.abs(rows).reshape(bq // 8, 8, width), axis=0))

        m8 = lax.fori_loop(0, t_new // bq, body, jnp.zeros((8, width), F32))
        kmax_ref[...] = jnp.max(m8, axis=0, keepdims=True)

    lo = lax.broadcasted_iota(jnp.int32, (bq, 128), 1) < SB_HEAD_DIM
    ones = _head_ones(128, SB_HEAD_DIM)
    q_e, zbs = [], []
    for p in pairs:
        q = q_ref[0, :, p * 128:(p + 1) * 128] * scale
        q_e.append([jnp.where(lo, q, 0.0), jnp.where(lo, 0.0, q)])
        zb = _head_sum(jnp.abs(q) * kmax_ref[:, p * 128:(p + 1) * 128], ones)
        zbs += [zb[:, 0:1], zb[:, SB_HEAD_DIM:SB_HEAD_DIM + 1]]
    suffix = _suffix_ones(bq)

    def block(off, carries, accs, mask):
        new_c, new_a = [], []
        for p in pairs:
            c, a = _sb_block(q_e[p], kn_ref[0, pl.ds(off, bq), p * 128:(p + 1) * 128],
                             vn_ref[0, pl.ds(off, bq), p * 128:(p + 1) * 128],
                             carries[2 * p:2 * p + 2], accs[p], lo, suffix, mask)
            new_c += c
            new_a.append(a)
        return new_c, new_a

    def live(carries):
        worst = carries[0] + zbs[0]
        for c, zb in zip(carries[1:], zbs[1:]):
            worst = jnp.maximum(worst, c + zb)
        return (jnp.max(worst) > SB_LOG_CUT).astype(jnp.int32)

    zero_c = jnp.zeros((bq, 1), F32)
    carries, accs = block(pl.multiple_of(i * bq, bq), [zero_c] * (2 * n_pairs),
                          [jnp.zeros((bq, 128), F32)] * n_pairs, _tri(bq, strict=True))

    def cond(s):
        return (s[0] < i) & (s[1] > 0)

    def body(s):
        jj, _, cs, acs = s
        cs, acs = block(pl.multiple_of((i - 1 - jj) * bq, bq), list(cs), list(acs), None)
        return jj + 1, live(cs), tuple(cs), tuple(acs)

    st = lax.while_loop(cond, body, (jnp.int32(0), live(carries), tuple(carries), tuple(accs)))
    for p in pairs:
        o_ref[0, :, p * 128:(p + 1) * 128] = st[3][p].astype(o_ref.dtype)


def sb_attention(qkv, *, block=256, pairs_per_step=2):
    bsz, t_len, d3 = qkv.shape
    d = d3 // 3
    width = 128 * pairs_per_step
    assert d % width == 0
    n_groups = d // width
    bq = min(block, t_len)
    assert t_len % bq == 0 and bq % 8 == 0

    def kv_spec(which):
        return pl.BlockSpec((1, t_len, width), lambda b, p, i: (b, 0, which * n_groups + p),
                            pipeline_mode=pl.Buffered(1))

    return pl.pallas_call(
        functools.partial(_sb_kernel, bq=bq, n_pairs=pairs_per_step, scale=SB_HEAD_DIM ** -0.5),
        out_shape=jax.ShapeDtypeStruct((bsz, t_len, d), BF16),
        grid=(bsz, n_groups, t_len // bq),
        in_specs=[pl.BlockSpec((1, bq, width), lambda b, p, i: (b, i, p)), kv_spec(1), kv_spec(2)],
        out_specs=pl.BlockSpec((1, bq, width), lambda b, p, i: (b, i, p)),
        compiler_params=_cparams("parallel", "parallel", "arbitrary"),
        scratch_shapes=[pltpu.VMEM((1, width), F32)],
        name="sb_attention",
    )(qkv, qkv, qkv)


def _sb_decode_kernel(qkv_ref, kc_ref, vc_hbm, o_ref, q_s, acc_s, carry_s, v_buf, v_sem, *, t, n_heads, bp, scale):
    j = pl.program_id(1)
    hd = SB_HEAD_DIM
    d = n_heads * hd
    rows = n_heads * t
    heads = range(n_heads)

    def scores(keys):
        return jnp.concatenate([_dot3_nt(q_s[h * t:(h + 1) * t, :], keys(h)) for h in heads], axis=0)

    def weights(z, carry, suffix, mask):
        lnb = -_softplus(z)
        if mask is not None:
            lnb = jnp.where(mask, lnb, 0.0)
        hi, lw = _split(lnb)
        rsum = _mm(hi, suffix) + _mm(lw, suffix) + carry
        att = jnp.exp(z + rsum)
        if mask is not None:
            att = jnp.where(mask, att, 0.0)
        return att, rsum[:, 0:1]

    @pl.when(j == 0)
    def _():
        for h in heads:
            q_s[h * t:(h + 1) * t, :] = qkv_ref[0, :, h * hd:(h + 1) * hd] * scale
        z = scores(lambda h: qkv_ref[0, :, d + h * hd:d + (h + 1) * hd])
        qi = lax.rem(lax.broadcasted_iota(jnp.int32, (rows, t), 0), t)
        mask = lax.broadcasted_iota(jnp.int32, (rows, t), 1) < qi
        att, carry = weights(z, 0.0, _suffix_ones(t), mask)
        for h in heads:
            acc_s[h * t:(h + 1) * t, :] = _dot(att[h * t:(h + 1) * t],
                                               qkv_ref[0, :, 2 * d + h * hd:2 * d + (h + 1) * hd])
        carry_s[...] = carry

    ub = jnp.concatenate([_dot(jnp.abs(q_s[h * t:(h + 1) * t, :]), jnp.abs(kc_ref[0, h])) for h in heads], axis=0)
    zb = SB_BOUND_SLACK * jnp.max(ub, axis=-1, keepdims=True)

    @pl.when(jnp.max(carry_s[...] + zb) > SB_LOG_CUT)
    def _():
        off = pl.multiple_of((pl.num_programs(1) - 1 - j) * bp, bp)
        v_copy = pltpu.make_async_copy(vc_hbm.at[pl.program_id(0), :, :, pl.ds(off, bp)], v_buf, v_sem)
        v_copy.start()
        z = jnp.concatenate([_dot3(q_s[h * t:(h + 1) * t, :], kc_ref[0, h]) for h in heads], axis=0)
        att, carry = weights(z, carry_s[...], _suffix_ones(bp), None)
        v_copy.wait()
        for h in heads:
            acc_s[h * t:(h + 1) * t, :] += _dot_nt(att[h * t:(h + 1) * t], v_buf[h])
        carry_s[...] = carry

    @pl.when(j == pl.num_programs(1) - 1)
    def _():
        o_ref[0] = jnp.concatenate([acc_s[h * t:(h + 1) * t, :] for h in heads], axis=1).astype(o_ref.dtype)


def sb_decode(qkv, k_cache, v_cache, *, block=512):
    bsz, t_len, d3 = qkv.shape
    _, past_len, n_heads, hd = k_cache.shape
    assert hd == SB_HEAD_DIM and n_heads * hd * 3 == d3 and t_len % 8 == 0
    bp = min(block, past_len)
    assert past_len % bp == 0
    n_blk = past_len // bp
    k_cache = jnp.transpose(k_cache, (0, 2, 3, 1))
    v_cache = jnp.transpose(v_cache, (0, 2, 3, 1))
    cache_spec = pl.BlockSpec((1, n_heads, hd, bp), lambda b, j: (b, 0, 0, n_blk - 1 - j))
    return pl.pallas_call(
        functools.partial(_sb_decode_kernel, t=t_len, n_heads=n_heads, bp=bp, scale=hd ** -0.5),
        out_shape=jax.ShapeDtypeStruct((bsz, t_len, n_heads * hd), BF16),
        grid=(bsz, n_blk),
        in_specs=[pl.BlockSpec((1, t_len, d3), lambda b, j: (b, 0, 0)), cache_spec,
                  pl.BlockSpec(memory_space=pl.ANY)],
        out_specs=pl.BlockSpec((1, t_len, n_heads * hd), lambda b, j: (b, 0, 0)),
        scratch_shapes=[pltpu.VMEM((n_heads * t_len, hd), F32), pltpu.VMEM((n_heads * t_len, hd), F32),
                        pltpu.VMEM((n_heads * t_len, 1), F32), pltpu.VMEM((n_heads, hd, bp), F32),
                        pltpu.SemaphoreType.DMA(())],
        compiler_params=_cparams("parallel", "arbitrary"),
        name="sb_decode",
    )(qkv, k_cache, v_cache)


def _qkv_proj_kernel(x_ref, g_ref, w_ref, wkt_ref, wvt_ref, o_ref, kt_ref, vt_ref):
    h = _rmsnorm(x_ref[0], g_ref[...]).astype(BF16)
    o_ref[0] = jnp.dot(h, w_ref[...], preferred_element_type=F32)
    nt = (((1,), (1,)), ((), ()))
    kt_ref[0] = lax.dot_general(wkt_ref[...], h, nt, preferred_element_type=F32)
    vt_ref[0] = lax.dot_general(wvt_ref[...], h, nt, preferred_element_type=F32)


def qkv_proj(x, t_valid, g, w_qkv, *, tm=256):
    bsz, t_len, d = x.shape
    tm = _row_tile(t_len, tm)
    assert t_len % tm == 0
    wkt = w_qkv[:, d:2 * d].T
    wvt = w_qkv[:, 2 * d:].T
    t_out = jax.ShapeDtypeStruct((bsz, d, t_valid), F32)
    t_spec = pl.BlockSpec((1, d, tm), lambda b, i: (b, 0, i))
    return pl.pallas_call(
        _qkv_proj_kernel,
        out_shape=(jax.ShapeDtypeStruct((bsz, t_len, 3 * d), F32), t_out, t_out),
        grid=(bsz, pl.cdiv(t_len, tm)),
        in_specs=[pl.BlockSpec((1, tm, d), lambda b, i: (b, i, 0)), _resident((1, d)), _resident((d, 3 * d)),
                  _resident((d, d)), _resident((d, d))],
        out_specs=(pl.BlockSpec((1, tm, 3 * d), lambda b, i: (b, i, 0)), t_spec, t_spec),
        compiler_params=_cparams("parallel", "parallel"),
        name="qkv_proj",
    )(x, g.reshape(1, d), w_qkv, wkt, wvt)


def sb_mixer(x, t_valid, k_past, v_past, norm_g, w_qkv):
    bsz, t_len, d = x.shape
    n_heads = d // SB_HEAD_DIM
    if k_past is None:
        qkv, k_t, v_t = qkv_proj(x, t_valid, norm_g, w_qkv)
        o = sb_attention(qkv)
        k_new = k_t.reshape(bsz, n_heads, SB_HEAD_DIM, t_valid).transpose(0, 3, 1, 2)
        v_new = v_t.reshape(bsz, n_heads, SB_HEAD_DIM, t_valid).transpose(0, 3, 1, 2)
    else:
        qkv = norm_matmul(x.reshape(bsz * t_len, d), norm_g, w_qkv).reshape(bsz, t_len, 3 * d)
        o = sb_decode(qkv, k_past, v_past)
        k_new = qkv[:, :t_valid, d:2 * d].reshape(bsz, t_valid, n_heads, SB_HEAD_DIM)
        v_new = qkv[:, :t_valid, 2 * d:].reshape(bsz, t_valid, n_heads, SB_HEAD_DIM)
    return o.reshape(bsz * t_len, d), k_new, v_new


def _run_trunk(x, t_valid, st, p):
    bsz, t_len, d = x.shape
    new = {}
    bf = lambda a: a.astype(BF16)

    def finish_layer(x, a, w_out, i, g_out=None):
        return mix_ffn(a, bf(w_out), x.reshape(bsz * t_len, d), p['norm_ffn'][i], bf(p['ffn_w_gu'][i]),
                       bf(p['ffn_w_down'][i]), g_out).reshape(bsz, t_len, d)

    a, new['ssm_conv'], new['ssm'] = mamba2_mixer(
        x, t_valid, st['ssm_conv'], st['ssm'], p['norm_mix'][0], bf(p['mb_w_in']), p['mb_conv_w'],
        p['mb_conv_b'], p['mb_dt_bias'], p['mb_a_log'], p['mb_d'], p['mb_norm'])
    x = finish_layer(x, a, p['mb_w_out'], 0)
    a, new['gdn_conv'], new['gdn'] = gdn_mixer(
        x, t_valid, st['gdn_conv'], st['gdn'], p['norm_mix'][1], bf(p['gdn_w_in']), p['gdn_conv_w'],
        p['gdn_dt_bias'], p['gdn_a_log'], p['gdn_norm'])
    x = finish_layer(x, a, p['gdn_w_out'], 1)
    a, new['rwkv_shift'], new['rwkv'] = rwkv_mixer(x, t_valid, st['rwkv_shift'], st['rwkv'], p['norm_mix'][2], p)
    x = finish_layer(x, a, p['rw_w_out'], 2)
    a, new['sb_k'], new['sb_v'] = sb_mixer(x, t_valid, st['sb_k'], st['sb_v'], p['norm_mix'][3], bf(p['sb_w_qkv']))
    return finish_layer(x, a, p['sb_w_out'], 3, p['norm_final']), new


def kernel(x_prompt, x_sample, state_ssm, state_ssm_conv, state_gdn, state_gdn_conv, state_rwkv, state_rwkv_shift, cache_sb_k, cache_sb_v, meta_tokens, norm_mix, norm_ffn, norm_final, ffn_w_gu, ffn_w_down, mb_w_in, mb_conv_w, mb_conv_b, mb_dt_bias, mb_a_log, mb_d, mb_norm, mb_w_out, gdn_w_in, gdn_conv_w, gdn_dt_bias, gdn_a_log, gdn_norm, gdn_w_out, rw_mu, rw_w_rkv, rw_w0, rw_w1, rw_w2, rw_a0, rw_a1, rw_a2, rw_g1, rw_g2, rw_k_k, rw_k_a, rw_r_k, rw_ln_w, rw_ln_b, rw_w_out, sb_w_qkv, sb_w_out):
    p = dict(
        norm_mix=norm_mix, norm_ffn=norm_ffn, norm_final=norm_final, ffn_w_gu=ffn_w_gu, ffn_w_down=ffn_w_down,
        mb_w_in=mb_w_in, mb_conv_w=mb_conv_w, mb_conv_b=mb_conv_b, mb_dt_bias=mb_dt_bias, mb_a_log=mb_a_log,
        mb_d=mb_d, mb_norm=mb_norm, mb_w_out=mb_w_out,
        gdn_w_in=gdn_w_in, gdn_conv_w=gdn_conv_w, gdn_dt_bias=gdn_dt_bias, gdn_a_log=gdn_a_log,
        gdn_norm=gdn_norm, gdn_w_out=gdn_w_out,
        rw_mu=rw_mu, rw_w_rkv=rw_w_rkv, rw_w0=rw_w0, rw_w1=rw_w1, rw_w2=rw_w2, rw_a0=rw_a0, rw_a1=rw_a1,
        rw_a2=rw_a2, rw_g1=rw_g1, rw_g2=rw_g2, rw_k_k=rw_k_k, rw_k_a=rw_k_a, rw_r_k=rw_r_k, rw_ln_w=rw_ln_w,
        rw_ln_b=rw_ln_b, rw_w_out=rw_w_out, sb_w_qkv=sb_w_qkv, sb_w_out=sb_w_out)
    bsz, seq, d = x_prompt.shape
    n_meta = meta_tokens.shape[0]
    t_valid = n_meta + seq
    t_pad = -(-t_valid // PROMPT_ROW_ALIGN) * PROMPT_ROW_ALIGN
    meta = jnp.broadcast_to(meta_tokens[None], (bsz, n_meta, d))
    x0 = jnp.concatenate([meta, x_prompt, jnp.zeros((bsz, t_pad - t_valid, d), x_prompt.dtype)], axis=1)
    fresh = dict(
        ssm=jnp.zeros((bsz,) + state_ssm.shape[1:], F32), ssm_conv=jnp.zeros((bsz,) + state_ssm_conv.shape[1:], F32),
        gdn=jnp.zeros((bsz,) + state_gdn.shape[1:], F32), gdn_conv=jnp.zeros((bsz,) + state_gdn_conv.shape[1:], F32),
        rwkv=jnp.zeros((bsz,) + state_rwkv.shape[1:], F32),
        rwkv_shift=jnp.zeros((bsz,) + state_rwkv_shift.shape[1:], F32), sb_k=None, sb_v=None)
    y_full, sp = _run_trunk(x0, t_valid, fresh, p)
    y_prompt = y_full[:, n_meta:t_valid]
    past = dict(ssm=state_ssm, ssm_conv=state_ssm_conv, gdn=state_gdn, gdn_conv=state_gdn_conv,
                rwkv=state_rwkv, rwkv_shift=state_rwkv_shift, sb_k=cache_sb_k, sb_v=cache_sb_v)
    y_sample, ss = _run_trunk(x_sample, x_sample.shape[1], past, p)
    return (y_prompt, y_sample,
            sp['ssm'], sp['ssm_conv'], sp['gdn'], sp['gdn_conv'], sp['rwkv'], sp['rwkv_shift'],
            sp['sb_k'], sp['sb_v'],
            ss['ssm'], ss['ssm_conv'], ss['gdn'], ss['gdn_conv'], ss['rwkv'], ss['rwkv_shift'],
            ss['sb_k'], ss['sb_v'])
```

```python
import functools
import math

import jax
import jax.numpy as jnp
from jax import lax
from jax.experimental import pallas as pl
from jax.experimental.pallas import tpu as pltpu

F32 = jnp.float32
BF16 = jnp.bfloat16
EPS = 1e-6
VMEM_LIMIT = 48 * 1024 * 1024
HI = lax.Precision.HIGHEST

PROMPT_ROW_ALIGN = 256
MB_HEAD_DIM = 64
MB_D_STATE = 128
MB_GROUPS = 4
GDN_DK = 128
GDN_DV = 128
RW_HEAD_DIM = 64
RW_GN_EPS = 64e-5
RW_DECAY_SCALE = 0.6065306597126334
SB_HEAD_DIM = 64
SB_LOG_CUT = -100.0
SB_BOUND_SLACK = 1.01


def _cparams(*sem):
    return pltpu.CompilerParams(dimension_semantics=sem, vmem_limit_bytes=VMEM_LIMIT)


def _dot(a, b):
    return jnp.dot(a.astype(BF16), b.astype(BF16), preferred_element_type=F32)


def _dot_nt(a, b):
    return lax.dot_general(a.astype(BF16), b.astype(BF16), (((1,), (1,)), ((), ())),
                           preferred_element_type=F32)


def _dot_tn(a, b):
    return lax.dot_general(a.astype(BF16), b.astype(BF16), (((0,), (0,)), ((), ())),
                           preferred_element_type=F32)


def _dot_hi(a, b):
    return jnp.dot(a, b, preferred_element_type=F32, precision=HI)


def _silu(x):
    return x * jax.nn.sigmoid(x)


def _softplus(x):
    return jnp.maximum(x, 0.0) + jnp.log(1.0 + jnp.exp(-jnp.abs(x)))


def _rmsnorm(x, g):
    return x * lax.rsqrt(jnp.mean(x * x, axis=-1, keepdims=True) + EPS) * g


def _row_tile(n, want):
    return n if n <= want else want


def _resident(shape):
    zeros = (0,) * len(shape)
    return pl.BlockSpec(shape, lambda *_: zeros, pipeline_mode=pl.Buffered(1))


def _norm_matmul_kernel(x_ref, g_ref, w_ref, o_ref):
    h = _rmsnorm(x_ref[...], g_ref[...]).astype(BF16)
    o_ref[...] = jnp.dot(h, w_ref[...], preferred_element_type=F32)


def norm_matmul(x, g, w, *, tm=256):
    n, d = x.shape
    m = w.shape[1]
    tm = _row_tile(n, tm)
    return pl.pallas_call(
        _norm_matmul_kernel,
        out_shape=jax.ShapeDtypeStruct((n, m), F32),
        grid=(pl.cdiv(n, tm),),
        in_specs=[pl.BlockSpec((tm, d), lambda i: (i, 0)), _resident((1, d)), _resident((d, m))],
        out_specs=pl.BlockSpec((tm, m), lambda i: (i, 0)),
        compiler_params=_cparams("parallel"),
        name="norm_matmul",
    )(x, g.reshape(1, d), w)


def _mix_ffn_kernel(a_ref, wo_ref, x_ref, g_ref, wgu_ref, wd_ref, *rest, f, tf):
    o_ref = rest[-1]
    x = x_ref[...] + jnp.dot(a_ref[...], wo_ref[...], preferred_element_type=F32)
    h = _rmsnorm(x, g_ref[...]).astype(BF16)
    acc = x
    for c in range(f // tf):
        gate = jnp.dot(h, wgu_ref[:, c * tf:(c + 1) * tf], preferred_element_type=F32)
        up = jnp.dot(h, wgu_ref[:, f + c * tf:f + (c + 1) * tf], preferred_element_type=F32)
        act = (_silu(gate) * up).astype(BF16)
        acc = acc + jnp.dot(act, wd_ref[c * tf:(c + 1) * tf, :], preferred_element_type=F32)
    if len(rest) == 2:
        acc = _rmsnorm(acc, rest[0][...])
    o_ref[...] = acc


def mix_ffn(a, w_out, x, g, w_gu, w_down, g_out=None, *, tm=512, tf=256):
    n, d = x.shape
    k = a.shape[1]
    f = w_down.shape[0]
    tm = _row_tile(n, tm)
    if f % tf:
        tf = f
    args = [a, w_out, x, g.reshape(1, d), w_gu, w_down]
    in_specs = [pl.BlockSpec((tm, k), lambda i: (i, 0)), _resident((k, d)),
                pl.BlockSpec((tm, d), lambda i: (i, 0)), _resident((1, d)), _resident((d, 2 * f)),
                _resident((f, d))]
    if g_out is not None:
        args.append(g_out.reshape(1, d))
        in_specs.append(_resident((1, d)))
    return pl.pallas_call(
        functools.partial(_mix_ffn_kernel, f=f, tf=tf),
        out_shape=jax.ShapeDtypeStruct((n, d), F32),
        grid=(pl.cdiv(n, tm),),
        in_specs=in_specs,
        out_specs=pl.BlockSpec((tm, d), lambda i: (i, 0)),
        compiler_params=_cparams("parallel"),
        name="mix_ffn",
    )(*args)


def _causal_conv_silu(raw, buf_ref, w_ref, bias, q):
    buf_ref[pl.ds(8, q), :] = raw
    out = buf_ref[pl.ds(5, q), :] * w_ref[0:1, :]
    out = out + buf_ref[pl.ds(6, q), :] * w_ref[1:2, :]
    out = out + buf_ref[pl.ds(7, q), :] * w_ref[2:3, :]
    out = out + raw * w_ref[3:4, :]
    if bias is not None:
        out = out + bias
    buf_ref[pl.ds(5, 3), :] = buf_ref[pl.ds(q + 5, 3), :]
    return _silu(out)


def _tri(q, strict=False):
    r = lax.broadcasted_iota(jnp.int32, (q, q), 0)
    c = lax.broadcasted_iota(jnp.int32, (q, q), 1)
    return (r > c) if strict else (r >= c)


def _mamba_kernel(z_ref, xs_ref, bc_ref, dt_ref, cx_ref, cbc_ref, h0_ref,
                  cwx_ref, cwbc_ref, cbx_ref, cbbc_ref, dtb_ref, a_ref, dskip_ref, nw_ref,
                  y_ref, h_ref, bufx_ref, bufbc_ref, *, q, t_len, n_heads):
    c = pl.program_id(1)
    n_state = MB_D_STATE
    hpg = n_heads // MB_GROUPS
    gw = hpg * MB_HEAD_DIM

    @pl.when(c == 0)
    def _():
        bufx_ref[pl.ds(5, 3), :] = cx_ref[0]
        bufbc_ref[pl.ds(5, 3), :] = cbc_ref[0]
        h_ref[0] = h0_ref[0]

    rows = c * q + lax.broadcasted_iota(jnp.int32, (q, 1), 0)
    valid = rows < t_len
    xs = _causal_conv_silu(jnp.where(valid, xs_ref[0], 0.0), bufx_ref, cwx_ref, cbx_ref[...], q)
    bc = _causal_conv_silu(jnp.where(valid, bc_ref[0], 0.0), bufbc_ref, cwbc_ref, cbbc_ref[...], q)

    lane = lax.broadcasted_iota(jnp.int32, (q, 128), 1)
    dt = _softplus(jnp.where(valid & (lane < n_heads), dt_ref[0], 0.0) + dtb_ref[...])
    dt = jnp.where(valid & (lane < n_heads), dt, 0.0)
    da = dt * a_ref[...]
    incl = _tri(q)
    acum = _dot_hi(incl.astype(F32), da)
    acum_t = acum.T
    dt_t = dt.T
    last = acum[q - 1:q, :]
    wts = jnp.exp(last - acum) * dt
    e_acum = jnp.exp(acum)
    e_last = jnp.exp(last)

    lo_half = lax.broadcasted_iota(jnp.int32, (q, 128), 1) < MB_HEAD_DIM
    lo_rows = lax.broadcasted_iota(jnp.int32, (128, 128), 0) < MB_HEAD_DIM

    z = z_ref[0]
    for g in range(MB_GROUPS):
        bm = bc[:, g * n_state:(g + 1) * n_state]
        cm = bc[:, (MB_GROUPS + g) * n_state:(MB_GROUPS + g + 1) * n_state]
        cb = _dot_nt(cm, bm)
        hg = h_ref[0, g * gw:(g + 1) * gw, :]
        ch = _dot_nt(cm, hg)
        ys = []
        for p in range(hpg // 2):
            h_e = g * hpg + 2 * p
            x_pair = xs[:, h_e * MB_HEAD_DIM:(h_e + 2) * MB_HEAD_DIM]
            outs = []
            for hh in (h_e, h_e + 1):
                seg = acum[:, hh:hh + 1] - acum_t[hh:hh + 1, :]
                decay = jnp.where(incl, jnp.exp(jnp.where(incl, seg, 0.0)), 0.0)
                m = cb * decay * dt_t[hh:hh + 1, :]
                outs.append(_dot(m, x_pair))
            y_pair = jnp.where(lo_half, outs[0], outs[1])
            e_pair = jnp.where(lo_half, e_acum[:, h_e:h_e + 1], e_acum[:, h_e + 1:h_e + 2])
            y_pair = y_pair + e_pair * ch[:, 2 * p * MB_HEAD_DIM:(2 * p + 2) * MB_HEAD_DIM]
            ys.append(y_pair)
            w_pair = jnp.where(lo_half, wts[:, h_e:h_e + 1], wts[:, h_e + 1:h_e + 2])
            upd = _dot_tn(x_pair * w_pair, bm)
            scale = jnp.where(lo_rows, e_last[:, h_e:h_e + 1], e_last[:, h_e + 1:h_e + 2])
            r0 = h_e * MB_HEAD_DIM
            h_ref[0, r0:r0 + 128, :] = h_ref[0, r0:r0 + 128, :] * scale + upd
        yg = jnp.concatenate(ys, axis=1)
        xg = xs[:, g * gw:(g + 1) * gw]
        yg = yg + xg * dskip_ref[:, g * gw:(g + 1) * gw]
        yg = yg * _silu(z[:, g * gw:(g + 1) * gw])
        yg = yg * lax.rsqrt(jnp.mean(yg * yg, axis=-1, keepdims=True) + EPS)
        y_ref[0, :, g * gw:(g + 1) * gw] = (yg * nw_ref[:, g * gw:(g + 1) * gw]).astype(y_ref.dtype)


def mamba_core(proj, t_valid, conv_buf, h0, conv_w, conv_b, dt_bias, a_log, d_skip, norm_w, *, chunk=128):
    bsz, t_len, _ = proj.shape
    n_heads = dt_bias.shape[0]
    d_inner = n_heads * MB_HEAD_DIM
    gn = MB_GROUPS * MB_D_STATE
    assert d_inner % 1024 == 0 and 2 * gn == 1024 and n_heads <= 128
    q = min(chunk, t_len)
    assert q % 8 == 0 and t_len % q == 0
    n_chunks = t_len // q
    dtb = jnp.zeros((1, 128), F32).at[0, :n_heads].set(dt_bias)
    a_row = jnp.zeros((1, 128), F32).at[0, :n_heads].set(-jnp.exp(a_log))
    dskip = jnp.repeat(d_skip, MB_HEAD_DIM).reshape(1, d_inner)
    kern = functools.partial(_mamba_kernel, q=q, t_len=t_valid, n_heads=n_heads)
    xblk = d_inner // 1024
    y, h = pl.pallas_call(
        kern,
        out_shape=(jax.ShapeDtypeStruct((bsz, t_len, d_inner), BF16),
                   jax.ShapeDtypeStruct((bsz, n_heads * MB_HEAD_DIM, MB_D_STATE), F32)),
        grid=(bsz, n_chunks),
        in_specs=[
            pl.BlockSpec((1, q, d_inner), lambda b, c: (b, c, 0)),
            pl.BlockSpec((1, q, d_inner), lambda b, c: (b, c, 1)),
            pl.BlockSpec((1, q, 2 * gn), lambda b, c: (b, c, 2 * xblk)),
            pl.BlockSpec((1, q, 128), lambda b, c: (b, c, (2 * d_inner + 2 * gn) // 128)),
            pl.BlockSpec((1, 3, d_inner), lambda b, c: (b, 0, 0)),
            pl.BlockSpec((1, 3, 2 * gn), lambda b, c: (b, 0, xblk)),
            pl.BlockSpec((1, n_heads * MB_HEAD_DIM, MB_D_STATE), lambda b, c: (b, 0, 0)),
            pl.BlockSpec((4, d_inner), lambda b, c: (0, 0)),
            pl.BlockSpec((4, 2 * gn), lambda b, c: (0, xblk)),
            pl.BlockSpec((1, d_inner), lambda b, c: (0, 0)),
            pl.BlockSpec((1, 2 * gn), lambda b, c: (0, xblk)),
            _resident((1, 128)), _resident((1, 128)), _resident((1, d_inner)), _resident((1, d_inner)),
        ],
        out_specs=(pl.BlockSpec((1, q, d_inner), lambda b, c: (b, c, 0)),
                   pl.BlockSpec((1, n_heads * MB_HEAD_DIM, MB_D_STATE), lambda b, c: (b, 0, 0))),
        scratch_shapes=[pltpu.VMEM((q + 8, d_inner), F32), pltpu.VMEM((q + 8, 2 * gn), F32)],
        compiler_params=_cparams("parallel", "arbitrary"),
        name="mamba_core",
    )(proj, proj, proj, proj, conv_buf, conv_buf, h0.reshape(bsz, n_heads * MB_HEAD_DIM, MB_D_STATE),
      conv_w, conv_w, conv_b.reshape(1, -1), conv_b.reshape(1, -1), dtb, a_row, dskip,
      norm_w.reshape(1, d_inner))
    return y, h.reshape(bsz, n_heads, MB_HEAD_DIM, MB_D_STATE)


def mamba2_mixer(x, t_valid, conv_buf, ssm_state, norm_g, w_in, conv_w, conv_b, dt_bias, a_log, d_skip, norm_w):
    bsz, t_len, d = x.shape
    d_inner = dt_bias.shape[0] * MB_HEAD_DIM
    proj = norm_matmul(x.reshape(bsz * t_len, d), norm_g, w_in).reshape(bsz, t_len, -1)
    y, h_new = mamba_core(proj, t_valid, conv_buf, ssm_state, conv_w, conv_b, dt_bias, a_log, d_skip, norm_w)
    conv_new = proj[:, t_valid - 3:t_valid, d_inner:d_inner + conv_w.shape[1]]
    return y.reshape(bsz * t_len, d_inner), conv_new, h_new


def _split(a):
    hi = a.astype(BF16)
    lo = (a - hi.astype(F32)).astype(BF16)
    return hi, lo


def _mm(a, b):
    return jnp.dot(a, b, preferred_element_type=F32)


def _dot3(a, b):
    ah, al = _split(a)
    bh, bl = _split(b)
    return _mm(ah, bh) + (_mm(ah, bl) + _mm(al, bh))


INV_BASE = 8


def _unit_lower_inverse(mats):
    n = mats[0].shape[0]
    r = lax.broadcasted_iota(jnp.int32, (n, n), 0)
    c = lax.broadcasted_iota(jnp.int32, (n, n), 1)
    eye = jnp.where(r == c, 1.0, 0.0)
    same = (r // INV_BASE) == (c // INV_BASE)
    diag = [jnp.where(same, a, 0.0) for a in mats]
    xs = [eye - dg for dg in diag]
    ps = diag
    k = 1
    while 2 * k < INV_BASE:
        ps = [_dot(p, p) for p in ps]
        xs = [x + _dot(x, p) for x, p in zip(xs, ps)]
        k *= 2
    w = INV_BASE
    while w < n:
        off = ((r // (2 * w)) == (c // (2 * w))) != ((r // w) == (c // w))
        ts = [_dot(jnp.where(off, a, 0.0), x) for a, x in zip(mats, xs)]
        xs = [x - _dot(x, t) for x, t in zip(xs, ts)]
        w *= 2
    return xs


def _solve_unit_lower(mats, rhs, apply):
    x_inv = _unit_lower_inverse(mats)
    d0 = apply(x_inv, rhs, _dot)
    t0 = apply(mats, d0, _dot3)
    res = [b - d - t for b, d, t in zip(rhs, d0, t0)]
    corr = apply(x_inv, res, _dot)
    return [d + e for d, e in zip(d0, corr)]


def _gdn_kernel(q_ref, k_ref, v_ref, z_ref, ba_ref, cq_ref, ck_ref, cv_ref, s0_ref,
                cwq_ref, cwk_ref, cwv_ref, dtb_ref, al_ref, nw_ref,
                o_ref, s_ref, bufq_ref, bufk_ref, bufv_ref, *, q, t_len, n_hv, n_hk):
    c = pl.program_id(1)
    rep = n_hv // n_hk

    @pl.when(c == 0)
    def _():
        bufq_ref[pl.ds(5, 3), :] = cq_ref[0]
        bufk_ref[pl.ds(5, 3), :] = ck_ref[0]
        bufv_ref[pl.ds(5, 3), :] = cv_ref[0]
        s_ref[0] = s0_ref[0]

    rows = c * q + lax.broadcasted_iota(jnp.int32, (q, 1), 0)
    valid = rows < t_len
    qc = _causal_conv_silu(jnp.where(valid, q_ref[0], 0.0), bufq_ref, cwq_ref, None, q)
    kc = _causal_conv_silu(jnp.where(valid, k_ref[0], 0.0), bufk_ref, cwk_ref, None, q)
    vc = _causal_conv_silu(jnp.where(valid, v_ref[0], 0.0), bufv_ref, cwv_ref, None, q)

    lane = lax.broadcasted_iota(jnp.int32, (q, 128), 1)
    ba = jnp.where(valid & (lane < 2 * n_hv), ba_ref[0], 0.0)
    beta = jnp.where(valid & (lane < n_hv), jax.nn.sigmoid(ba), 0.0)
    g = jnp.where(valid & (lane >= n_hv) & (lane < 2 * n_hv),
                  -jnp.exp(al_ref[...]) * _softplus(ba + dtb_ref[...]), 0.0)
    incl = _tri(q)
    strict = _tri(q, strict=True)
    gcum = _dot_hi(incl.astype(F32), g)
    gcum_t = gcum.T
    eg = jnp.exp(gcum)
    glast = gcum[q - 1:q, :]
    w_last = jnp.exp(glast - gcum)
    e_last = jnp.exp(glast)

    z = z_ref[0]
    qn, kn, kk, qk0 = [], [], [], []
    for kh in range(n_hk):
        qh = qc[:, kh * GDN_DK:(kh + 1) * GDN_DK]
        kh_ = kc[:, kh * GDN_DK:(kh + 1) * GDN_DK]
        qh = qh * (lax.rsqrt(jnp.sum(qh * qh, axis=-1, keepdims=True) + 1e-6) * (GDN_DK ** -0.5))
        kh_ = kh_ * lax.rsqrt(jnp.sum(kh_ * kh_, axis=-1, keepdims=True) + 1e-6)
        qn.append(qh)
        kn.append(kh_)
        kk.append(_dot_nt(kh_, kh_))
        qk0.append(_dot_nt(qh, kh_))
    heads = range(n_hv)
    s_old = [s_ref[0, h * GDN_DK:(h + 1) * GDN_DK, :] for h in heads]
    ks = [_dot(kn[h // rep], s_old[h]) for h in heads]
    qs = [_dot(qn[h // rep], s_old[h]) for h in heads]
    dec, a_mat, rhs = [], [], []
    for h in heads:
        hl = n_hv + h
        seg = gcum[:, hl:hl + 1] - gcum_t[hl:hl + 1, :]
        dec.append(jnp.where(incl, jnp.exp(jnp.where(incl, seg, 0.0)), 0.0))
        bt = beta[:, h:h + 1]
        a_mat.append(jnp.where(strict, kk[h // rep] * dec[h], 0.0) * bt)
        rhs.append(bt * (vc[:, h * GDN_DV:(h + 1) * GDN_DV] - eg[:, hl:hl + 1] * ks[h]))
    delta = _solve_unit_lower(a_mat, rhs, lambda ms, vs, dot: [dot(m, v) for m, v in zip(ms, vs)])
    o_mm = [_dot(qk0[h // rep] * dec[h], delta[h]) for h in heads]
    upd = [_dot_tn(kn[h // rep] * w_last[:, n_hv + h:n_hv + h + 1], delta[h]) for h in heads]
    for h in heads:
        hl = n_hv + h
        s_ref[0, h * GDN_DK:(h + 1) * GDN_DK, :] = s_old[h] * e_last[:, hl:hl + 1] + upd[h]
        o = eg[:, hl:hl + 1] * qs[h] + o_mm[h]
        o = o * lax.rsqrt(jnp.mean(o * o, axis=-1, keepdims=True) + EPS) * nw_ref[...]
        o = o * _silu(z[:, h * GDN_DV:(h + 1) * GDN_DV])
        o_ref[0, :, h * GDN_DV:(h + 1) * GDN_DV] = o.astype(o_ref.dtype)


def gdn_core(proj, t_valid, conv_buf, s0, conv_w, dt_bias, a_log, norm_w, *, chunk=64):
    bsz, t_len, width = proj.shape
    n_hv = dt_bias.shape[0]
    val_dim = n_hv * GDN_DV
    key_dim = (width - 2 * n_hv - 2 * val_dim) // 2
    n_hk = key_dim // GDN_DK
    assert val_dim == 2 * key_dim and key_dim % 128 == 0 and 2 * n_hv <= 128
    q = min(chunk, t_len)
    assert q % 8 == 0 and t_len % q == 0
    n_chunks = t_len // q
    dtb = jnp.zeros((1, 128), F32).at[0, n_hv:2 * n_hv].set(dt_bias)
    alog = jnp.zeros((1, 128), F32).at[0, n_hv:2 * n_hv].set(a_log)
    kern = functools.partial(_gdn_kernel, q=q, t_len=t_valid, n_hv=n_hv, n_hk=n_hk)
    ba_blk = (2 * key_dim + 2 * val_dim) // 128
    s0 = s0.reshape(bsz, n_hv * GDN_DK, GDN_DV)
    o, s = pl.pallas_call(
        kern,
        out_shape=(jax.ShapeDtypeStruct((bsz, t_len, val_dim), BF16),
                   jax.ShapeDtypeStruct(s0.shape, F32)),
        grid=(bsz, n_chunks),
        in_specs=[
            pl.BlockSpec((1, q, key_dim), lambda b, c: (b, c, 0)),
            pl.BlockSpec((1, q, key_dim), lambda b, c: (b, c, 1)),
            pl.BlockSpec((1, q, val_dim), lambda b, c: (b, c, 1)),
            pl.BlockSpec((1, q, val_dim), lambda b, c: (b, c, 2)),
            pl.BlockSpec((1, q, 128), lambda b, c: (b, c, ba_blk)),
            pl.BlockSpec((1, 3, key_dim), lambda b, c: (b, 0, 0)),
            pl.BlockSpec((1, 3, key_dim), lambda b, c: (b, 0, 1)),
            pl.BlockSpec((1, 3, val_dim), lambda b, c: (b, 0, 1)),
            pl.BlockSpec((1,) + s0.shape[1:], lambda b, c: (b, 0, 0)),
            pl.BlockSpec((4, key_dim), lambda b, c: (0, 0)),
            pl.BlockSpec((4, key_dim), lambda b, c: (0, 1)),
            pl.BlockSpec((4, val_dim), lambda b, c: (0, 1)),
            _resident((1, 128)), _resident((1, 128)), _resident((1, GDN_DV)),
        ],
        out_specs=(pl.BlockSpec((1, q, val_dim), lambda b, c: (b, c, 0)),
                   pl.BlockSpec((1,) + s0.shape[1:], lambda b, c: (b, 0, 0))),
        scratch_shapes=[pltpu.VMEM((q + 8, key_dim), F32), pltpu.VMEM((q + 8, key_dim), F32),
                        pltpu.VMEM((q + 8, val_dim), F32)],
        compiler_params=_cparams("parallel", "arbitrary"),
        name="gdn_core",
    )(proj, proj, proj, proj, proj, conv_buf, conv_buf, conv_buf, s0,
      conv_w, conv_w, conv_w, dtb, alog, norm_w.reshape(1, GDN_DV))
    return o, s.reshape(bsz, n_hv, GDN_DK, GDN_DV)


def gdn_mixer(x, t_valid, conv_buf, s0, norm_g, w_in, conv_w, dt_bias, a_log, norm_w):
    bsz, t_len, d = x.shape
    proj = norm_matmul(x.reshape(bsz * t_len, d), norm_g, w_in).reshape(bsz, t_len, -1)
    o, s_new = gdn_core(proj, t_valid, conv_buf, s0, conv_w, dt_bias, a_log, norm_w)
    conv_new = proj[:, t_valid - 3:t_valid, :conv_w.shape[1]]
    return o.reshape(bsz * t_len, -1), conv_new, s_new


def _head_ones(width, head):
    r = lax.broadcasted_iota(jnp.int32, (width, width), 0) // head
    c = lax.broadcasted_iota(jnp.int32, (width, width), 1) // head
    return jnp.where(r == c, 1.0, 0.0).astype(BF16)


def _head_sum(x, ones):
    hi, lo = _split(x)
    return _mm(hi, ones) + _mm(lo, ones)


def _rw_proj_kernel(x_ref, sh_ref, gn_ref, mu_ref, wr_ref, wk_ref, wv_ref, w1_ref, w2_ref, a1_ref, a2_ref,
                    g1_ref, g2_ref, w0_ref, a0_ref, kk_ref, ka_ref,
                    r_ref, lw_ref, k_ref, v_ref, kn_ref, b_ref, g_ref, shn_ref, buf_ref,
                    *, tm, state_tile, state_row):
    ti = pl.program_id(1)
    hn = _rmsnorm(x_ref[0], gn_ref[...])

    @pl.when(ti == 0)
    def _():
        buf_ref[pl.ds(7, 1), :] = sh_ref[0]

    buf_ref[pl.ds(8, tm), :] = hn
    dlt = buf_ref[pl.ds(7, tm), :] - hn

    @pl.when(ti == state_tile)
    def _():
        shn_ref[0] = buf_ref[pl.ds(8 + state_row, 1), :]

    buf_ref[pl.ds(7, 1), :] = buf_ref[pl.ds(tm + 7, 1), :]
    r_ref, lw_ref, k_ref, v_ref, kn_ref, b_ref, g_ref = (
        o.at[0] for o in (r_ref, lw_ref, k_ref, v_ref, kn_ref, b_ref, g_ref))

    def mix(c):
        return (hn + dlt * mu_ref[c:c + 1, :]).astype(BF16)

    r_ref[...] = _mm(mix(0), wr_ref[...])
    k = _mm(mix(1), wk_ref[...])
    v_ref[...] = _mm(mix(2), wv_ref[...])
    dec = w0_ref[...] + _mm(jnp.tanh(_mm(mix(3), w1_ref[...])).astype(BF16), w2_ref[...])
    lw_ref[...] = -RW_DECAY_SCALE * jax.nn.sigmoid(dec)
    a = jax.nn.sigmoid(a0_ref[...] + _mm(_mm(mix(4), a1_ref[...]).astype(BF16), a2_ref[...]))
    g_ref[...] = _mm(jax.nn.sigmoid(_mm(mix(5), g1_ref[...])).astype(BF16), g2_ref[...])
    ones = _head_ones(128, RW_HEAD_DIM)
    kn = k * kk_ref[...]
    d = kn.shape[1]
    for j in range(d // 128):
        sl = slice(j * 128, (j + 1) * 128)
        knj = kn[:, sl]
        knj = knj * lax.rsqrt(_head_sum(knj * knj, ones) + 1e-6)
        kn_ref[:, sl] = knj
        b_ref[:, sl] = knj * a[:, sl]
    k_ref[...] = k * (1.0 + (a - 1.0) * ka_ref[...])


def rw_proj(x, t_valid, shift_buf, norm_g, p, *, tm=256):
    bsz, t_len, d = x.shape
    tm = _row_tile(t_len, tm)
    assert t_len % tm == 0
    state_tile, state_row = divmod(t_valid - 1, tm)
    big = pl.BlockSpec((1, tm, d), lambda b, i: (b, i, 0))
    one_row = pl.BlockSpec((1, 1, d), lambda b, i: (b, 0, 0))
    ws = [p['rw_w_rkv'][0].astype(BF16), p['rw_w_rkv'][1].astype(BF16), p['rw_w_rkv'][2].astype(BF16),
          p['rw_w1'].astype(BF16), p['rw_w2'].astype(BF16), p['rw_a1'].astype(BF16), p['rw_a2'].astype(BF16),
          p['rw_g1'].astype(BF16), p['rw_g2'].astype(BF16)]
    vecs = [norm_g.reshape(1, d), p['rw_mu'], *ws, p['rw_w0'].reshape(1, d), p['rw_a0'].reshape(1, d),
            p['rw_k_k'].reshape(1, d), p['rw_k_a'].reshape(1, d)]
    big_out = jax.ShapeDtypeStruct((bsz, t_len, d), F32)
    return pl.pallas_call(
        functools.partial(_rw_proj_kernel, tm=tm, state_tile=state_tile, state_row=state_row),
        out_shape=(big_out,) * 7 + (jax.ShapeDtypeStruct((bsz, 1, d), F32),),
        grid=(bsz, t_len // tm),
        in_specs=[big, one_row] + [_resident(v.shape) for v in vecs],
        out_specs=(big,) * 7 + (one_row,),
        scratch_shapes=[pltpu.VMEM((tm + 8, d), F32)],
        compiler_params=_cparams("parallel", "arbitrary"),
        name="rw_proj",
    )(x, shift_buf, *vecs)


def _rw_scan_kernel(r_ref, lw_ref, k_ref, v_ref, kn_ref, b_ref, g_ref, s0_ref, rk_ref, lnw_ref, lnb_ref,
                    o_ref, s_ref, *, q, t_len):
    c = pl.program_id(1)

    @pl.when(c == 0)
    def _():
        s_ref[0] = s0_ref[0]

    rows = c * q + lax.broadcasted_iota(jnp.int32, (q, 1), 0)
    valid = rows < t_len
    incl = _tri(q)
    strict = _tri(q, strict=True)
    lw_all = jnp.where(valid, lw_ref[0], 0.0)
    cum_all = _dot_hi(incl.astype(F32), lw_all)
    ones = _head_ones(128, RW_HEAD_DIM)
    lo = lax.broadcasted_iota(jnp.int32, (q, 128), 1) < RW_HEAD_DIM
    blockdiag = ((lax.broadcasted_iota(jnp.int32, (128, 128), 0) < RW_HEAD_DIM)
                 == (lax.broadcasted_iota(jnp.int32, (128, 128), 1) < RW_HEAD_DIM))
    d = lw_all.shape[1]
    pairs = range(d // 128)
    sls = [slice(p * 128, (p + 1) * 128) for p in pairs]
    rs, ks, vs, bends, kends, s_old, cum_ends = [], [], [], [], [], [], []
    pk, pb, u0, y0 = [], [], [], []
    for p in pairs:
        sl = sls[p]
        lw = lw_all[:, sl]
        cum = cum_all[:, sl]
        cum_end = cum[q - 1:q, :]
        r = jnp.where(valid, r_ref[0, :, sl], 0.0)
        k = jnp.where(valid, k_ref[0, :, sl], 0.0)
        v = jnp.where(valid, v_ref[0, :, sl], 0.0)
        kn = jnp.where(valid, kn_ref[0, :, sl], 0.0)
        b = jnp.where(valid, b_ref[0, :, sl], 0.0)
        kq = kn * jnp.exp(cum - lw)
        rq = r * jnp.exp(cum)
        einv = jnp.exp(-cum)
        kd = k * einv
        bd = b * einv
        eend = jnp.exp(cum_end - cum)
        s2 = s_ref[0, sl, :]
        lhs = jnp.concatenate([jnp.where(lo, kq, 0.0), jnp.where(lo, 0.0, kq),
                               jnp.where(lo, rq, 0.0), jnp.where(lo, 0.0, rq)], axis=0)
        pk.append(_dot_nt(lhs, kd))
        pb.append(_dot_nt(lhs, bd))
        u0.append(_dot_nt(kq, s2))
        y0.append(_dot_nt(rq, s2))
        rs.append(r)
        ks.append(k)
        vs.append(v)
        kends.append(k * eend)
        bends.append(b * eend)
        s_old.append(s2)
        cum_ends.append(cum_end)
    ab = [jnp.where(strict, pb[p][e * q:(e + 1) * q], 0.0) for p in pairs for e in range(2)]
    rhs = [u0[p] + jnp.where(lo, _dot(jnp.where(strict, pk[p][0:q], 0.0), vs[p]),
                             _dot(jnp.where(strict, pk[p][q:2 * q], 0.0), vs[p])) for p in pairs]

    def per_head(ms, vecs, dot):
        return [jnp.where(lo, dot(ms[2 * p], vecs[p]), dot(ms[2 * p + 1], vecs[p])) for p in pairs]

    us = _solve_unit_lower(ab, rhs, per_head)
    ys = []
    for p in pairs:
        rk = [jnp.where(incl, pk[p][(2 + e) * q:(3 + e) * q], 0.0) for e in range(2)]
        rb = [jnp.where(incl, pb[p][(2 + e) * q:(3 + e) * q], 0.0) for e in range(2)]
        ys.append(y0[p] + jnp.where(lo, _dot(rk[0], vs[p]) - _dot(rb[0], us[p]),
                                    _dot(rk[1], vs[p]) - _dot(rb[1], us[p])))
    s_new = [s_old[p] * jnp.exp(cum_ends[p]) + _dot_tn(vs[p], kends[p]) - _dot_tn(us[p], bends[p])
             for p in pairs]
    for p in pairs:
        sl = sls[p]
        s_ref[0, sl, :] = jnp.where(blockdiag, s_new[p], 0.0)
        y = ys[p]
        mean = _head_sum(y, ones) * (1.0 / RW_HEAD_DIM)
        yc = y - mean
        var = _head_sum(yc * yc, ones) * (1.0 / RW_HEAD_DIM)
        yn = yc * lax.rsqrt(var + RW_GN_EPS) * lnw_ref[:, sl] + lnb_ref[:, sl]
        bonus = _head_sum(rs[p] * ks[p] * rk_ref[:, sl], ones) * vs[p]
        o_ref[0, :, sl] = ((yn + bonus) * g_ref[0, :, sl]).astype(o_ref.dtype)


def rw_scan(r, lw, k, v, kn, b, g, t_valid, s0, r_k, ln_w, ln_b, *, chunk=64):
    bsz, t_len, d = r.shape
    q = min(chunk, t_len)
    assert q % 8 == 0 and t_len % q == 0 and d % 128 == 0
    n_heads = d // RW_HEAD_DIM
    s5 = s0.reshape(bsz, n_heads // 2, 2, RW_HEAD_DIM, RW_HEAD_DIM)
    s2 = jnp.einsum('bpevk,ef->bpevfk', s5, jnp.eye(2, dtype=F32)).reshape(bsz, d, 128)
    blk = pl.BlockSpec((1, q, d), lambda bb, c: (bb, c, 0))
    vec = pl.BlockSpec((1, d), lambda bb, c: (0, 0))
    st = pl.BlockSpec((1, d, 128), lambda bb, c: (bb, 0, 0))
    o, s_new = pl.pallas_call(
        functools.partial(_rw_scan_kernel, q=q, t_len=t_valid),
        out_shape=(jax.ShapeDtypeStruct((bsz, t_len, d), BF16), jax.ShapeDtypeStruct((bsz, d, 128), F32)),
        grid=(bsz, t_len // q),
        in_specs=[blk] * 7 + [st, vec, vec, vec],
        out_specs=(blk, st),
        compiler_params=_cparams("parallel", "arbitrary"),
        name="rw_scan",
    )(r, lw, k, v, kn, b, g, s2, r_k.reshape(1, d), ln_w.reshape(1, d), ln_b.reshape(1, d))
    s6 = s_new.reshape(bsz, n_heads // 2, 2, RW_HEAD_DIM, 2, RW_HEAD_DIM)
    s_out = jnp.stack([s6[:, :, 0, :, 0, :], s6[:, :, 1, :, 1, :]], axis=2)
    return o, s_out.reshape(bsz, n_heads, RW_HEAD_DIM, RW_HEAD_DIM)


def rwkv_mixer(x, t_valid, shift_buf, s0, norm_g, p):
    bsz, t_len, d = x.shape
    r, lw, k, v, kn, b, g, shift_new = rw_proj(x, t_valid, shift_buf, norm_g, p)
    yg, s_new = rw_scan(r, lw, k, v, kn, b, g, t_valid, s0, p['rw_r_k'], p['rw_ln_w'], p['rw_ln_b'])
    return yg.reshape(bsz * t_len, d), shift_new, s_new


def _dot3_nt(a, b):
    ah, al = _split(a)
    bh, bl = _split(b)
    dn = (((1,), (1,)), ((), ()))
    f = lambda u, w: lax.dot_general(u, w, dn, preferred_element_type=F32)
    return f(ah, bh) + (f(ah, bl) + f(al, bh))


def _suffix_ones(n):
    r = lax.broadcasted_iota(jnp.int32, (n, n), 0)
    c = lax.broadcasted_iota(jnp.int32, (n, n), 1)
    return jnp.where(r >= c, 1.0, 0.0).astype(BF16)


def _sb_block(q_e, kblk, vblk, carry, acc, lo, suffix, mask):
    outs = []
    new_carry = []
    for e in range(2):
        z = _dot3_nt(q_e[e], kblk)
        lnb = -_softplus(z)
        if mask is not None:
            lnb = jnp.where(mask, lnb, 0.0)
        hi, lw = _split(lnb)
        rsum = _mm(hi, suffix) + _mm(lw, suffix) + carry[e]
        att = jnp.exp(z + rsum)
        if mask is not None:
            att = jnp.where(mask, att, 0.0)
        outs.append(_dot(att, vblk))
        new_carry.append(rsum[:, 0:1])
    return new_carry, acc + jnp.where(lo, outs[0], outs[1])


def _sb_kernel(q_ref, kn_ref, vn_ref, o_ref, kmax_ref, *, bq, n_pairs, scale):
    i = pl.program_id(2)
    t_new = kn_ref.shape[1]
    width = 128 * n_pairs
    pairs = range(n_pairs)

    @pl.when(i == 0)
    def _():
        def body(j, m):
            rows = kn_ref[0, pl.ds(pl.multiple_of(j * bq, bq), bq), :]
            return jnp.maximum(m, jnp.max(jnp.abs(rows).reshape(bq // 8, 8, width), axis=0))

        m8 = lax.fori_loop(0, t_new // bq, body, jnp.zeros((8, width), F32))
        kmax_ref[...] = jnp.max(m8, axis=0, keepdims=True)

    lo = lax.broadcasted_iota(jnp.int32, (bq, 128), 1) < SB_HEAD_DIM
    ones = _head_ones(128, SB_HEAD_DIM)
    q_e, zbs = [], []
    for p in pairs:
        q = q_ref[0, :, p * 128:(p + 1) * 128] * scale
        q_e.append([jnp.where(lo, q, 0.0), jnp.where(lo, 0.0, q)])
        zb = _head_sum(jnp.abs(q) * kmax_ref[:, p * 128:(p + 1) * 128], ones)
        zbs += [zb[:, 0:1], zb[:, SB_HEAD_DIM:SB_HEAD_DIM + 1]]
    suffix = _suffix_ones(bq)

    def block(off, carries, accs, mask):
        new_c, new_a = [], []
        for p in pairs:
            c, a = _sb_block(q_e[p], kn_ref[0, pl.ds(off, bq), p * 128:(p + 1) * 128],
                             vn_ref[0, pl.ds(off, bq), p * 128:(p + 1) * 128],
                             carries[2 * p:2 * p + 2], accs[p], lo, suffix, mask)
            new_c += c
            new_a.append(a)
        return new_c, new_a

    def live(carries):
        worst = carries[0] + zbs[0]
        for c, zb in zip(carries[1:], zbs[1:]):
            worst = jnp.maximum(worst, c + zb)
        return (jnp.max(worst) > SB_LOG_CUT).astype(jnp.int32)

    zero_c = jnp.zeros((bq, 1), F32)
    carries, accs = block(pl.multiple_of(i * bq, bq), [zero_c] * (2 * n_pairs),
                          [jnp.zeros((bq, 128), F32)] * n_pairs, _tri(bq, strict=True))

    def cond(s):
        return (s[0] < i) & (s[1] > 0)

    def body(s):
        jj, _, cs, acs = s
        cs, acs = block(pl.multiple_of((i - 1 - jj) * bq, bq), list(cs), list(acs), None)
        return jj + 1, live(cs), tuple(cs), tuple(acs)

    st = lax.while_loop(cond, body, (jnp.int32(0), live(carries), tuple(carries), tuple(accs)))
    for p in pairs:
        o_ref[0, :, p * 128:(p + 1) * 128] = st[3][p].astype(o_ref.dtype)


def sb_attention(qkv, *, block=256, pairs_per_step=2):
    bsz, t_len, d3 = qkv.shape
    d = d3 // 3
    width = 128 * pairs_per_step
    assert d % width == 0
    n_groups = d // width
    bq = min(block, t_len)
    assert t_len % bq == 0 and bq % 8 == 0

    def kv_spec(which):
        return pl.BlockSpec((1, t_len, width), lambda b, p, i: (b, 0, which * n_groups + p),
                            pipeline_mode=pl.Buffered(1))

    return pl.pallas_call(
        functools.partial(_sb_kernel, bq=bq, n_pairs=pairs_per_step, scale=SB_HEAD_DIM ** -0.5),
        out_shape=jax.ShapeDtypeStruct((bsz, t_len, d), BF16),
        grid=(bsz, n_groups, t_len // bq),
        in_specs=[pl.BlockSpec((1, bq, width), lambda b, p, i: (b, i, p)), kv_spec(1), kv_spec(2)],
        out_specs=pl.BlockSpec((1, bq, width), lambda b, p, i: (b, i, p)),
        compiler_params=_cparams("parallel", "parallel", "arbitrary"),
        scratch_shapes=[pltpu.VMEM((1, width), F32)],
        name="sb_attention",
    )(qkv, qkv, qkv)


def _sb_decode_kernel(qkv_ref, kc_ref, vc_hbm, o_ref, q_s, acc_s, carry_s, v_buf, v_sem, *, t, n_heads, bp, scale):
    j = pl.program_id(1)
    hd = SB_HEAD_DIM
    d = n_heads * hd
    rows = n_heads * t
    heads = range(n_heads)

    def scores(keys):
        return jnp.concatenate([_dot3_nt(q_s[h * t:(h + 1) * t, :], keys(h)) for h in heads], axis=0)

    def weights(z, carry, suffix, mask):
        lnb = -_softplus(z)
        if mask is not None:
            lnb = jnp.where(mask, lnb, 0.0)
        hi, lw = _split(lnb)
        rsum = _mm(hi, suffix) + _mm(lw, suffix) + carry
        att = jnp.exp(z + rsum)
        if mask is not None:
            att = jnp.where(mask, att, 0.0)
        return att, rsum[:, 0:1]

    @pl.when(j == 0)
    def _():
        for h in heads:
            q_s[h * t:(h + 1) * t, :] = qkv_ref[0, :, h * hd:(h + 1) * hd] * scale
        z = scores(lambda h: qkv_ref[0, :, d + h * hd:d + (h + 1) * hd])
        qi = lax.rem(lax.broadcasted_iota(jnp.int32, (rows, t), 0), t)
        mask = lax.broadcasted_iota(jnp.int32, (rows, t), 1) < qi
        att, carry = weights(z, 0.0, _suffix_ones(t), mask)
        for h in heads:
            acc_s[h * t:(h + 1) * t, :] = _dot(att[h * t:(h + 1) * t],
                                               qkv_ref[0, :, 2 * d + h * hd:2 * d + (h + 1) * hd])
        carry_s[...] = carry

    ub = jnp.concatenate([_dot(jnp.abs(q_s[h * t:(h + 1) * t, :]), jnp.abs(kc_ref[0, h])) for h in heads], axis=0)
    zb = SB_BOUND_SLACK * jnp.max(ub, axis=-1, keepdims=True)

    @pl.when(jnp.max(carry_s[...] + zb) > SB_LOG_CUT)
    def _():
        off = pl.multiple_of((pl.num_programs(1) - 1 - j) * bp, bp)
        v_copy = pltpu.make_async_copy(vc_hbm.at[pl.program_id(0), :, :, pl.ds(off, bp)], v_buf, v_sem)
        v_copy.start()
        z = jnp.concatenate([_dot3(q_s[h * t:(h + 1) * t, :], kc_ref[0, h]) for h in heads], axis=0)
        att, carry = weights(z, carry_s[...], _suffix_ones(bp), None)
        v_copy.wait()
        for h in heads:
            acc_s[h * t:(h + 1) * t, :] += _dot_nt(att[h * t:(h + 1) * t], v_buf[h])
        carry_s[...] = carry

    @pl.when(j == pl.num_programs(1) - 1)
    def _():
        o_ref[0] = jnp.concatenate([acc_s[h * t:(h + 1) * t, :] for h in heads], axis=1).astype(o_ref.dtype)


def sb_decode(qkv, k_cache, v_cache, *, block=512):
    bsz, t_len, d3 = qkv.shape
    _, past_len, n_heads, hd = k_cache.shape
    assert hd == SB_HEAD_DIM and n_heads * hd * 3 == d3 and t_len % 8 == 0
    bp = min(block, past_len)
    assert past_len % bp == 0
    n_blk = past_len // bp
    k_cache = jnp.transpose(k_cache, (0, 2, 3, 1))
    v_cache = jnp.transpose(v_cache, (0, 2, 3, 1))
    cache_spec = pl.BlockSpec((1, n_heads, hd, bp), lambda b, j: (b, 0, 0, n_blk - 1 - j))
    return pl.pallas_call(
        functools.partial(_sb_decode_kernel, t=t_len, n_heads=n_heads, bp=bp, scale=hd ** -0.5),
        out_shape=jax.ShapeDtypeStruct((bsz, t_len, n_heads * hd), BF16),
        grid=(bsz, n_blk),
        in_specs=[pl.BlockSpec((1, t_len, d3), lambda b, j: (b, 0, 0)), cache_spec,
                  pl.BlockSpec(memory_space=pl.ANY)],
        out_specs=pl.BlockSpec((1, t_len, n_heads * hd), lambda b, j: (b, 0, 0)),
        scratch_shapes=[pltpu.VMEM((n_heads * t_len, hd), F32), pltpu.VMEM((n_heads * t_len, hd), F32),
                        pltpu.VMEM((n_heads * t_len, 1), F32), pltpu.VMEM((n_heads, hd, bp), F32),
                        pltpu.SemaphoreType.DMA(())],
        compiler_params=_cparams("parallel", "arbitrary"),
        name="sb_decode",
    )(qkv, k_cache, v_cache)


def _qkv_proj_kernel(x_ref, g_ref, w_ref, wkt_ref, wvt_ref, o_ref, kt_ref, vt_ref):
    h = _rmsnorm(x_ref[0], g_ref[...]).astype(BF16)
    o_ref[0] = jnp.dot(h, w_ref[...], preferred_element_type=F32)
    nt = (((1,), (1,)), ((), ()))
    kt_ref[0] = lax.dot_general(wkt_ref[...], h, nt, preferred_element_type=F32)
    vt_ref[0] = lax.dot_general(wvt_ref[...], h, nt, preferred_element_type=F32)


def qkv_proj(x, t_valid, g, w_qkv, *, tm=256):
    bsz, t_len, d = x.shape
    tm = _row_tile(t_len, tm)
    assert t_len % tm == 0
    wkt = w_qkv[:, d:2 * d].T
    wvt = w_qkv[:, 2 * d:].T
    t_out = jax.ShapeDtypeStruct((bsz, d, t_valid), F32)
    t_spec = pl.BlockSpec((1, d, tm), lambda b, i: (b, 0, i))
    return pl.pallas_call(
        _qkv_proj_kernel,
        out_shape=(jax.ShapeDtypeStruct((bsz, t_len, 3 * d), F32), t_out, t_out),
        grid=(bsz, pl.cdiv(t_len, tm)),
        in_specs=[pl.BlockSpec((1, tm, d), lambda b, i: (b, i, 0)), _resident((1, d)), _resident((d, 3 * d)),
                  _resident((d, d)), _resident((d, d))],
        out_specs=(pl.BlockSpec((1, tm, 3 * d), lambda b, i: (b, i, 0)), t_spec, t_spec),
        compiler_params=_cparams("parallel", "parallel"),
        name="qkv_proj",
    )(x, g.reshape(1, d), w_qkv, wkt, wvt)


def sb_mixer(x, t_valid, k_past, v_past, norm_g, w_qkv):
    bsz, t_len, d = x.shape
    n_heads = d // SB_HEAD_DIM
    if k_past is None:
        qkv, k_t, v_t = qkv_proj(x, t_valid, norm_g, w_qkv)
        o = sb_attention(qkv)
        k_new = k_t.reshape(bsz, n_heads, SB_HEAD_DIM, t_valid).transpose(0, 3, 1, 2)
        v_new = v_t.reshape(bsz, n_heads, SB_HEAD_DIM, t_valid).transpose(0, 3, 1, 2)
    else:
        qkv = norm_matmul(x.reshape(bsz * t_len, d), norm_g, w_qkv).reshape(bsz, t_len, 3 * d)
        o = sb_decode(qkv, k_past, v_past)
        k_new = qkv[:, :t_valid, d:2 * d].reshape(bsz, t_valid, n_heads, SB_HEAD_DIM)
        v_new = qkv[:, :t_valid, 2 * d:].reshape(bsz, t_valid, n_heads, SB_HEAD_DIM)
    return o.reshape(bsz * t_len, d), k_new, v_new


def _run_trunk(x, t_valid, st, p):
    bsz, t_len, d = x.shape
    new = {}
    bf = lambda a: a.astype(BF16)

    def finish_layer(x, a, w_out, i, g_out=None):
        return mix_ffn(a, bf(w_out), x.reshape(bsz * t_len, d), p['norm_ffn'][i], bf(p['ffn_w_gu'][i]),
                       bf(p['ffn_w_down'][i]), g_out).reshape(bsz, t_len, d)

    a, new['ssm_conv'], new['ssm'] = mamba2_mixer(
        x, t_valid, st['ssm_conv'], st['ssm'], p['norm_mix'][0], bf(p['mb_w_in']), p['mb_conv_w'],
        p['mb_conv_b'], p['mb_dt_bias'], p['mb_a_log'], p['mb_d'], p['mb_norm'])
    x = finish_layer(x, a, p['mb_w_out'], 0)
    a, new['gdn_conv'], new['gdn'] = gdn_mixer(
        x, t_valid, st['gdn_conv'], st['gdn'], p['norm_mix'][1], bf(p['gdn_w_in']), p['gdn_conv_w'],
        p['gdn_dt_bias'], p['gdn_a_log'], p['gdn_norm'])
    x = finish_layer(x, a, p['gdn_w_out'], 1)
    a, new['rwkv_shift'], new['rwkv'] = rwkv_mixer(x, t_valid, st['rwkv_shift'], st['rwkv'], p['norm_mix'][2], p)
    x = finish_layer(x, a, p['rw_w_out'], 2)
    a, new['sb_k'], new['sb_v'] = sb_mixer(x, t_valid, st['sb_k'], st['sb_v'], p['norm_mix'][3], bf(p['sb_w_qkv']))
    return finish_layer(x, a, p['sb_w_out'], 3, p['norm_final']), new


def kernel(x_prompt, x_sample, state_ssm, state_ssm_conv, state_gdn, state_gdn_conv, state_rwkv, state_rwkv_shift, cache_sb_k, cache_sb_v, meta_tokens, norm_mix, norm_ffn, norm_final, ffn_w_gu, ffn_w_down, mb_w_in, mb_conv_w, mb_conv_b, mb_dt_bias, mb_a_log, mb_d, mb_norm, mb_w_out, gdn_w_in, gdn_conv_w, gdn_dt_bias, gdn_a_log, gdn_norm, gdn_w_out, rw_mu, rw_w_rkv, rw_w0, rw_w1, rw_w2, rw_a0, rw_a1, rw_a2, rw_g1, rw_g2, rw_k_k, rw_k_a, rw_r_k, rw_ln_w, rw_ln_b, rw_w_out, sb_w_qkv, sb_w_out):
    p = dict(
        norm_mix=norm_mix, norm_ffn=norm_ffn, norm_final=norm_final, ffn_w_gu=ffn_w_gu, ffn_w_down=ffn_w_down,
        mb_w_in=mb_w_in, mb_conv_w=mb_conv_w, mb_conv_b=mb_conv_b, mb_dt_bias=mb_dt_bias, mb_a_log=mb_a_log,
        mb_d=mb_d, mb_norm=mb_norm, mb_w_out=mb_w_out,
        gdn_w_in=gdn_w_in, gdn_conv_w=gdn_conv_w, gdn_dt_bias=gdn_dt_bias, gdn_a_log=gdn_a_log,
        gdn_norm=gdn_norm, gdn_w_out=gdn_w_out,
        rw_mu=rw_mu, rw_w_rkv=rw_w_rkv, rw_w0=rw_w0, rw_w1=rw_w1, rw_w2=rw_w2, rw_a0=rw_a0, rw_a1=rw_a1,
        rw_a2=rw_a2, rw_g1=rw_g1, rw_g2=rw_g2, rw_k_k=rw_k_k, rw_k_a=rw_k_a, rw_r_k=rw_r_k, rw_ln_w=rw_ln_w,
        rw_ln_b=rw_ln_b, rw_w_out=rw_w_out, sb_w_qkv=sb_w_qkv, sb_w_out=sb_w_out)
    bsz, seq, d = x_prompt.shape
    n_meta = meta_tokens.shape[0]
    t_valid = n_meta + seq
    t_pad = -(-t_valid // PROMPT_ROW_ALIGN) * PROMPT_ROW_ALIGN
    meta = jnp.broadcast_to(meta_tokens[None], (bsz, n_meta, d))
    x0 = jnp.concatenate([meta, x_prompt, jnp.zeros((bsz, t_pad - t_valid, d), x_prompt.dtype)], axis=1)
    fresh = dict(
        ssm=jnp.zeros((bsz,) + state_ssm.shape[1:], F32), ssm_conv=jnp.zeros((bsz,) + state_ssm_conv.shape[1:], F32),
        gdn=jnp.zeros((bsz,) + state_gdn.shape[1:], F32), gdn_conv=jnp.zeros((bsz,) + state_gdn_conv.shape[1:], F32),
        rwkv=jnp.zeros((bsz,) + state_rwkv.shape[1:], F32),
        rwkv_shift=jnp.zeros((bsz,) + state_rwkv_shift.shape[1:], F32), sb_k=None, sb_v=None)
    y_full, sp = _run_trunk(x0, t_valid, fresh, p)
    y_prompt = y_full[:, n_meta:t_valid]
    past = dict(ssm=state_ssm, ssm_conv=state_ssm_conv, gdn=state_gdn, gdn_conv=state_gdn_conv,
                rwkv=state_rwkv, rwkv_shift=state_rwkv_shift, sb_k=cache_sb_k, sb_v=cache_sb_v)
    y_sample, ss = _run_trunk(x_sample, x_sample.shape[1], past, p)
    return (y_prompt, y_sample,
            sp['ssm'], sp['ssm_conv'], sp['gdn'], sp['gdn_conv'], sp['rwkv'], sp['rwkv_shift'],
            sp['sb_k'], sp['sb_v'],
            ss['ssm'], ss['ssm_conv'], ss['gdn'], ss['gdn_conv'], ss['rwkv'], ss['rwkv_shift'],
            ss['sb_k'], ss['sb_v'])
```
